```python
import math
import jax, jax.numpy as jnp
from jax import lax
import numpy as np

D_MODEL = 1024
BATCH = 8
SEQ = 2048
DEPTH = 1

SSM_WIDTH = D_MODEL // 2
SSM_GROUP = 16
SSM_GROUPS = SSM_WIDTH // SSM_GROUP
SSM_STATE = 64
N_HEADS = 8
HEAD_DIM = 64
ATTN_WIDTH = N_HEADS * HEAD_DIM
KV_DIM = HEAD_DIM
IDX_HEADS = 4
IDX_DIM = 64
INDEX_TOPK = 256
Q_BLOCK = 128
N_BRANCH = 2
D_FF = -(-8 * D_MODEL // (3 * 256)) * 256
RMS_EPS = 1e-6
DT_MIN = 1e-3
DT_MAX = 1e-1

IN_SIZES = (SSM_WIDTH, ATTN_WIDTH, KV_DIM, KV_DIM, IDX_HEADS * IDX_DIM, IDX_DIM, IDX_HEADS, N_BRANCH * D_MODEL)
IN_COLS = sum(IN_SIZES)
IN_SPLITS = tuple(int(s) for s in np.cumsum(IN_SIZES)[:-1])

kernel_name = "hybrid_s5_dsa_gated_block"


def _rms(x, g, eps=RMS_EPS):
    xf = x.astype(jnp.float32)
    y = xf * lax.rsqrt(jnp.mean(xf * xf, axis=-1, keepdims=True) + eps)
    return (y * g.astype(jnp.float32)).astype(x.dtype)


def _complex_combine(e1, e2):
    a1r, a1i, b1r, b1i = e1
    a2r, a2i, b2r, b2i = e2
    return (a2r * a1r - a2i * a1i,
            a2r * a1i + a2i * a1r,
            a2r * b1r - a2i * b1i + b2r,
            a2r * b1i + a2i * b1r + b2i)


def _s5_branch(u, A_re, A_im, log_dt, B_re, B_im, C_re, C_im, D_skip, w_glu, b_glu):
    bsz, L, _ = u.shape
    uf = u.astype(jnp.float32).reshape(bsz, L, SSM_GROUPS, SSM_GROUP)
    ar = A_re.astype(jnp.float32)
    ai = A_im.astype(jnp.float32)
    dt = jnp.exp(log_dt.astype(jnp.float32))[:, None]
    mag = jnp.exp(ar * dt)
    abar_r = mag * jnp.cos(ai * dt)
    abar_i = mag * jnp.sin(ai * dt)
    den = ar * ar + ai * ai
    nr = abar_r - 1.0
    coef_r = (nr * ar + abar_i * ai) / den
    coef_i = (abar_i * ar - nr * ai) / den
    bu_r = jnp.einsum('gnp,blgp->blgn', B_re.astype(jnp.float32), uf)
    bu_i = jnp.einsum('gnp,blgp->blgn', B_im.astype(jnp.float32), uf)
    b_r = coef_r * bu_r - coef_i * bu_i
    b_i = coef_r * bu_i + coef_i * bu_r
    a_r = jnp.broadcast_to(abar_r, (1, L, SSM_GROUPS, SSM_STATE))
    a_i = jnp.broadcast_to(abar_i, (1, L, SSM_GROUPS, SSM_STATE))
    _, _, s_r, s_i = lax.associative_scan(_complex_combine, (a_r, a_i, b_r, b_i), axis=1)
    y = (jnp.einsum('gpn,blgn->blgp', C_re.astype(jnp.float32), s_r)
         - jnp.einsum('gpn,blgn->blgp', C_im.astype(jnp.float32), s_i)
         + D_skip.astype(jnp.float32).reshape(SSM_GROUPS, SSM_GROUP) * uf)
    y = jax.nn.gelu(y.reshape(bsz, L, SSM_WIDTH))
    y = y * jax.nn.sigmoid(y @ w_glu.astype(jnp.float32) + b_glu.astype(jnp.float32))
    return y.astype(u.dtype)


def _dsa_branch(q, k, v, qi, ki, wi):
    bsz, L = q.shape[0], q.shape[1]
    k_sel = min(INDEX_TOPK, L // 4)
    nb = L // Q_BLOCK
    q_blk = q.reshape(bsz, nb, Q_BLOCK, N_HEADS, HEAD_DIM).transpose(1, 0, 2, 3, 4)
    qi_blk = qi.reshape(bsz, nb, Q_BLOCK, IDX_HEADS, IDX_DIM).transpose(1, 0, 2, 3, 4)
    wi_blk = wi.reshape(bsz, nb, Q_BLOCK, IDX_HEADS).transpose(1, 0, 2, 3)
    starts = jnp.arange(nb, dtype=jnp.int32) * Q_BLOCK
    kif = ki.astype(jnp.float32)
    key_pos = jnp.arange(L, dtype=jnp.int32)
    w_scale = (IDX_HEADS ** -0.5) * (IDX_DIM ** -0.5)
    a_scale = HEAD_DIM ** -0.5
    gather = jax.vmap(lambda kk, ii: kk[ii])

    def block(args):
        qb, qib, wib, start = args
        pos_q = start + jnp.arange(Q_BLOCK, dtype=jnp.int32)
        visible = key_pos[None, :] <= pos_q[:, None]
        rel = jax.nn.relu(jnp.einsum('bqhd,bsd->bqhs', qib.astype(jnp.float32), kif))
        iscore = jnp.einsum('bqhs,bqh->bqs', rel, wib.astype(jnp.float32) * w_scale)
        iscore = jnp.where(visible[None], iscore, -jnp.inf)
        _, idx = lax.top_k(iscore, k_sel)
        valid = idx <= pos_q[None, :, None]
        kg = gather(k, idx).astype(jnp.float32)
        vg = gather(v, idx).astype(jnp.float32)
        logits = jnp.einsum('bqhd,bqkd->bhqk', qb.astype(jnp.float32), kg) * a_scale
        logits = jnp.where(valid[:, None], logits, -jnp.inf)
        p = jax.nn.softmax(logits, axis=-1)
        o = jnp.einsum('bhqk,bqkd->bqhd', p, vg)
        return o.astype(q.dtype)

    out = lax.map(block, (q_blk, qi_blk, wi_blk, starts))
    return out.transpose(1, 0, 2, 3, 4).reshape(bsz, L, ATTN_WIDTH)


def setup_inputs(seed: int = 0) -> dict:
    key = jax.random.key(seed)
    ks = jax.random.split(key, 24)
    f32 = jnp.float32
    nrm = lambda k, shape, s: jax.random.normal(k, shape, f32) * s
    gain = lambda k, shape: 1.0 + 0.02 * jax.random.normal(k, shape, f32)
    Ld = DEPTH
    G, N, P = SSM_GROUPS, SSM_STATE, SSM_GROUP
    a_im = jnp.broadcast_to(math.pi * jnp.arange(N, dtype=f32), (Ld, G, N)) + 0.01 * jax.random.normal(ks[4], (Ld, G, N), f32)
    log_dt = jax.random.uniform(ks[5], (Ld, G), f32, math.log(DT_MIN), math.log(DT_MAX))
    return {
        "x": jax.random.normal(ks[0], (BATCH, SEQ, D_MODEL), f32),
        "norm1_g": gain(ks[1], (Ld, D_MODEL)),
        "w_in": nrm(ks[2], (Ld, D_MODEL, IN_COLS), D_MODEL ** -0.5),
        "A_re": -0.5 + 0.01 * jax.random.normal(ks[3], (Ld, G, N), f32),
        "A_im": a_im,
        "log_dt": log_dt,
        "B_re": nrm(ks[6], (Ld, G, N, P), (2 * P) ** -0.5),
        "B_im": nrm(ks[7], (Ld, G, N, P), (2 * P) ** -0.5),
        "C_re": nrm(ks[8], (Ld, G, P, N), (2 * N) ** -0.5),
        "C_im": nrm(ks[9], (Ld, G, P, N), (2 * N) ** -0.5),
        "D_skip": nrm(ks[10], (Ld, SSM_WIDTH), 1.0),
        "w_glu": nrm(ks[11], (Ld, SSM_WIDTH, SSM_WIDTH), SSM_WIDTH ** -0.5),
        "b_glu": nrm(ks[12], (Ld, SSM_WIDTH), 0.01),
        "q_norm_g": gain(ks[13], (Ld, HEAD_DIM)),
        "k_norm_g": gain(ks[14], (Ld, KV_DIM)),
        "idx_k_norm_g": gain(ks[15], (Ld, IDX_DIM)),
        "w_proj_ssm": nrm(ks[16], (Ld, SSM_WIDTH, D_MODEL), SSM_WIDTH ** -0.5),
        "w_proj_attn": nrm(ks[17], (Ld, ATTN_WIDTH, D_MODEL), ATTN_WIDTH ** -0.5),
        "w_out": nrm(ks[18], (Ld, D_MODEL, D_MODEL), D_MODEL ** -0.5),
        "norm2_g": gain(ks[19], (Ld, D_MODEL)),
        "w_ffn_gate": nrm(ks[20], (Ld, D_MODEL, D_FF), D_MODEL ** -0.5),
        "w_ffn_up": nrm(ks[21], (Ld, D_MODEL, D_FF), D_MODEL ** -0.5),
        "w_ffn_down": nrm(ks[22], (Ld, D_FF, D_MODEL), D_FF ** -0.5),
    }


def reference(x, norm1_g, w_in, A_re, A_im, log_dt, B_re, B_im, C_re, C_im, D_skip, w_glu, b_glu,
              q_norm_g, k_norm_g, idx_k_norm_g, w_proj_ssm, w_proj_attn, w_out,
              norm2_g, w_ffn_gate, w_ffn_up, w_ffn_down):
    bsz, L, _ = x.shape
    for l in range(DEPTH):
        h = _rms(x, norm1_g[l])
        proj = h @ w_in[l]
        u, q, k, v, qi, ki, wi, gates = jnp.split(proj, IN_SPLITS, axis=-1)
        y_ssm = _s5_branch(u, A_re[l], A_im[l], log_dt[l], B_re[l], B_im[l], C_re[l], C_im[l],
                           D_skip[l], w_glu[l], b_glu[l])
        q = _rms(q.reshape(bsz, L, N_HEADS, HEAD_DIM), q_norm_g[l])
        k = _rms(k, k_norm_g[l])
        ki = _rms(ki, idx_k_norm_g[l])
        qi = qi.reshape(bsz, L, IDX_HEADS, IDX_DIM)
        y_att = _dsa_branch(q, k, v, qi, ki, wi)
        g_ssm, g_att = jnp.split(gates, N_BRANCH, axis=-1)
        merged = (jax.nn.sigmoid(g_ssm) * (y_ssm @ w_proj_ssm[l])
                  + jax.nn.sigmoid(g_att) * (y_att @ w_proj_attn[l]))
        x = x + merged @ w_out[l]
        h2 = _rms(x, norm2_g[l])
        x = x + (jax.nn.silu(h2 @ w_ffn_gate[l]) * (h2 @ w_ffn_up[l])) @ w_ffn_down[l]
    return x
```

```python
import functools
import math

import jax
import jax.numpy as jnp
from jax import lax
from jax.experimental import pallas as pl
from jax.experimental.pallas import tpu as pltpu

F32 = jnp.float32
BF16 = jnp.bfloat16

D_MODEL = 1024
SSM_WIDTH = 512
SSM_GROUP = 16
SSM_GROUPS = 32
SSM_STATE = 64
N_HEADS = 8
HEAD_DIM = 64
ATTN_WIDTH = 512
IDX_HEADS = 4
IDX_DIM = 64
INDEX_TOPK = 256
D_FF = 2816
RMS_EPS = 1e-6

LANES = 128
SUBLANES = 8
VMEM_LIMIT = 56 * 1024 * 1024

TL_IN = 512
TC_SCAN = 64
QB = 128
TK = 256
TL_FFN = 256

W1_COLS = 1536
NEG_BIG = -1e30


def _dot(a, b):
    return jnp.dot(a, b, preferred_element_type=F32)


def _dot_nt(a, b):
    return lax.dot_general(a, b, (((1,), (1,)), ((), ())), preferred_element_type=F32)


def _rms(x, g):
    return x * lax.rsqrt(jnp.mean(x * x, axis=-1, keepdims=True) + RMS_EPS) * g


def _disc_kernel(are_ref, aim_ref, ldt_ref, bre_ref, bim_ref, abr_ref, abi_ref, bpr_ref, bpi_ref):
    ar = are_ref[...]
    ai = aim_ref[...]
    dt = jnp.exp(ldt_ref[...])
    mag = jnp.exp(ar * dt)
    abar_r = mag * jnp.cos(ai * dt)
    abar_i = mag * jnp.sin(ai * dt)
    den = ar * ar + ai * ai
    nr = abar_r - 1.0
    coef_r = (nr * ar + abar_i * ai) / den
    coef_i = (abar_i * ar - nr * ai) / den
    abr_ref[...] = abar_r
    abi_ref[...] = abar_i
    br = bre_ref[...]
    bi = bim_ref[...]
    bpr_ref[...] = coef_r * br - coef_i * bi
    bpi_ref[...] = coef_r * bi + coef_i * br


def _inproj_kernel(x_ref, g1_ref, w1_ref, qg_ref, kg_ref, ikg_ref,
                   u_ref, q_ref, qs_ref, sg_ref, k_ref, ki_ref, v_ref):
    x = x_ref[0]
    h = _rms(x, g1_ref[...]).astype(BF16)
    proj = _dot(h, w1_ref[...])
    u_ref[...] = proj[:, 0:512]
    qscale = (HEAD_DIM ** -0.5) * math.log2(math.e)
    qg = qg_ref[...]
    for hd in range(N_HEADS):
        qh = proj[:, 512 + hd * HEAD_DIM: 512 + (hd + 1) * HEAD_DIM]
        q_ref[0, hd] = (_rms(qh, qg) * qscale).astype(BF16)
    wi = proj[:, 1408:1412]
    w_scale = (IDX_HEADS ** -0.5) * (IDX_DIM ** -0.5)
    wabs = jnp.abs(wi) * w_scale
    for hd in range(IDX_HEADS):
        qih = proj[:, 1024 + hd * IDX_DIM: 1024 + (hd + 1) * IDX_DIM]
        qs_ref[0, hd] = (qih * wabs[:, hd:hd + 1]).astype(BF16)
    sg_ref[0] = jnp.where(wi >= 0, 1.0, -1.0)
    k_ref[0] = _rms(proj[:, 1280:1344], kg_ref[...]).astype(BF16)
    ki_ref[0] = _rms(proj[:, 1344:1408], ikg_ref[...]).astype(BF16)
    vv = proj[:, 1408:1536]
    lane = lax.broadcasted_iota(jnp.int32, vv.shape, 1)
    v_ref[0] = jnp.where(lane < HEAD_DIM, jnp.where(lane == 0, 1.0, 0.0), vv).astype(BF16)


def _s5_kernel(u_ref, bplo_ref, bphi_ref, cplo_ref, cphi_ref, ar_ref, ai_ref, d_ref, wglu_ref, bglu_ref,
               wps_ref, o_ref, bu_scr, st_scr):
    @pl.when(pl.program_id(0) == 0)
    def _():
        st_scr[...] = jnp.zeros_like(st_scr)

    u = u_ref[...]
    ub = u.astype(BF16)
    bu_scr[:, 0:2048] = _dot(ub[:, 0:256], bplo_ref[...])
    bu_scr[:, 2048:4096] = _dot(ub[:, 256:512], bphi_ref[...])
    ar = ar_ref[...]
    ai = ai_ref[...]

    def step(t, s):
        r0 = pl.multiple_of(t * SUBLANES, SUBLANES)
        b = bu_scr[pl.ds(r0, SUBLANES), :]
        parts = []
        for half in range(2):
            o = half * 2048
            sr, si = s[:, o:o + 1024], s[:, o + 1024:o + 2048]
            br, bi = b[:, o:o + 1024], b[:, o + 1024:o + 2048]
            a_r, a_i = ar[:, half * 1024:(half + 1) * 1024], ai[:, half * 1024:(half + 1) * 1024]
            parts.append(a_r * sr - a_i * si + br)
            parts.append(a_r * si + a_i * sr + bi)
        s2 = jnp.concatenate(parts, axis=1)
        bu_scr[pl.ds(r0, SUBLANES), :] = s2
        return s2

    s_fin = lax.fori_loop(0, TC_SCAN, step, st_scr[...])
    st_scr[...] = s_fin
    sb = bu_scr[...].astype(BF16)
    y = jnp.concatenate([_dot(sb[:, 0:2048], cplo_ref[...]), _dot(sb[:, 2048:4096], cphi_ref[...])], axis=1)
    y = jax.nn.gelu(y + d_ref[...] * u)
    z = _dot(y.astype(BF16), wglu_ref[...]) + bglu_ref[...]
    y = y * jax.nn.sigmoid(z)
    o_ref[...] = _dot(y.astype(BF16), wps_ref[...]).astype(BF16)


def _key_to_f32(key):
    bits = jnp.where(key < 0, key & jnp.int32(0x7FFFFFFF), ~key)
    return pltpu.bitcast(bits, F32)


def _attn_kernel(qs_ref, sg_ref, q_ref, ki_ref, k_ref, v_ref, o_ref, s_scr, m_scr, acc_scr):
    i = pl.program_id(1)
    n_tiles = (i * QB + QB + TK - 1) // TK
    q_pos = i * QB + lax.broadcasted_iota(jnp.int32, (QB, TK), 0)
    col = lax.broadcasted_iota(jnp.int32, (QB, TK), 1)
    ksel = float(INDEX_TOPK)

    sg = sg_ref[0]
    sgn = [jnp.broadcast_to(sg[:, hd:hd + 1], (QB, TK)) for hd in range(IDX_HEADS)]

    def idx_body(j, carry):
        kt = ki_ref[0, pl.ds(pl.multiple_of(j * TK, TK), TK), :]
        acc = jnp.zeros((QB, TK), F32)
        for hd in range(IDX_HEADS):
            acc = acc + sgn[hd] * jnp.maximum(_dot_nt(qs_ref[0, hd], kt), 0.0)
        s_scr[j] = jnp.where(j * TK + col <= q_pos, acc, -jnp.inf)
        return carry

    lax.fori_loop(0, n_tiles, idx_body, 0)

    ones_b = jnp.ones((LANES, LANES), BF16)

    def count(pred):
        def body(j, acc):
            s = s_scr[j]
            for c in range(TK // LANES):
                acc = acc + jnp.where(pred(s[:, c * LANES:(c + 1) * LANES], j, c), 1.0, 0.0)
            return acc
        acc = lax.fori_loop(0, n_tiles, body, jnp.zeros((QB, LANES), F32))
        return _dot(acc.astype(BF16), ones_b)

    def bit_body(it, key):
        cand = key | lax.shift_left(jnp.int32(1), 31 - it)
        tc = _key_to_f32(cand)
        cnt = count(lambda s, j, c: s >= tc)
        below_ninf = lax.shift_right_logical(cand, 23) == 0
        return jnp.where(below_ninf | (cnt >= ksel), cand, key)

    key = lax.fori_loop(0, 32, bit_body, jnp.zeros((QB, LANES), jnp.int32))
    thr = _key_to_f32(key)
    cnt_ge = count(lambda s, j, c: s >= thr)
    has_excess = jnp.max(cnt_ge) > ksel

    @pl.when(jnp.logical_not(has_excess))
    def _():
        def body(j, carry):
            s = s_scr[j]
            vis = j * TK + col <= q_pos
            parts = []
            for c in range(TK // LANES):
                sl = slice(c * LANES, (c + 1) * LANES)
                parts.append(jnp.where(vis[:, sl], jnp.where(s[:, sl] >= thr, 0.0, -jnp.inf), -jnp.inf))
            s_scr[j] = jnp.concatenate(parts, axis=1)
            return carry
        lax.fori_loop(0, n_tiles, body, 0)

    @pl.when(has_excess)
    def _():
        cnt_gt = count(lambda s, j, c: s > thr)
        need = ksel - cnt_gt
        lane = lax.broadcasted_iota(jnp.int32, (QB, LANES), 1)

        def jbit_body(it, jkey):
            cand = jkey | lax.shift_left(jnp.int32(1), 10 - it)
            cnt = count(lambda s, j, c: (s == thr) & (j * TK + c * LANES + lane < cand))
            return jnp.where(cnt < need, cand, jkey)

        jkey = lax.fori_loop(0, 11, jbit_body, jnp.zeros((QB, LANES), jnp.int32))

        def body(j, carry):
            s = s_scr[j]
            vis = j * TK + col <= q_pos
            parts = []
            for c in range(TK // LANES):
                sl = slice(c * LANES, (c + 1) * LANES)
                kpos = j * TK + c * LANES + lane
                sel = (s[:, sl] > thr) | ((s[:, sl] == thr) & (kpos <= jkey))
                parts.append(jnp.where(vis[:, sl], jnp.where(sel, 0.0, -jnp.inf), -jnp.inf))
            s_scr[j] = jnp.concatenate(parts, axis=1)
            return carry
        lax.fori_loop(0, n_tiles, body, 0)

    m_scr[...] = jnp.full(m_scr.shape, NEG_BIG, F32)
    acc_scr[...] = jnp.zeros(acc_scr.shape, F32)

    def att_body(j, carry):
        r0 = pl.multiple_of(j * TK, TK)
        kt = k_ref[0, pl.ds(r0, TK), :]
        vt = v_ref[0, pl.ds(r0, TK), :]
        bias = s_scr[j]
        for hd in range(N_HEADS):
            s = _dot_nt(q_ref[0, hd], kt) + bias
            m_old = m_scr[hd]
            mx = s[:, 0:LANES]
            for c in range(1, TK // LANES):
                mx = jnp.maximum(mx, s[:, c * LANES:(c + 1) * LANES])
            m_new = jnp.maximum(m_old, jnp.max(mx, axis=-1, keepdims=True))
            alpha = jnp.exp2(m_old - m_new)
            p = jnp.concatenate(
                [jnp.exp2(s[:, c * LANES:(c + 1) * LANES] - m_new) for c in range(TK // LANES)], axis=1)
            acc_scr[hd] = alpha * acc_scr[hd] + _dot(p.astype(BF16), vt)
            m_scr[hd] = m_new
        return carry

    lax.fori_loop(0, n_tiles, att_body, 0)

    outs = []
    for hd in range(N_HEADS):
        a = acc_scr[hd]
        outs.append(a[:, HEAD_DIM:2 * HEAD_DIM] / a[:, 0:1])
    o_ref[0] = jnp.concatenate(outs, axis=1).astype(BF16)


def _ffn_kernel(x_ref, ps_ref, ya_ref, g1_ref, wgate_ref, wpa_ref, wo_ref, g2_ref, wfg_ref, wfu_ref, wfd_ref,
                o_ref):
    x = x_ref[0]
    h = _rms(x, g1_ref[...]).astype(BF16)
    gates = _dot(h, wgate_ref[...])
    pa = _dot(ya_ref[0], wpa_ref[...])
    merged = (jax.nn.sigmoid(gates[:, 0:D_MODEL]) * ps_ref[...].astype(F32)
              + jax.nn.sigmoid(gates[:, D_MODEL:2 * D_MODEL]) * pa)
    x1 = x + _dot(merged.astype(BF16), wo_ref[...])
    h2 = _rms(x1, g2_ref[...]).astype(BF16)
    hid = jax.nn.silu(_dot(h2, wfg_ref[...])) * _dot(h2, wfu_ref[...])
    o_ref[0] = x1 + _dot(hid.astype(BF16), wfd_ref[...])


def _const_spec(shape):
    nd = len(shape)
    return pl.BlockSpec(shape, lambda *_: (0,) * nd, pipeline_mode=pl.Buffered(1))


def _block_diag(blocks):
    g, r, c = blocks.shape
    eye = jnp.eye(g, dtype=blocks.dtype)
    return jnp.einsum('grc,gh->grhc', blocks, eye).reshape(g * r, g * c)


def kernel(x, norm1_g, w_in, A_re, A_im, log_dt, B_re, B_im, C_re, C_im, D_skip, w_glu, b_glu, q_norm_g, k_norm_g,
           idx_k_norm_g, w_proj_ssm, w_proj_attn, w_out, norm2_g, w_ffn_gate, w_ffn_up, w_ffn_down):
    bsz, seq, _ = x.shape
    assert x.shape == (8, 2048, D_MODEL) and w_in.shape[0] == 1
    G, N, P = SSM_GROUPS, SSM_STATE, SSM_GROUP
    cparams = functools.partial(pltpu.CompilerParams, vmem_limit_bytes=VMEM_LIMIT)

    w = w_in[0]
    w_u, w_q, w_k, w_v, w_qi, w_ki, w_wi, w_gates = jnp.split(
        w, (512, 1024, 1088, 1152, 1408, 1472, 1476), axis=1)
    w1 = jnp.concatenate([w_u, w_q, w_qi, w_k, w_ki, w_wi, jnp.zeros((D_MODEL, HEAD_DIM - IDX_HEADS), F32), w_v],
                         axis=1).astype(BF16)
    g1 = norm1_g[0].reshape(1, D_MODEL)

    rep = lambda a: jnp.repeat(a, P, axis=0)
    gpn = jax.ShapeDtypeStruct((G * P, N), F32)
    abar_r, abar_i, bp_r, bp_i = pl.pallas_call(
        _disc_kernel, out_shape=(gpn, gpn, gpn, gpn), name="s5_discretise",
    )(rep(A_re[0]), rep(A_im[0]), rep(jnp.broadcast_to(log_dt[0].reshape(G, 1), (G, N))),
      B_re[0].transpose(0, 2, 1).reshape(G * P, N), B_im[0].transpose(0, 2, 1).reshape(G * P, N))
    abar_r, abar_i = abar_r[::P], abar_i[::P]
    bp_r, bp_i = bp_r.reshape(G, P, N), bp_i.reshape(G, P, N)
    hg = G // 2
    bp = [jnp.concatenate([_block_diag(bp_r[sl]), _block_diag(bp_i[sl])], axis=1).astype(BF16)
          for sl in (slice(0, hg), slice(hg, G))]
    c_r = C_re[0].transpose(0, 2, 1)
    c_i = C_im[0].transpose(0, 2, 1)
    cp = [jnp.concatenate([_block_diag(c_r[sl]), -_block_diag(c_i[sl])], axis=0).astype(BF16)
          for sl in (slice(0, hg), slice(hg, G))]
    ar8 = jnp.broadcast_to(abar_r.reshape(1, G * N), (SUBLANES, G * N))
    ai8 = jnp.broadcast_to(abar_i.reshape(1, G * N), (SUBLANES, G * N))

    n_in = seq // TL_IN
    u_t, q_h, qs_h, sg, k_n, ki_n, v_e = pl.pallas_call(
        _inproj_kernel,
        grid=(bsz, n_in),
        in_specs=[
            pl.BlockSpec((1, TL_IN, D_MODEL), lambda b, i: (b, i, 0)),
            _const_spec((1, D_MODEL)),
            _const_spec((D_MODEL, W1_COLS)),
            _const_spec((1, HEAD_DIM)), _const_spec((1, HEAD_DIM)), _const_spec((1, IDX_DIM)),
        ],
        out_specs=[
            pl.BlockSpec((TL_IN, SSM_WIDTH), lambda b, i: (i, b)),
            pl.BlockSpec((1, N_HEADS, TL_IN, HEAD_DIM), lambda b, i: (b, 0, i, 0)),
            pl.BlockSpec((1, IDX_HEADS, TL_IN, IDX_DIM), lambda b, i: (b, 0, i, 0)),
            pl.BlockSpec((1, TL_IN, IDX_HEADS), lambda b, i: (b, i, 0)),
            pl.BlockSpec((1, TL_IN, HEAD_DIM), lambda b, i: (b, i, 0)),
            pl.BlockSpec((1, TL_IN, IDX_DIM), lambda b, i: (b, i, 0)),
            pl.BlockSpec((1, TL_IN, 2 * HEAD_DIM), lambda b, i: (b, i, 0)),
        ],
        out_shape=(
            jax.ShapeDtypeStruct((seq, bsz * SSM_WIDTH), F32),
            jax.ShapeDtypeStruct((bsz, N_HEADS, seq, HEAD_DIM), BF16),
            jax.ShapeDtypeStruct((bsz, IDX_HEADS, seq, IDX_DIM), BF16),
            jax.ShapeDtypeStruct((bsz, seq, IDX_HEADS), F32),
            jax.ShapeDtypeStruct((bsz, seq, HEAD_DIM), BF16),
            jax.ShapeDtypeStruct((bsz, seq, IDX_DIM), BF16),
            jax.ShapeDtypeStruct((bsz, seq, 2 * HEAD_DIM), BF16),
        ),
        compiler_params=cparams(dimension_semantics=("parallel", "parallel")),
        name="in_projection",
    )(x, g1, w1, q_norm_g[0].reshape(1, HEAD_DIM), k_norm_g[0].reshape(1, HEAD_DIM),
      idx_k_norm_g[0].reshape(1, IDX_DIM))

    rows = SUBLANES * TC_SCAN
    ps_t = pl.pallas_call(
        _s5_kernel,
        grid=(seq // TC_SCAN,),
        in_specs=[
            pl.BlockSpec((rows, SSM_WIDTH), lambda c: (c, 0)),
            _const_spec((256, 2048)), _const_spec((256, 2048)),
            _const_spec((2048, 256)), _const_spec((2048, 256)),
            _const_spec((SUBLANES, G * N)), _const_spec((SUBLANES, G * N)),
            _const_spec((1, SSM_WIDTH)), _const_spec((SSM_WIDTH, SSM_WIDTH)), _const_spec((1, SSM_WIDTH)),
            _const_spec((SSM_WIDTH, D_MODEL)),
        ],
        out_specs=pl.BlockSpec((rows, D_MODEL), lambda c: (c, 0)),
        out_shape=jax.ShapeDtypeStruct((seq * bsz, D_MODEL), BF16),
        scratch_shapes=[pltpu.VMEM((rows, 2 * G * N), F32), pltpu.VMEM((SUBLANES, 2 * G * N), F32)],
        compiler_params=cparams(dimension_semantics=("arbitrary",)),
        name="s5_branch",
    )(u_t.reshape(seq * bsz, SSM_WIDTH), bp[0], bp[1], cp[0], cp[1], ar8, ai8,
      D_skip[0].reshape(1, SSM_WIDTH), w_glu[0].astype(BF16), b_glu[0].reshape(1, SSM_WIDTH),
      w_proj_ssm[0].astype(BF16))

    y_att = pl.pallas_call(
        _attn_kernel,
        grid=(bsz, seq // QB),
        in_specs=[
            pl.BlockSpec((1, IDX_HEADS, QB, IDX_DIM), lambda b, i: (b, 0, i, 0)),
            pl.BlockSpec((1, QB, IDX_HEADS), lambda b, i: (b, i, 0)),
            pl.BlockSpec((1, N_HEADS, QB, HEAD_DIM), lambda b, i: (b, 0, i, 0)),
            pl.BlockSpec((1, seq, IDX_DIM), lambda b, i: (b, 0, 0)),
            pl.BlockSpec((1, seq, HEAD_DIM), lambda b, i: (b, 0, 0)),
            pl.BlockSpec((1, seq, 2 * HEAD_DIM), lambda b, i: (b, 0, 0)),
        ],
        out_specs=pl.BlockSpec((1, QB, ATTN_WIDTH), lambda b, i: (b, i, 0)),
        out_shape=jax.ShapeDtypeStruct((bsz, seq, ATTN_WIDTH), BF16),
        scratch_shapes=[pltpu.VMEM((seq // TK, QB, TK), F32),
                        pltpu.VMEM((N_HEADS, QB, LANES), F32),
                        pltpu.VMEM((N_HEADS, QB, LANES), F32)],
        compiler_params=cparams(dimension_semantics=("parallel", "arbitrary")),
        name="sparse_attention",
    )(qs_h, sg, q_h, ki_n, k_n, v_e)

    out = pl.pallas_call(
        _ffn_kernel,
        grid=(bsz, seq // TL_FFN),
        in_specs=[
            pl.BlockSpec((1, TL_FFN, D_MODEL), lambda b, i: (b, i, 0)),
            pl.BlockSpec((TL_FFN, D_MODEL), lambda b, i: (i, b)),
            pl.BlockSpec((1, TL_FFN, ATTN_WIDTH), lambda b, i: (b, i, 0)),
            _const_spec((1, D_MODEL)),
            _const_spec((D_MODEL, 2 * D_MODEL)),
            _const_spec((ATTN_WIDTH, D_MODEL)),
            _const_spec((D_MODEL, D_MODEL)),
            _const_spec((1, D_MODEL)),
            _const_spec((D_MODEL, D_FF)), _const_spec((D_MODEL, D_FF)), _const_spec((D_FF, D_MODEL)),
        ],
        out_specs=pl.BlockSpec((1, TL_FFN, D_MODEL), lambda b, i: (b, i, 0)),
        out_shape=jax.ShapeDtypeStruct((bsz, seq, D_MODEL), F32),
        compiler_params=cparams(dimension_semantics=("parallel", "parallel")),
        name="merge_ffn",
    )(x, ps_t.reshape(seq, bsz * D_MODEL), y_att, g1, w_gates.astype(BF16), w_proj_attn[0].astype(BF16),
      w_out[0].astype(BF16), norm2_g[0].reshape(1, D_MODEL), w_ffn_gate[0].astype(BF16),
      w_ffn_up[0].astype(BF16), w_ffn_down[0].astype(BF16))
    return out
```

```python
import functools
import math

import jax
import jax.numpy as jnp
from jax import lax
from jax.experimental import pallas as pl
from jax.experimental.pallas import tpu as pltpu

F32 = jnp.float32
BF16 = jnp.bfloat16

D_MODEL = 1024
SSM_WIDTH = 512
SSM_GROUP = 16
SSM_GROUPS = 32
SSM_STATE = 64
N_HEADS = 8
HEAD_DIM = 64
ATTN_WIDTH = 512
IDX_HEADS = 4
IDX_DIM = 64
INDEX_TOPK = 256
D_FF = 2816
RMS_EPS = 1e-6

LANES = 128
SUBLANES = 8
VMEM_LIMIT = 56 * 1024 * 1024

TL_IN = 512
TC_SCAN = 64
QB = 128
TK = 256
TL_FFN = 256

W1_COLS = 1536
NEG_BIG = -1e30
assert QB == LANES


def _dot(a, b):
    return jnp.dot(a, b, preferred_element_type=F32)


def _dot_nt(a, b):
    return lax.dot_general(a, b, (((1,), (1,)), ((), ())), preferred_element_type=F32)


def _rms(x, g):
    return x * lax.rsqrt(jnp.mean(x * x, axis=-1, keepdims=True) + RMS_EPS) * g


def _disc_kernel(are_ref, aim_ref, ldt_ref, bre_ref, bim_ref, abr_ref, abi_ref, bpr_ref, bpi_ref):
    ar = are_ref[...]
    ai = aim_ref[...]
    dt = jnp.exp(ldt_ref[...])
    mag = jnp.exp(ar * dt)
    abar_r = mag * jnp.cos(ai * dt)
    abar_i = mag * jnp.sin(ai * dt)
    den = ar * ar + ai * ai
    nr = abar_r - 1.0
    coef_r = (nr * ar + abar_i * ai) / den
    coef_i = (abar_i * ar - nr * ai) / den
    abr_ref[...] = abar_r
    abi_ref[...] = abar_i
    br = bre_ref[...]
    bi = bim_ref[...]
    bpr_ref[...] = coef_r * br - coef_i * bi
    bpi_ref[...] = coef_r * bi + coef_i * br


def _inproj_kernel(x_ref, g1_ref, w1_ref, qg_ref, kg_ref, ikg_ref,
                   u_ref, q_ref, qs_ref, sg_ref, k_ref, ki_ref, v_ref):
    x = x_ref[0]
    h = _rms(x, g1_ref[...]).astype(BF16)
    proj = _dot(h, w1_ref[...])
    u_ref[...] = proj[:, 0:512]
    qscale = (HEAD_DIM ** -0.5) * math.log2(math.e)
    qg = qg_ref[...]
    for hd in range(N_HEADS):
        qh = proj[:, 512 + hd * HEAD_DIM: 512 + (hd + 1) * HEAD_DIM]
        q_ref[0, hd] = (_rms(qh, qg) * qscale).astype(BF16)
    wi = proj[:, 1408:1412]
    w_scale = (IDX_HEADS ** -0.5) * (IDX_DIM ** -0.5)
    wabs = jnp.abs(wi) * w_scale
    for hd in range(IDX_HEADS):
        qih = proj[:, 1024 + hd * IDX_DIM: 1024 + (hd + 1) * IDX_DIM]
        qs_ref[0, hd] = (qih * wabs[:, hd:hd + 1]).astype(BF16)
    k_ref[0] = _rms(proj[:, 1280:1344], kg_ref[...]).astype(BF16)
    ki_ref[0] = _rms(proj[:, 1344:1408], ikg_ref[...]).astype(BF16)
    t = proj[:, 1408:1536].T
    sg_ref[0] = jnp.where(t[0:SUBLANES] >= 0, 1.0, -1.0)
    for c in range(TL_IN // TK):
        v_ref[0, c] = t[HEAD_DIM:2 * HEAD_DIM, c * TK:(c + 1) * TK].astype(BF16)


def _s5_kernel(u_ref, bplo_ref, bphi_ref, cplo_ref, cphi_ref, ar_ref, ai_ref, d_ref, wglu_ref, bglu_ref,
               wps_ref, o_ref, bu_scr, st_scr):
    @pl.when(pl.program_id(0) == 0)
    def _():
        st_scr[...] = jnp.zeros_like(st_scr)

    u = u_ref[...]
    ub = u.astype(BF16)
    bu_scr[:, 0:2048] = _dot(ub[:, 0:256], bplo_ref[...])
    bu_scr[:, 2048:4096] = _dot(ub[:, 256:512], bphi_ref[...])
    ar = ar_ref[...]
    ai = ai_ref[...]

    def step(t, s):
        r0 = pl.multiple_of(t * SUBLANES, SUBLANES)
        b = bu_scr[pl.ds(r0, SUBLANES), :]
        parts = []
        for half in range(2):
            o = half * 2048
            sr, si = s[:, o:o + 1024], s[:, o + 1024:o + 2048]
            br, bi = b[:, o:o + 1024], b[:, o + 1024:o + 2048]
            a_r, a_i = ar[:, half * 1024:(half + 1) * 1024], ai[:, half * 1024:(half + 1) * 1024]
            parts.append(a_r * sr - a_i * si + br)
            parts.append(a_r * si + a_i * sr + bi)
        s2 = jnp.concatenate(parts, axis=1)
        bu_scr[pl.ds(r0, SUBLANES), :] = s2
        return s2

    s_fin = lax.fori_loop(0, TC_SCAN, step, st_scr[...])
    st_scr[...] = s_fin
    sb = bu_scr[...].astype(BF16)
    y = jnp.concatenate([_dot(sb[:, 0:2048], cplo_ref[...]), _dot(sb[:, 2048:4096], cphi_ref[...])], axis=1)
    y = jax.nn.gelu(y + d_ref[...] * u)
    z = _dot(y.astype(BF16), wglu_ref[...]) + bglu_ref[...]
    y = y * jax.nn.sigmoid(z)
    o_ref[...] = _dot(y.astype(BF16), wps_ref[...]).astype(BF16)


def _key_to_f32(key):
    bits = jnp.where(key < 0, key & jnp.int32(0x7FFFFFFF), ~key)
    return pltpu.bitcast(bits, F32)


def _attn_kernel(qs_ref, sg_ref, q_ref, ki_ref, k_ref, vt_ref, o_ref, s_scr, m_scr, l_scr, acc_scr):
    i = pl.program_id(1)
    n_tiles = (i * QB + QB + TK - 1) // TK
    nv = TK // SUBLANES
    shape3 = (nv, SUBLANES, QB)
    k_in_tile = (lax.broadcasted_iota(jnp.int32, shape3, 0) * SUBLANES
                 + lax.broadcasted_iota(jnp.int32, shape3, 1))
    q_pos = i * QB + lax.broadcasted_iota(jnp.int32, shape3, 2)
    ksel = float(INDEX_TOPK)

    def all_sublanes(a, op):
        for sh in (4, 2, 1):
            a = op(a, pltpu.roll(a, sh, 0))
        return a

    sg = sg_ref[0]

    def idx_body(j, carry):
        kt = ki_ref[0, pl.ds(pl.multiple_of(j * TK, TK), TK), :]
        acc = jnp.zeros((TK, QB), F32)
        for pr in range(IDX_HEADS // 2):
            x = _dot_nt(kt, qs_ref[0, 2 * pr:2 * pr + 2].reshape(2 * QB, IDX_DIM))
            for e in range(2):
                hd = 2 * pr + e
                acc = acc + sg[hd:hd + 1, :] * jnp.maximum(x[:, e * QB:(e + 1) * QB], 0.0)
        vis = j * TK + k_in_tile <= q_pos
        s_scr[j] = jnp.where(vis, acc.reshape(shape3), -jnp.inf).reshape(TK, QB)
        return carry

    lax.fori_loop(0, n_tiles, idx_body, 0)

    def count(pred):
        def body(j, acc):
            hit = jnp.where(pred(s_scr[j].reshape(shape3), j), 1.0, 0.0)
            return acc + jnp.sum(hit.reshape(nv // 4, 4, SUBLANES, QB), axis=0)
        acc = lax.fori_loop(0, n_tiles, body, jnp.zeros((4, SUBLANES, QB), F32))
        return all_sublanes(jnp.sum(acc, axis=0), jnp.add)

    def bit_body(it, key):
        cand = key | lax.shift_left(jnp.int32(1), 31 - it)
        tc = _key_to_f32(cand)
        cnt = count(lambda s, j: s >= tc)
        below_ninf = lax.shift_right_logical(cand, 23) == 0
        return jnp.where(below_ninf | (cnt >= ksel), cand, key)

    key = lax.fori_loop(0, 32, bit_body, jnp.zeros((SUBLANES, QB), jnp.int32))
    thr = _key_to_f32(key)
    cnt_ge = count(lambda s, j: s >= thr)
    has_excess = jnp.max(cnt_ge) > ksel

    def write_bias(select):
        def body(j, carry):
            s = s_scr[j].reshape(shape3)
            vis = j * TK + k_in_tile <= q_pos
            bias = jnp.where(vis, jnp.where(select(s, j), 0.0, -jnp.inf), -jnp.inf)
            s_scr[j] = bias.reshape(TK, QB)
            return carry
        lax.fori_loop(0, n_tiles, body, 0)

    @pl.when(jnp.logical_not(has_excess))
    def _():
        write_bias(lambda s, j: s >= thr)

    @pl.when(has_excess)
    def _():
        cnt_gt = count(lambda s, j: s > thr)
        need = ksel - cnt_gt

        def jbit_body(it, jkey):
            cand = jkey | lax.shift_left(jnp.int32(1), 10 - it)
            cnt = count(lambda s, j: (s == thr) & (j * TK + k_in_tile < cand))
            return jnp.where(cnt < need, cand, jkey)

        jkey = lax.fori_loop(0, 11, jbit_body, jnp.zeros((SUBLANES, QB), jnp.int32))
        write_bias(lambda s, j: (s > thr) | ((s == thr) & (j * TK + k_in_tile <= jkey)))

    m_scr[...] = jnp.full(m_scr.shape, NEG_BIG, F32)
    l_scr[...] = jnp.zeros(l_scr.shape, F32)
    acc_scr[...] = jnp.zeros(acc_scr.shape, F32)

    def att_body(j, carry):
        kt = k_ref[0, pl.ds(pl.multiple_of(j * TK, TK), TK), :]
        vt = vt_ref[0, j]
        bias = s_scr[j].reshape(shape3)
        for pr in range(N_HEADS // 2):
            sp = _dot_nt(kt, q_ref[0, 2 * pr:2 * pr + 2].reshape(2 * QB, HEAD_DIM))
            for e in range(2):
                hd = 2 * pr + e
                s = sp[:, e * QB:(e + 1) * QB].reshape(shape3) + bias
                m_old = m_scr[hd]
                m_new = jnp.maximum(m_old, all_sublanes(jnp.max(s, axis=0), jnp.maximum))
                alpha = jnp.exp2(m_old - m_new)
                p = jnp.exp2(s - m_new)
                l_scr[hd] = alpha * l_scr[hd] + jnp.sum(p, axis=0)
                pv = _dot(vt, p.reshape(TK, QB).astype(BF16))
                acc = acc_scr[hd].reshape(HEAD_DIM // SUBLANES, SUBLANES, QB) * alpha
                acc_scr[hd] = acc.reshape(HEAD_DIM, QB) + pv
                m_scr[hd] = m_new
        return carry

    lax.fori_loop(0, n_tiles, att_body, 0)

    outs = []
    for hd in range(N_HEADS):
        l = all_sublanes(l_scr[hd], jnp.add)
        o = acc_scr[hd].reshape(HEAD_DIM // SUBLANES, SUBLANES, QB) / l
        outs.append(o.reshape(HEAD_DIM, QB))
    o_ref[0] = jnp.concatenate(outs, axis=0).T.astype(BF16)


def _ffn_kernel(x_ref, ps_ref, ya_ref, g1_ref, wgate_ref, wpa_ref, wo_ref, g2_ref, wfg_ref, wfu_ref, wfd_ref,
                o_ref):
    x = x_ref[0]
    h = _rms(x, g1_ref[...]).astype(BF16)
    gates = _dot(h, wgate_ref[...])
    pa = _dot(ya_ref[0], wpa_ref[...])
    merged = (jax.nn.sigmoid(gates[:, 0:D_MODEL]) * ps_ref[...].astype(F32)
              + jax.nn.sigmoid(gates[:, D_MODEL:2 * D_MODEL]) * pa)
    x1 = x + _dot(merged.astype(BF16), wo_ref[...])
    h2 = _rms(x1, g2_ref[...]).astype(BF16)
    hid = jax.nn.silu(_dot(h2, wfg_ref[...])) * _dot(h2, wfu_ref[...])
    o_ref[0] = x1 + _dot(hid.astype(BF16), wfd_ref[...])


def _const_spec(shape):
    nd = len(shape)
    return pl.BlockSpec(shape, lambda *_: (0,) * nd, pipeline_mode=pl.Buffered(1))


def _block_diag(blocks):
    g, r, c = blocks.shape
    eye = jnp.eye(g, dtype=blocks.dtype)
    return jnp.einsum('grc,gh->grhc', blocks, eye).reshape(g * r, g * c)


def kernel(x, norm1_g, w_in, A_re, A_im, log_dt, B_re, B_im, C_re, C_im, D_skip, w_glu, b_glu, q_norm_g, k_norm_g,
           idx_k_norm_g, w_proj_ssm, w_proj_attn, w_out, norm2_g, w_ffn_gate, w_ffn_up, w_ffn_down):
    bsz, seq, _ = x.shape
    assert x.shape == (8, 2048, D_MODEL) and w_in.shape[0] == 1
    G, N, P = SSM_GROUPS, SSM_STATE, SSM_GROUP
    cparams = functools.partial(pltpu.CompilerParams, vmem_limit_bytes=VMEM_LIMIT)

    w = w_in[0]
    w_u, w_q, w_k, w_v, w_qi, w_ki, w_wi, w_gates = jnp.split(
        w, (512, 1024, 1088, 1152, 1408, 1472, 1476), axis=1)
    w1 = jnp.concatenate([w_u, w_q, w_qi, w_k, w_ki, w_wi, jnp.zeros((D_MODEL, HEAD_DIM - IDX_HEADS), F32), w_v],
                         axis=1).astype(BF16)
    g1 = norm1_g[0].reshape(1, D_MODEL)

    rep = lambda a: jnp.repeat(a, P, axis=0)
    gpn = jax.ShapeDtypeStruct((G * P, N), F32)
    abar_r, abar_i, bp_r, bp_i = pl.pallas_call(
        _disc_kernel, out_shape=(gpn, gpn, gpn, gpn), name="s5_discretise",
    )(rep(A_re[0]), rep(A_im[0]), rep(jnp.broadcast_to(log_dt[0].reshape(G, 1), (G, N))),
      B_re[0].transpose(0, 2, 1).reshape(G * P, N), B_im[0].transpose(0, 2, 1).reshape(G * P, N))
    abar_r, abar_i = abar_r[::P], abar_i[::P]
    bp_r, bp_i = bp_r.reshape(G, P, N), bp_i.reshape(G, P, N)
    hg = G // 2
    bp = [jnp.concatenate([_block_diag(bp_r[sl]), _block_diag(bp_i[sl])], axis=1).astype(BF16)
          for sl in (slice(0, hg), slice(hg, G))]
    c_r = C_re[0].transpose(0, 2, 1)
    c_i = C_im[0].transpose(0, 2, 1)
    cp = [jnp.concatenate([_block_diag(c_r[sl]), -_block_diag(c_i[sl])], axis=0).astype(BF16)
          for sl in (slice(0, hg), slice(hg, G))]
    ar8 = jnp.broadcast_to(abar_r.reshape(1, G * N), (SUBLANES, G * N))
    ai8 = jnp.broadcast_to(abar_i.reshape(1, G * N), (SUBLANES, G * N))

    n_in = seq // TL_IN
    u_t, q_h, qs_h, sg_t, k_n, ki_n, v_t = pl.pallas_call(
        _inproj_kernel,
        grid=(bsz, n_in),
        in_specs=[
            pl.BlockSpec((1, TL_IN, D_MODEL), lambda b, i: (b, i, 0)),
            _const_spec((1, D_MODEL)),
            _const_spec((D_MODEL, W1_COLS)),
            _const_spec((1, HEAD_DIM)), _const_spec((1, HEAD_DIM)), _const_spec((1, IDX_DIM)),
        ],
        out_specs=[
            pl.BlockSpec((TL_IN, SSM_WIDTH), lambda b, i: (i, b)),
            pl.BlockSpec((1, N_HEADS, TL_IN, HEAD_DIM), lambda b, i: (b, 0, i, 0)),
            pl.BlockSpec((1, IDX_HEADS, TL_IN, IDX_DIM), lambda b, i: (b, 0, i, 0)),
            pl.BlockSpec((1, SUBLANES, TL_IN), lambda b, i: (b, 0, i)),
            pl.BlockSpec((1, TL_IN, HEAD_DIM), lambda b, i: (b, i, 0)),
            pl.BlockSpec((1, TL_IN, IDX_DIM), lambda b, i: (b, i, 0)),
            pl.BlockSpec((1, TL_IN // TK, HEAD_DIM, TK), lambda b, i: (b, i, 0, 0)),
        ],
        out_shape=(
            jax.ShapeDtypeStruct((seq, bsz * SSM_WIDTH), F32),
            jax.ShapeDtypeStruct((bsz, N_HEADS, seq, HEAD_DIM), BF16),
            jax.ShapeDtypeStruct((bsz, IDX_HEADS, seq, IDX_DIM), BF16),
            jax.ShapeDtypeStruct((bsz, SUBLANES, seq), F32),
            jax.ShapeDtypeStruct((bsz, seq, HEAD_DIM), BF16),
            jax.ShapeDtypeStruct((bsz, seq, IDX_DIM), BF16),
            jax.ShapeDtypeStruct((bsz, seq // TK, HEAD_DIM, TK), BF16),
        ),
        compiler_params=cparams(dimension_semantics=("parallel", "parallel")),
        name="in_projection",
    )(x, g1, w1, q_norm_g[0].reshape(1, HEAD_DIM), k_norm_g[0].reshape(1, HEAD_DIM),
      idx_k_norm_g[0].reshape(1, IDX_DIM))

    rows = SUBLANES * TC_SCAN
    ps_t = pl.pallas_call(
        _s5_kernel,
        grid=(seq // TC_SCAN,),
        in_specs=[
            pl.BlockSpec((rows, SSM_WIDTH), lambda c: (c, 0)),
            _const_spec((256, 2048)), _const_spec((256, 2048)),
            _const_spec((2048, 256)), _const_spec((2048, 256)),
            _const_spec((SUBLANES, G * N)), _const_spec((SUBLANES, G * N)),
            _const_spec((1, SSM_WIDTH)), _const_spec((SSM_WIDTH, SSM_WIDTH)), _const_spec((1, SSM_WIDTH)),
            _const_spec((SSM_WIDTH, D_MODEL)),
        ],
        out_specs=pl.BlockSpec((rows, D_MODEL), lambda c: (c, 0)),
        out_shape=jax.ShapeDtypeStruct((seq * bsz, D_MODEL), BF16),
        scratch_shapes=[pltpu.VMEM((rows, 2 * G * N), F32), pltpu.VMEM((SUBLANES, 2 * G * N), F32)],
        compiler_params=cparams(dimension_semantics=("arbitrary",)),
        name="s5_branch",
    )(u_t.reshape(seq * bsz, SSM_WIDTH), bp[0], bp[1], cp[0], cp[1], ar8, ai8,
      D_skip[0].reshape(1, SSM_WIDTH), w_glu[0].astype(BF16), b_glu[0].reshape(1, SSM_WIDTH),
      w_proj_ssm[0].astype(BF16))

    y_att = pl.pallas_call(
        _attn_kernel,
        grid=(bsz, seq // QB),
        in_specs=[
            pl.BlockSpec((1, IDX_HEADS, QB, IDX_DIM), lambda b, i: (b, 0, i, 0)),
            pl.BlockSpec((1, SUBLANES, QB), lambda b, i: (b, 0, i)),
            pl.BlockSpec((1, N_HEADS, QB, HEAD_DIM), lambda b, i: (b, 0, i, 0)),
            pl.BlockSpec((1, seq, IDX_DIM), lambda b, i: (b, 0, 0)),
            pl.BlockSpec((1, seq, HEAD_DIM), lambda b, i: (b, 0, 0)),
            pl.BlockSpec((1, seq // TK, HEAD_DIM, TK), lambda b, i: (b, 0, 0, 0)),
        ],
        out_specs=pl.BlockSpec((1, QB, ATTN_WIDTH), lambda b, i: (b, i, 0)),
        out_shape=jax.ShapeDtypeStruct((bsz, seq, ATTN_WIDTH), BF16),
        scratch_shapes=[pltpu.VMEM((seq // TK, TK, QB), F32),
                        pltpu.VMEM((N_HEADS, SUBLANES, QB), F32),
                        pltpu.VMEM((N_HEADS, SUBLANES, QB), F32),
                        pltpu.VMEM((N_HEADS, HEAD_DIM, QB), F32)],
        compiler_params=cparams(dimension_semantics=("parallel", "arbitrary")),
        name="sparse_attention",
    )(qs_h, sg_t, q_h, ki_n, k_n, v_t)

    out = pl.pallas_call(
        _ffn_kernel,
        grid=(bsz, seq // TL_FFN),
        in_specs=[
            pl.BlockSpec((1, TL_FFN, D_MODEL), lambda b, i: (b, i, 0)),
            pl.BlockSpec((TL_FFN, D_MODEL), lambda b, i: (i, b)),
            pl.BlockSpec((1, TL_FFN, ATTN_WIDTH), lambda b, i: (b, i, 0)),
            _const_spec((1, D_MODEL)),
            _const_spec((D_MODEL, 2 * D_MODEL)),
            _const_spec((ATTN_WIDTH, D_MODEL)),
            _const_spec((D_MODEL, D_MODEL)),
            _const_spec((1, D_MODEL)),
            _const_spec((D_MODEL, D_FF)), _const_spec((D_MODEL, D_FF)), _const_spec((D_FF, D_MODEL)),
        ],
        out_specs=pl.BlockSpec((1, TL_FFN, D_MODEL), lambda b, i: (b, i, 0)),
        out_shape=jax.ShapeDtypeStruct((bsz, seq, D_MODEL), F32),
        compiler_params=cparams(dimension_semantics=("parallel", "parallel")),
        name="merge_ffn",
    )(x, ps_t.reshape(seq, bsz * D_MODEL), y_att, g1, w_gates.astype(BF16), w_proj_attn[0].astype(BF16),
      w_out[0].astype(BF16), norm2_g[0].reshape(1, D_MODEL), w_ffn_gate[0].astype(BF16),
      w_ffn_up[0].astype(BF16), w_ffn_down[0].astype(BF16))
    return out
```

```python
import functools
import math

import jax
import jax.numpy as jnp
from jax import lax
from jax.experimental import pallas as pl
from jax.experimental.pallas import tpu as pltpu

F32 = jnp.float32
BF16 = jnp.bfloat16

D_MODEL = 1024
SSM_WIDTH = 512
SSM_GROUP = 16
SSM_GROUPS = 32
SSM_STATE = 64
N_HEADS = 8
HEAD_DIM = 64
ATTN_WIDTH = 512
IDX_HEADS = 4
IDX_DIM = 64
INDEX_TOPK = 256
D_FF = 2816
RMS_EPS = 1e-6

LANES = 128
SUBLANES = 8
VMEM_LIMIT = 56 * 1024 * 1024

TL_IN = 512
TC_SCAN = 64
QB = 128
TK = 256
TL_FFN = 256

W1_COLS = 1536
NEG_BIG = -1e30
assert QB == LANES


def _dot(a, b):
    return jnp.dot(a, b, preferred_element_type=F32)


def _dot_nt(a, b):
    return lax.dot_general(a, b, (((1,), (1,)), ((), ())), preferred_element_type=F32)


def _rms(x, g):
    return x * lax.rsqrt(jnp.mean(x * x, axis=-1, keepdims=True) + RMS_EPS) * g


def _disc_kernel(are_ref, aim_ref, ldt_ref, bre_ref, bim_ref, abr_ref, abi_ref, bpr_ref, bpi_ref):
    ar = are_ref[...]
    ai = aim_ref[...]
    dt = jnp.exp(ldt_ref[...])
    mag = jnp.exp(ar * dt)
    abar_r = mag * jnp.cos(ai * dt)
    abar_i = mag * jnp.sin(ai * dt)
    den = ar * ar + ai * ai
    nr = abar_r - 1.0
    coef_r = (nr * ar + abar_i * ai) / den
    coef_i = (abar_i * ar - nr * ai) / den
    abr_ref[...] = abar_r
    abi_ref[...] = abar_i
    br = bre_ref[...]
    bi = bim_ref[...]
    bpr_ref[...] = coef_r * br - coef_i * bi
    bpi_ref[...] = coef_r * bi + coef_i * br


def _inproj_kernel(x_ref, g1_ref, w1_ref, qg_ref, kg_ref, ikg_ref,
                   u_ref, q_ref, qs_ref, sg_ref, k_ref, ki_ref, v_ref):
    x = x_ref[0]
    h = _rms(x, g1_ref[...]).astype(BF16)
    proj = _dot(h, w1_ref[...])
    u_ref[...] = proj[:, 0:512]
    qscale = (HEAD_DIM ** -0.5) * math.log2(math.e)
    qg = qg_ref[...]
    for hd in range(N_HEADS):
        qh = proj[:, 512 + hd * HEAD_DIM: 512 + (hd + 1) * HEAD_DIM]
        q_ref[0, hd] = (_rms(qh, qg) * qscale).astype(BF16)
    wi = proj[:, 1408:1412]
    w_scale = (IDX_HEADS ** -0.5) * (IDX_DIM ** -0.5)
    wabs = jnp.abs(wi) * w_scale
    for hd in range(IDX_HEADS):
        qih = proj[:, 1024 + hd * IDX_DIM: 1024 + (hd + 1) * IDX_DIM]
        qs_ref[0, hd] = (qih * wabs[:, hd:hd + 1]).astype(BF16)
    k_ref[0] = _rms(proj[:, 1280:1344], kg_ref[...]).astype(BF16)
    ki_ref[0] = _rms(proj[:, 1344:1408], ikg_ref[...]).astype(BF16)
    t = proj[:, 1408:1536].T
    sg_ref[0] = jnp.where(t[0:SUBLANES] >= 0, 1.0, -1.0)
    for c in range(TL_IN // TK):
        v_ref[0, c] = t[HEAD_DIM:2 * HEAD_DIM, c * TK:(c + 1) * TK].astype(BF16)


def _s5_kernel(u_ref, bplo_ref, bphi_ref, cplo_ref, cphi_ref, ar_ref, ai_ref, d_ref, wglu_ref, bglu_ref,
               wps_ref, o_ref, bu_scr, st_scr):
    @pl.when(pl.program_id(0) == 0)
    def _():
        st_scr[...] = jnp.zeros_like(st_scr)

    u = u_ref[...]
    ub = u.astype(BF16)
    bu_scr[:, 0:2048] = _dot(ub[:, 0:256], bplo_ref[...])
    bu_scr[:, 2048:4096] = _dot(ub[:, 256:512], bphi_ref[...])
    ar = ar_ref[...]
    ai = ai_ref[...]

    def step(t, s):
        r0 = pl.multiple_of(t * SUBLANES, SUBLANES)
        b = bu_scr[pl.ds(r0, SUBLANES), :]
        parts = []
        for half in range(2):
            o = half * 2048
            sr, si = s[:, o:o + 1024], s[:, o + 1024:o + 2048]
            br, bi = b[:, o:o + 1024], b[:, o + 1024:o + 2048]
            a_r, a_i = ar[:, half * 1024:(half + 1) * 1024], ai[:, half * 1024:(half + 1) * 1024]
            parts.append(a_r * sr - a_i * si + br)
            parts.append(a_r * si + a_i * sr + bi)
        s2 = jnp.concatenate(parts, axis=1)
        bu_scr[pl.ds(r0, SUBLANES), :] = s2
        return s2

    s_fin = lax.fori_loop(0, TC_SCAN, step, st_scr[...])
    st_scr[...] = s_fin
    sb = bu_scr[...].astype(BF16)
    y = jnp.concatenate([_dot(sb[:, 0:2048], cplo_ref[...]), _dot(sb[:, 2048:4096], cphi_ref[...])], axis=1)
    y = jax.nn.gelu(y + d_ref[...] * u)
    z = _dot(y.astype(BF16), wglu_ref[...]) + bglu_ref[...]
    y = y * jax.nn.sigmoid(z)
    o_ref[...] = _dot(y.astype(BF16), wps_ref[...]).astype(BF16)


def _key_to_f32(key):
    bits = jnp.where(key < 0, key & jnp.int32(0x7FFFFFFF), ~key)
    return pltpu.bitcast(bits, F32)


def _attn_kernel(qs_ref, sg_ref, q_ref, ki_ref, k_ref, vt_ref, o_ref, s_scr, m_scr, l_scr, acc_scr, sp_scr):
    i = pl.program_id(1)
    n_tiles = (i * QB + QB + TK - 1) // TK
    nv = TK // SUBLANES
    shape3 = (nv, SUBLANES, QB)
    k_in_tile = (lax.broadcasted_iota(jnp.int32, shape3, 0) * SUBLANES
                 + lax.broadcasted_iota(jnp.int32, shape3, 1))
    q_pos = i * QB + lax.broadcasted_iota(jnp.int32, shape3, 2)
    ksel = float(INDEX_TOPK)

    def all_sublanes(a, op):
        for sh in (4, 2, 1):
            a = op(a, pltpu.roll(a, sh, 0))
        return a

    sg = sg_ref[0]

    def idx_body(j, carry):
        kt = ki_ref[0, pl.ds(pl.multiple_of(j * TK, TK), TK), :]
        acc = jnp.zeros((TK, QB), F32)
        for pr in range(IDX_HEADS // 2):
            x = _dot_nt(kt, qs_ref[0, 2 * pr:2 * pr + 2].reshape(2 * QB, IDX_DIM))
            for e in range(2):
                hd = 2 * pr + e
                acc = acc + sg[hd:hd + 1, :] * jnp.maximum(x[:, e * QB:(e + 1) * QB], 0.0)
        vis = j * TK + k_in_tile <= q_pos
        s_scr[j] = jnp.where(vis, acc.reshape(shape3), -jnp.inf).reshape(TK, QB)
        return carry

    lax.fori_loop(0, n_tiles, idx_body, 0)

    def count(pred):
        def body(j, acc):
            hit = jnp.where(pred(s_scr[j].reshape(shape3), j), 1.0, 0.0)
            return acc + jnp.sum(hit.reshape(nv // 4, 4, SUBLANES, QB), axis=0)
        acc = lax.fori_loop(0, n_tiles, body, jnp.zeros((4, SUBLANES, QB), F32))
        return all_sublanes(jnp.sum(acc, axis=0), jnp.add)

    def bit_body(it, key):
        cand = key | lax.shift_left(jnp.int32(1), 31 - it)
        tc = _key_to_f32(cand)
        cnt = count(lambda s, j: s >= tc)
        below_ninf = lax.shift_right_logical(cand, 23) == 0
        return jnp.where(below_ninf | (cnt >= ksel), cand, key)

    key = lax.fori_loop(0, 32, bit_body, jnp.zeros((SUBLANES, QB), jnp.int32))
    thr = _key_to_f32(key)
    cnt_ge = count(lambda s, j: s >= thr)
    has_excess = jnp.max(cnt_ge) > ksel

    def write_bias(select):
        def body(j, carry):
            s = s_scr[j].reshape(shape3)
            vis = j * TK + k_in_tile <= q_pos
            bias = jnp.where(vis, jnp.where(select(s, j), 0.0, -jnp.inf), -jnp.inf)
            s_scr[j] = bias.reshape(TK, QB)
            return carry
        lax.fori_loop(0, n_tiles, body, 0)

    @pl.when(jnp.logical_not(has_excess))
    def _():
        write_bias(lambda s, j: s >= thr)

    @pl.when(has_excess)
    def _():
        cnt_gt = count(lambda s, j: s > thr)
        need = ksel - cnt_gt

        def jbit_body(it, jkey):
            cand = jkey | lax.shift_left(jnp.int32(1), 10 - it)
            cnt = count(lambda s, j: (s == thr) & (j * TK + k_in_tile < cand))
            return jnp.where(cnt < need, cand, jkey)

        jkey = lax.fori_loop(0, 11, jbit_body, jnp.zeros((SUBLANES, QB), jnp.int32))
        write_bias(lambda s, j: (s > thr) | ((s == thr) & (j * TK + k_in_tile <= jkey)))

    m_scr[...] = jnp.full(m_scr.shape, NEG_BIG, F32)
    l_scr[...] = jnp.zeros(l_scr.shape, F32)
    acc_scr[...] = jnp.zeros(acc_scr.shape, F32)

    def att_body(j, carry):
        kt = k_ref[0, pl.ds(pl.multiple_of(j * TK, TK), TK), :]
        vt = vt_ref[0, j]
        bias = s_scr[j].reshape(shape3)
        for pr in range(N_HEADS // 2):
            sp_scr[pr] = _dot_nt(kt, q_ref[0, 2 * pr:2 * pr + 2].reshape(2 * QB, HEAD_DIM))
        for pr in range(N_HEADS // 2):
            for e in range(2):
                hd = 2 * pr + e
                s = sp_scr[pr, :, e * QB:(e + 1) * QB].reshape(shape3) + bias
                m_old = m_scr[hd]
                m_new = jnp.maximum(m_old, all_sublanes(jnp.max(s, axis=0), jnp.maximum))
                alpha = jnp.exp2(m_old - m_new)
                p = jnp.exp2(s - m_new)
                l_scr[hd] = alpha * l_scr[hd] + jnp.sum(p, axis=0)
                pv = _dot(vt, p.reshape(TK, QB).astype(BF16))
                acc = acc_scr[hd].reshape(HEAD_DIM // SUBLANES, SUBLANES, QB) * alpha
                acc_scr[hd] = acc.reshape(HEAD_DIM, QB) + pv
                m_scr[hd] = m_new
        return carry

    lax.fori_loop(0, n_tiles, att_body, 0)

    outs = []
    for hd in range(N_HEADS):
        l = all_sublanes(l_scr[hd], jnp.add)
        o = acc_scr[hd].reshape(HEAD_DIM // SUBLANES, SUBLANES, QB) / l
        outs.append(o.reshape(HEAD_DIM, QB))
    o_ref[0] = jnp.concatenate(outs, axis=0).T.astype(BF16)


def _ffn_kernel(x_ref, ps_ref, ya_ref, g1_ref, wgate_ref, wpa_ref, wo_ref, g2_ref, wfg_ref, wfu_ref, wfd_ref,
                o_ref):
    x = x_ref[0]
    h = _rms(x, g1_ref[...]).astype(BF16)
    gates = _dot(h, wgate_ref[...])
    pa = _dot(ya_ref[0], wpa_ref[...])
    merged = (jax.nn.sigmoid(gates[:, 0:D_MODEL]) * ps_ref[...].astype(F32)
              + jax.nn.sigmoid(gates[:, D_MODEL:2 * D_MODEL]) * pa)
    x1 = x + _dot(merged.astype(BF16), wo_ref[...])
    h2 = _rms(x1, g2_ref[...]).astype(BF16)
    hid = jax.nn.silu(_dot(h2, wfg_ref[...])) * _dot(h2, wfu_ref[...])
    o_ref[0] = x1 + _dot(hid.astype(BF16), wfd_ref[...])


def _const_spec(shape):
    nd = len(shape)
    return pl.BlockSpec(shape, lambda *_: (0,) * nd, pipeline_mode=pl.Buffered(1))


def _block_diag(blocks):
    g, r, c = blocks.shape
    eye = jnp.eye(g, dtype=blocks.dtype)
    return jnp.einsum('grc,gh->grhc', blocks, eye).reshape(g * r, g * c)


def kernel(x, norm1_g, w_in, A_re, A_im, log_dt, B_re, B_im, C_re, C_im, D_skip, w_glu, b_glu, q_norm_g, k_norm_g,
           idx_k_norm_g, w_proj_ssm, w_proj_attn, w_out, norm2_g, w_ffn_gate, w_ffn_up, w_ffn_down):
    bsz, seq, _ = x.shape
    assert x.shape == (8, 2048, D_MODEL) and w_in.shape[0] == 1
    G, N, P = SSM_GROUPS, SSM_STATE, SSM_GROUP
    cparams = functools.partial(pltpu.CompilerParams, vmem_limit_bytes=VMEM_LIMIT)

    w = w_in[0]
    w_u, w_q, w_k, w_v, w_qi, w_ki, w_wi, w_gates = jnp.split(
        w, (512, 1024, 1088, 1152, 1408, 1472, 1476), axis=1)
    w1 = jnp.concatenate([w_u, w_q, w_qi, w_k, w_ki, w_wi, jnp.zeros((D_MODEL, HEAD_DIM - IDX_HEADS), F32), w_v],
                         axis=1).astype(BF16)
    g1 = norm1_g[0].reshape(1, D_MODEL)

    rep = lambda a: jnp.repeat(a, P, axis=0)
    gpn = jax.ShapeDtypeStruct((G * P, N), F32)
    abar_r, abar_i, bp_r, bp_i = pl.pallas_call(
        _disc_kernel, out_shape=(gpn, gpn, gpn, gpn), name="s5_discretise",
    )(rep(A_re[0]), rep(A_im[0]), rep(jnp.broadcast_to(log_dt[0].reshape(G, 1), (G, N))),
      B_re[0].transpose(0, 2, 1).reshape(G * P, N), B_im[0].transpose(0, 2, 1).reshape(G * P, N))
    abar_r, abar_i = abar_r[::P], abar_i[::P]
    bp_r, bp_i = bp_r.reshape(G, P, N), bp_i.reshape(G, P, N)
    hg = G // 2
    bp = [jnp.concatenate([_block_diag(bp_r[sl]), _block_diag(bp_i[sl])], axis=1).astype(BF16)
          for sl in (slice(0, hg), slice(hg, G))]
    c_r = C_re[0].transpose(0, 2, 1)
    c_i = C_im[0].transpose(0, 2, 1)
    cp = [jnp.concatenate([_block_diag(c_r[sl]), -_block_diag(c_i[sl])], axis=0).astype(BF16)
          for sl in (slice(0, hg), slice(hg, G))]
    ar8 = jnp.broadcast_to(abar_r.reshape(1, G * N), (SUBLANES, G * N))
    ai8 = jnp.broadcast_to(abar_i.reshape(1, G * N), (SUBLANES, G * N))

    n_in = seq // TL_IN
    u_t, q_h, qs_h, sg_t, k_n, ki_n, v_t = pl.pallas_call(
        _inproj_kernel,
        grid=(bsz, n_in),
        in_specs=[
            pl.BlockSpec((1, TL_IN, D_MODEL), lambda b, i: (b, i, 0)),
            _const_spec((1, D_MODEL)),
            _const_spec((D_MODEL, W1_COLS)),
            _const_spec((1, HEAD_DIM)), _const_spec((1, HEAD_DIM)), _const_spec((1, IDX_DIM)),
        ],
        out_specs=[
            pl.BlockSpec((TL_IN, SSM_WIDTH), lambda b, i: (i, b)),
            pl.BlockSpec((1, N_HEADS, TL_IN, HEAD_DIM), lambda b, i: (b, 0, i, 0)),
            pl.BlockSpec((1, IDX_HEADS, TL_IN, IDX_DIM), lambda b, i: (b, 0, i, 0)),
            pl.BlockSpec((1, SUBLANES, TL_IN), lambda b, i: (b, 0, i)),
            pl.BlockSpec((1, TL_IN, HEAD_DIM), lambda b, i: (b, i, 0)),
            pl.BlockSpec((1, TL_IN, IDX_DIM), lambda b, i: (b, i, 0)),
            pl.BlockSpec((1, TL_IN // TK, HEAD_DIM, TK), lambda b, i: (b, i, 0, 0)),
        ],
        out_shape=(
            jax.ShapeDtypeStruct((seq, bsz * SSM_WIDTH), F32),
            jax.ShapeDtypeStruct((bsz, N_HEADS, seq, HEAD_DIM), BF16),
            jax.ShapeDtypeStruct((bsz, IDX_HEADS, seq, IDX_DIM), BF16),
            jax.ShapeDtypeStruct((bsz, SUBLANES, seq), F32),
            jax.ShapeDtypeStruct((bsz, seq, HEAD_DIM), BF16),
            jax.ShapeDtypeStruct((bsz, seq, IDX_DIM), BF16),
            jax.ShapeDtypeStruct((bsz, seq // TK, HEAD_DIM, TK), BF16),
        ),
        compiler_params=cparams(dimension_semantics=("parallel", "parallel")),
        name="in_projection",
    )(x, g1, w1, q_norm_g[0].reshape(1, HEAD_DIM), k_norm_g[0].reshape(1, HEAD_DIM),
      idx_k_norm_g[0].reshape(1, IDX_DIM))

    rows = SUBLANES * TC_SCAN
    ps_t = pl.pallas_call(
        _s5_kernel,
        grid=(seq // TC_SCAN,),
        in_specs=[
            pl.BlockSpec((rows, SSM_WIDTH), lambda c: (c, 0)),
            _const_spec((256, 2048)), _const_spec((256, 2048)),
            _const_spec((2048, 256)), _const_spec((2048, 256)),
            _const_spec((SUBLANES, G * N)), _const_spec((SUBLANES, G * N)),
            _const_spec((1, SSM_WIDTH)), _const_spec((SSM_WIDTH, SSM_WIDTH)), _const_spec((1, SSM_WIDTH)),
            _const_spec((SSM_WIDTH, D_MODEL)),
        ],
        out_specs=pl.BlockSpec((rows, D_MODEL), lambda c: (c, 0)),
        out_shape=jax.ShapeDtypeStruct((seq * bsz, D_MODEL), BF16),
        scratch_shapes=[pltpu.VMEM((rows, 2 * G * N), F32), pltpu.VMEM((SUBLANES, 2 * G * N), F32)],
        compiler_params=cparams(dimension_semantics=("arbitrary",)),
        name="s5_branch",
    )(u_t.reshape(seq * bsz, SSM_WIDTH), bp[0], bp[1], cp[0], cp[1], ar8, ai8,
      D_skip[0].reshape(1, SSM_WIDTH), w_glu[0].astype(BF16), b_glu[0].reshape(1, SSM_WIDTH),
      w_proj_ssm[0].astype(BF16))

    y_att = pl.pallas_call(
        _attn_kernel,
        grid=(bsz, seq // QB),
        in_specs=[
            pl.BlockSpec((1, IDX_HEADS, QB, IDX_DIM), lambda b, i: (b, 0, i, 0)),
            pl.BlockSpec((1, SUBLANES, QB), lambda b, i: (b, 0, i)),
            pl.BlockSpec((1, N_HEADS, QB, HEAD_DIM), lambda b, i: (b, 0, i, 0)),
            pl.BlockSpec((1, seq, IDX_DIM), lambda b, i: (b, 0, 0)),
            pl.BlockSpec((1, seq, HEAD_DIM), lambda b, i: (b, 0, 0)),
            pl.BlockSpec((1, seq // TK, HEAD_DIM, TK), lambda b, i: (b, 0, 0, 0)),
        ],
        out_specs=pl.BlockSpec((1, QB, ATTN_WIDTH), lambda b, i: (b, i, 0)),
        out_shape=jax.ShapeDtypeStruct((bsz, seq, ATTN_WIDTH), BF16),
        scratch_shapes=[pltpu.VMEM((seq // TK, TK, QB), F32),
                        pltpu.VMEM((N_HEADS, SUBLANES, QB), F32),
                        pltpu.VMEM((N_HEADS, SUBLANES, QB), F32),
                        pltpu.VMEM((N_HEADS, HEAD_DIM, QB), F32),
                        pltpu.VMEM((N_HEADS // 2, TK, 2 * QB), F32)],
        compiler_params=cparams(dimension_semantics=("parallel", "arbitrary")),
        name="sparse_attention",
    )(qs_h, sg_t, q_h, ki_n, k_n, v_t)

    out = pl.pallas_call(
        _ffn_kernel,
        grid=(bsz, seq // TL_FFN),
        in_specs=[
            pl.BlockSpec((1, TL_FFN, D_MODEL), lambda b, i: (b, i, 0)),
            pl.BlockSpec((TL_FFN, D_MODEL), lambda b, i: (i, b)),
            pl.BlockSpec((1, TL_FFN, ATTN_WIDTH), lambda b, i: (b, i, 0)),
            _const_spec((1, D_MODEL)),
            _const_spec((D_MODEL, 2 * D_MODEL)),
            _const_spec((ATTN_WIDTH, D_MODEL)),
            _const_spec((D_MODEL, D_MODEL)),
            _const_spec((1, D_MODEL)),
            _const_spec((D_MODEL, D_FF)), _const_spec((D_MODEL, D_FF)), _const_spec((D_FF, D_MODEL)),
        ],
        out_specs=pl.BlockSpec((1, TL_FFN, D_MODEL), lambda b, i: (b, i, 0)),
        out_shape=jax.ShapeDtypeStruct((bsz, seq, D_MODEL), F32),
        compiler_params=cparams(dimension_semantics=("parallel", "parallel")),
        name="merge_ffn",
    )(x, ps_t.reshape(seq, bsz * D_MODEL), y_att, g1, w_gates.astype(BF16), w_proj_attn[0].astype(BF16),
      w_out[0].astype(BF16), norm2_g[0].reshape(1, D_MODEL), w_ffn_gate[0].astype(BF16),
      w_ffn_up[0].astype(BF16), w_ffn_down[0].astype(BF16))
    return out
```

```python
import functools
import math

import jax
import jax.numpy as jnp
from jax import lax
from jax.experimental import pallas as pl
from jax.experimental.pallas import tpu as pltpu

F32 = jnp.float32
BF16 = jnp.bfloat16

D_MODEL = 1024
SSM_WIDTH = 512
SSM_GROUP = 16
SSM_GROUPS = 32
SSM_STATE = 64
N_HEADS = 8
HEAD_DIM = 64
ATTN_WIDTH = 512
IDX_HEADS = 4
IDX_DIM = 64
INDEX_TOPK = 256
D_FF = 2816
RMS_EPS = 1e-6

LANES = 128
SUBLANES = 8
VMEM_LIMIT = 56 * 1024 * 1024

TL_IN = 512
TC_SCAN = 64
QB = 128
TK = 256
TL_FFN = 256

W1_COLS = 1536
NEG_BIG = -1e30
assert QB == LANES


def _dot(a, b):
    return jnp.dot(a, b, preferred_element_type=F32)


def _dot_nt(a, b):
    return lax.dot_general(a, b, (((1,), (1,)), ((), ())), preferred_element_type=F32)


def _rms(x, g):
    return x * lax.rsqrt(jnp.mean(x * x, axis=-1, keepdims=True) + RMS_EPS) * g


def _disc_kernel(are_ref, aim_ref, ldt_ref, bre_ref, bim_ref, abr_ref, abi_ref, bpr_ref, bpi_ref):
    ar = are_ref[...]
    ai = aim_ref[...]
    dt = jnp.exp(ldt_ref[...])
    mag = jnp.exp(ar * dt)
    abar_r = mag * jnp.cos(ai * dt)
    abar_i = mag * jnp.sin(ai * dt)
    den = ar * ar + ai * ai
    nr = abar_r - 1.0
    coef_r = (nr * ar + abar_i * ai) / den
    coef_i = (abar_i * ar - nr * ai) / den
    abr_ref[...] = abar_r
    abi_ref[...] = abar_i
    br = bre_ref[...]
    bi = bim_ref[...]
    bpr_ref[...] = coef_r * br - coef_i * bi
    bpi_ref[...] = coef_r * bi + coef_i * br


def _inproj_kernel(x_ref, g1_ref, w1_ref, qg_ref, kg_ref, seg_ref,
                   u_ref, q_ref, qs_ref, sg_ref, k_ref, ki_ref, v_ref):
    x = x_ref[0]
    h = _rms(x, g1_ref[...]).astype(BF16)
    proj = _dot(h, w1_ref[...])
    u_ref[0] = proj[:, 0:512]
    seg = seg_ref[...]

    def head_rms(v, g):
        sq = (v * v).astype(BF16)
        n = v.shape[1]
        w = min(n, seg.shape[0])
        ss = jnp.concatenate([_dot(sq[:, c:c + w], seg[0:w, 0:w]) for c in range(0, n, w)], axis=1)
        return v * lax.rsqrt(ss * (1.0 / HEAD_DIM) + RMS_EPS) * g

    qscale = (HEAD_DIM ** -0.5) * math.log2(math.e)
    qn = (head_rms(proj[:, 512:1024], qg_ref[...]) * qscale).astype(BF16)
    for hd in range(N_HEADS):
        q_ref[0, hd] = qn[:, hd * HEAD_DIM:(hd + 1) * HEAD_DIM]
    w_scale = (IDX_HEADS ** -0.5) * (IDX_DIM ** -0.5)
    wabs = jnp.abs(proj[:, 1408:1536]) * w_scale
    sshape = (2 * HEAD_DIM, IDX_HEADS * IDX_DIM)
    row = lax.broadcasted_iota(jnp.int32, sshape, 0)
    col = lax.broadcasted_iota(jnp.int32, sshape, 1)
    spread = jnp.where(row == col // IDX_DIM, 1.0, 0.0).astype(BF16)
    whi = wabs.astype(BF16)
    wlo = (wabs - whi.astype(F32)).astype(BF16)
    wrep = _dot(whi, spread) + _dot(wlo, spread)
    qs = (proj[:, 1024:1280] * wrep).astype(BF16)
    for hd in range(IDX_HEADS):
        qs_ref[0, hd] = qs[:, hd * IDX_DIM:(hd + 1) * IDX_DIM]
    kk = head_rms(proj[:, 1280:1408], kg_ref[...]).astype(BF16)
    k_ref[0] = kk[:, 0:HEAD_DIM]
    ki_ref[0] = kk[:, HEAD_DIM:2 * HEAD_DIM]
    t = proj[:, 1408:1536].T
    sg_ref[0] = jnp.where(t[0:SUBLANES] >= 0, 1.0, -1.0)
    for c in range(TL_IN // TK):
        v_ref[0, c] = t[HEAD_DIM:2 * HEAD_DIM, c * TK:(c + 1) * TK].astype(BF16)


def _s5_kernel(u_ref, bplo_ref, bphi_ref, cplo_ref, cphi_ref, ar_ref, ai_ref, d_ref, wglu_ref, bglu_ref,
               wps_ref, o_ref, bu_scr, st_scr):
    @pl.when(pl.program_id(0) == 0)
    def _():
        st_scr[...] = jnp.zeros_like(st_scr)

    u = pltpu.einshape("btc->(tb)c", u_ref[...])
    ub = u.astype(BF16)
    bu_scr[:, 0:2048] = _dot(ub[:, 0:256], bplo_ref[...])
    bu_scr[:, 2048:4096] = _dot(ub[:, 256:512], bphi_ref[...])
    ar = ar_ref[...]
    ai = ai_ref[...]

    def step(t, s):
        r0 = pl.multiple_of(t * SUBLANES, SUBLANES)
        b = bu_scr[pl.ds(r0, SUBLANES), :]
        parts = []
        for half in range(2):
            o = half * 2048
            sr, si = s[:, o:o + 1024], s[:, o + 1024:o + 2048]
            br, bi = b[:, o:o + 1024], b[:, o + 1024:o + 2048]
            a_r, a_i = ar[:, half * 1024:(half + 1) * 1024], ai[:, half * 1024:(half + 1) * 1024]
            parts.append(a_r * sr - a_i * si + br)
            parts.append(a_r * si + a_i * sr + bi)
        s2 = jnp.concatenate(parts, axis=1)
        bu_scr[pl.ds(r0, SUBLANES), :] = s2
        return s2

    s_fin = lax.fori_loop(0, TC_SCAN, step, st_scr[...])
    st_scr[...] = s_fin
    sb = bu_scr[...].astype(BF16)
    y = jnp.concatenate([_dot(sb[:, 0:2048], cplo_ref[...]), _dot(sb[:, 2048:4096], cphi_ref[...])], axis=1)
    y = jax.nn.gelu(y + d_ref[...] * u)
    z = _dot(y.astype(BF16), wglu_ref[...]) + bglu_ref[...]
    y = y * jax.nn.sigmoid(z)
    ps = _dot(y.astype(BF16), wps_ref[...])
    o_ref[...] = pltpu.einshape("(tb)c->btc", ps, b=SUBLANES).astype(BF16)


def _key_to_f32(key):
    bits = jnp.where(key < 0, key & jnp.int32(0x7FFFFFFF), ~key)
    return pltpu.bitcast(bits, F32)


def _attn_kernel(qs_ref, sg_ref, q_ref, ki_ref, k_ref, vt_ref, o_ref, s_scr, m_scr, l_scr, acc_scr, sp_scr):
    i = pl.program_id(1)
    n_tiles = (i * QB + QB + TK - 1) // TK
    nv = TK // SUBLANES
    shape3 = (nv, SUBLANES, QB)
    k_in_tile = (lax.broadcasted_iota(jnp.int32, shape3, 0) * SUBLANES
                 + lax.broadcasted_iota(jnp.int32, shape3, 1))
    q_pos = i * QB + lax.broadcasted_iota(jnp.int32, shape3, 2)
    ksel = float(INDEX_TOPK)

    def all_sublanes(a, op):
        for sh in (4, 2, 1):
            a = op(a, pltpu.roll(a, sh, 0))
        return a

    sg = sg_ref[0]

    def idx_body(j, carry):
        kt = ki_ref[0, pl.ds(pl.multiple_of(j * TK, TK), TK), :]
        acc = jnp.zeros((TK, QB), F32)
        for pr in range(IDX_HEADS // 2):
            x = _dot_nt(kt, qs_ref[0, 2 * pr:2 * pr + 2].reshape(2 * QB, IDX_DIM))
            for e in range(2):
                hd = 2 * pr + e
                acc = acc + sg[hd:hd + 1, :] * jnp.maximum(x[:, e * QB:(e + 1) * QB], 0.0)
        vis = j * TK + k_in_tile <= q_pos
        s_scr[j] = jnp.where(vis, acc.reshape(shape3), -jnp.inf).reshape(TK, QB)
        return carry

    lax.fori_loop(0, n_tiles, idx_body, 0)

    def count(pred):
        def body(j, acc):
            hit = jnp.where(pred(s_scr[j].reshape(shape3), j), 1.0, 0.0)
            return acc + jnp.sum(hit.reshape(nv // 4, 4, SUBLANES, QB), axis=0)
        acc = lax.fori_loop(0, n_tiles, body, jnp.zeros((4, SUBLANES, QB), F32))
        return all_sublanes(jnp.sum(acc, axis=0), jnp.add)

    def bit_body(it, key):
        cand = key | lax.shift_left(jnp.int32(1), 31 - it)
        tc = _key_to_f32(cand)
        cnt = count(lambda s, j: s >= tc)
        below_ninf = lax.shift_right_logical(cand, 23) == 0
        return jnp.where(below_ninf | (cnt >= ksel), cand, key)

    key = lax.fori_loop(0, 32, bit_body, jnp.zeros((SUBLANES, QB), jnp.int32))
    thr = _key_to_f32(key)
    cnt_ge = count(lambda s, j: s >= thr)
    has_excess = jnp.max(cnt_ge) > ksel

    def write_bias(select):
        def body(j, carry):
            s = s_scr[j].reshape(shape3)
            vis = j * TK + k_in_tile <= q_pos
            bias = jnp.where(vis, jnp.where(select(s, j), 0.0, -jnp.inf), -jnp.inf)
            s_scr[j] = bias.reshape(TK, QB)
            return carry
        lax.fori_loop(0, n_tiles, body, 0)

    @pl.when(jnp.logical_not(has_excess))
    def _():
        write_bias(lambda s, j: s >= thr)

    @pl.when(has_excess)
    def _():
        cnt_gt = count(lambda s, j: s > thr)
        need = ksel - cnt_gt

        def jbit_body(it, jkey):
            cand = jkey | lax.shift_left(jnp.int32(1), 10 - it)
            cnt = count(lambda s, j: (s == thr) & (j * TK + k_in_tile < cand))
            return jnp.where(cnt < need, cand, jkey)

        jkey = lax.fori_loop(0, 11, jbit_body, jnp.zeros((SUBLANES, QB), jnp.int32))
        write_bias(lambda s, j: (s > thr) | ((s == thr) & (j * TK + k_in_tile <= jkey)))

    m_scr[...] = jnp.full(m_scr.shape, NEG_BIG, F32)
    l_scr[...] = jnp.zeros(l_scr.shape, F32)
    acc_scr[...] = jnp.zeros(acc_scr.shape, F32)

    def att_body(j, carry):
        kt = k_ref[0, pl.ds(pl.multiple_of(j * TK, TK), TK), :]
        vt = vt_ref[0, j]
        bias = s_scr[j].reshape(shape3)
        for pr in range(N_HEADS // 2):
            sp_scr[pr] = _dot_nt(kt, q_ref[0, 2 * pr:2 * pr + 2].reshape(2 * QB, HEAD_DIM))
        for pr in range(N_HEADS // 2):
            for e in range(2):
                hd = 2 * pr + e
                s = sp_scr[pr, :, e * QB:(e + 1) * QB].reshape(shape3) + bias
                m_old = m_scr[hd]
                m_new = jnp.maximum(m_old, all_sublanes(jnp.max(s, axis=0), jnp.maximum))
                alpha = jnp.exp2(m_old - m_new)
                p = jnp.exp2(s - m_new)
                l_scr[hd] = alpha * l_scr[hd] + jnp.sum(p, axis=0)
                pv = _dot(vt, p.reshape(TK, QB).astype(BF16))
                acc = acc_scr[hd].reshape(HEAD_DIM // SUBLANES, SUBLANES, QB) * alpha
                acc_scr[hd] = acc.reshape(HEAD_DIM, QB) + pv
                m_scr[hd] = m_new
        return carry

    lax.fori_loop(0, n_tiles, att_body, 0)

    outs = []
    for hd in range(N_HEADS):
        l = all_sublanes(l_scr[hd], jnp.add)
        o = acc_scr[hd].reshape(HEAD_DIM // SUBLANES, SUBLANES, QB) / l
        outs.append(o.reshape(HEAD_DIM, QB))
    o_ref[0] = jnp.concatenate(outs, axis=0).T.astype(BF16)


def _ffn_kernel(x_ref, ps_ref, ya_ref, g1_ref, wgate_ref, wpa_ref, wo_ref, g2_ref, wfg_ref, wfu_ref, wfd_ref,
                o_ref):
    x = x_ref[0]
    h = _rms(x, g1_ref[...]).astype(BF16)
    gates = _dot(h, wgate_ref[...])
    pa = _dot(ya_ref[0], wpa_ref[...])
    merged = (jax.nn.sigmoid(gates[:, 0:D_MODEL]) * ps_ref[0].astype(F32)
              + jax.nn.sigmoid(gates[:, D_MODEL:2 * D_MODEL]) * pa)
    x1 = x + _dot(merged.astype(BF16), wo_ref[...])
    h2 = _rms(x1, g2_ref[...]).astype(BF16)
    hid = jax.nn.silu(_dot(h2, wfg_ref[...])) * _dot(h2, wfu_ref[...])
    o_ref[0] = x1 + _dot(hid.astype(BF16), wfd_ref[...])


def _const_spec(shape):
    nd = len(shape)
    return pl.BlockSpec(shape, lambda *_: (0,) * nd, pipeline_mode=pl.Buffered(1))


def _block_diag(blocks):
    g, r, c = blocks.shape
    eye = jnp.eye(g, dtype=blocks.dtype)
    return jnp.einsum('grc,gh->grhc', blocks, eye).reshape(g * r, g * c)


def kernel(x, norm1_g, w_in, A_re, A_im, log_dt, B_re, B_im, C_re, C_im, D_skip, w_glu, b_glu, q_norm_g, k_norm_g,
           idx_k_norm_g, w_proj_ssm, w_proj_attn, w_out, norm2_g, w_ffn_gate, w_ffn_up, w_ffn_down):
    bsz, seq, _ = x.shape
    assert x.shape == (8, 2048, D_MODEL) and w_in.shape[0] == 1
    G, N, P = SSM_GROUPS, SSM_STATE, SSM_GROUP
    cparams = functools.partial(pltpu.CompilerParams, vmem_limit_bytes=VMEM_LIMIT)

    w = w_in[0]
    w_u, w_q, w_k, w_v, w_qi, w_ki, w_wi, w_gates = jnp.split(
        w, (512, 1024, 1088, 1152, 1408, 1472, 1476), axis=1)
    w1 = jnp.concatenate([w_u, w_q, w_qi, w_k, w_ki, w_wi, jnp.zeros((D_MODEL, HEAD_DIM - IDX_HEADS), F32), w_v],
                         axis=1).astype(BF16)
    g1 = norm1_g[0].reshape(1, D_MODEL)

    rep = lambda a: jnp.repeat(a, P, axis=0)
    gpn = jax.ShapeDtypeStruct((G * P, N), F32)
    abar_r, abar_i, bp_r, bp_i = pl.pallas_call(
        _disc_kernel, out_shape=(gpn, gpn, gpn, gpn), name="s5_discretise",
    )(rep(A_re[0]), rep(A_im[0]), rep(jnp.broadcast_to(log_dt[0].reshape(G, 1), (G, N))),
      B_re[0].transpose(0, 2, 1).reshape(G * P, N), B_im[0].transpose(0, 2, 1).reshape(G * P, N))
    abar_r, abar_i = abar_r[::P], abar_i[::P]
    bp_r, bp_i = bp_r.reshape(G, P, N), bp_i.reshape(G, P, N)
    hg = G // 2
    bp = [jnp.concatenate([_block_diag(bp_r[sl]), _block_diag(bp_i[sl])], axis=1).astype(BF16)
          for sl in (slice(0, hg), slice(hg, G))]
    c_r = C_re[0].transpose(0, 2, 1)
    c_i = C_im[0].transpose(0, 2, 1)
    cp = [jnp.concatenate([_block_diag(c_r[sl]), -_block_diag(c_i[sl])], axis=0).astype(BF16)
          for sl in (slice(0, hg), slice(hg, G))]
    ar8 = jnp.broadcast_to(abar_r.reshape(1, G * N), (SUBLANES, G * N))
    ai8 = jnp.broadcast_to(abar_i.reshape(1, G * N), (SUBLANES, G * N))

    n_in = seq // TL_IN
    u_t, q_h, qs_h, sg_t, k_n, ki_n, v_t = pl.pallas_call(
        _inproj_kernel,
        grid=(bsz, n_in),
        in_specs=[
            pl.BlockSpec((1, TL_IN, D_MODEL), lambda b, i: (b, i, 0)),
            _const_spec((1, D_MODEL)),
            _const_spec((D_MODEL, W1_COLS)),
            _const_spec((1, ATTN_WIDTH)), _const_spec((1, 2 * HEAD_DIM)), _const_spec((2 * LANES, 2 * LANES)),
        ],
        out_specs=[
            pl.BlockSpec((1, TL_IN, SSM_WIDTH), lambda b, i: (b, i, 0)),
            pl.BlockSpec((1, N_HEADS, TL_IN, HEAD_DIM), lambda b, i: (b, 0, i, 0)),
            pl.BlockSpec((1, IDX_HEADS, TL_IN, IDX_DIM), lambda b, i: (b, 0, i, 0)),
            pl.BlockSpec((1, SUBLANES, TL_IN), lambda b, i: (b, 0, i)),
            pl.BlockSpec((1, TL_IN, HEAD_DIM), lambda b, i: (b, i, 0)),
            pl.BlockSpec((1, TL_IN, IDX_DIM), lambda b, i: (b, i, 0)),
            pl.BlockSpec((1, TL_IN // TK, HEAD_DIM, TK), lambda b, i: (b, i, 0, 0)),
        ],
        out_shape=(
            jax.ShapeDtypeStruct((bsz, seq, SSM_WIDTH), F32),
            jax.ShapeDtypeStruct((bsz, N_HEADS, seq, HEAD_DIM), BF16),
            jax.ShapeDtypeStruct((bsz, IDX_HEADS, seq, IDX_DIM), BF16),
            jax.ShapeDtypeStruct((bsz, SUBLANES, seq), F32),
            jax.ShapeDtypeStruct((bsz, seq, HEAD_DIM), BF16),
            jax.ShapeDtypeStruct((bsz, seq, IDX_DIM), BF16),
            jax.ShapeDtypeStruct((bsz, seq // TK, HEAD_DIM, TK), BF16),
        ),
        compiler_params=cparams(dimension_semantics=("parallel", "parallel")),
        name="in_projection",
    )(x, g1, w1, jnp.tile(q_norm_g[0], N_HEADS).reshape(1, ATTN_WIDTH),
      jnp.concatenate([k_norm_g[0], idx_k_norm_g[0]]).reshape(1, 2 * HEAD_DIM),
      jnp.kron(jnp.eye(2 * LANES // HEAD_DIM, dtype=BF16), jnp.ones((HEAD_DIM, HEAD_DIM), BF16)))

    rows = SUBLANES * TC_SCAN
    ps_t = pl.pallas_call(
        _s5_kernel,
        grid=(seq // TC_SCAN,),
        in_specs=[
            pl.BlockSpec((bsz, TC_SCAN, SSM_WIDTH), lambda c: (0, c, 0)),
            _const_spec((256, 2048)), _const_spec((256, 2048)),
            _const_spec((2048, 256)), _const_spec((2048, 256)),
            _const_spec((SUBLANES, G * N)), _const_spec((SUBLANES, G * N)),
            _const_spec((1, SSM_WIDTH)), _const_spec((SSM_WIDTH, SSM_WIDTH)), _const_spec((1, SSM_WIDTH)),
            _const_spec((SSM_WIDTH, D_MODEL)),
        ],
        out_specs=pl.BlockSpec((bsz, TC_SCAN, D_MODEL), lambda c: (0, c, 0)),
        out_shape=jax.ShapeDtypeStruct((bsz, seq, D_MODEL), BF16),
        scratch_shapes=[pltpu.VMEM((rows, 2 * G * N), F32), pltpu.VMEM((SUBLANES, 2 * G * N), F32)],
        compiler_params=cparams(dimension_semantics=("arbitrary",)),
        name="s5_branch",
    )(u_t, bp[0], bp[1], cp[0], cp[1], ar8, ai8,
      D_skip[0].reshape(1, SSM_WIDTH), w_glu[0].astype(BF16), b_glu[0].reshape(1, SSM_WIDTH),
      w_proj_ssm[0].astype(BF16))

    y_att = pl.pallas_call(
        _attn_kernel,
        grid=(bsz, seq // QB),
        in_specs=[
            pl.BlockSpec((1, IDX_HEADS, QB, IDX_DIM), lambda b, i: (b, 0, i, 0)),
            pl.BlockSpec((1, SUBLANES, QB), lambda b, i: (b, 0, i)),
            pl.BlockSpec((1, N_HEADS, QB, HEAD_DIM), lambda b, i: (b, 0, i, 0)),
            pl.BlockSpec((1, seq, IDX_DIM), lambda b, i: (b, 0, 0)),
            pl.BlockSpec((1, seq, HEAD_DIM), lambda b, i: (b, 0, 0)),
            pl.BlockSpec((1, seq // TK, HEAD_DIM, TK), lambda b, i: (b, 0, 0, 0)),
        ],
        out_specs=pl.BlockSpec((1, QB, ATTN_WIDTH), lambda b, i: (b, i, 0)),
        out_shape=jax.ShapeDtypeStruct((bsz, seq, ATTN_WIDTH), BF16),
        scratch_shapes=[pltpu.VMEM((seq // TK, TK, QB), F32),
                        pltpu.VMEM((N_HEADS, SUBLANES, QB), F32),
                        pltpu.VMEM((N_HEADS, SUBLANES, QB), F32),
                        pltpu.VMEM((N_HEADS, HEAD_DIM, QB), F32),
                        pltpu.VMEM((N_HEADS // 2, TK, 2 * QB), F32)],
        compiler_params=cparams(dimension_semantics=("parallel", "arbitrary")),
        name="sparse_attention",
    )(qs_h, sg_t, q_h, ki_n, k_n, v_t)

    out = pl.pallas_call(
        _ffn_kernel,
        grid=(bsz, seq // TL_FFN),
        in_specs=[
            pl.BlockSpec((1, TL_FFN, D_MODEL), lambda b, i: (b, i, 0)),
            pl.BlockSpec((1, TL_FFN, D_MODEL), lambda b, i: (b, i, 0)),
            pl.BlockSpec((1, TL_FFN, ATTN_WIDTH), lambda b, i: (b, i, 0)),
            _const_spec((1, D_MODEL)),
            _const_spec((D_MODEL, 2 * D_MODEL)),
            _const_spec((ATTN_WIDTH, D_MODEL)),
            _const_spec((D_MODEL, D_MODEL)),
            _const_spec((1, D_MODEL)),
            _const_spec((D_MODEL, D_FF)), _const_spec((D_MODEL, D_FF)), _const_spec((D_FF, D_MODEL)),
        ],
        out_specs=pl.BlockSpec((1, TL_FFN, D_MODEL), lambda b, i: (b, i, 0)),
        out_shape=jax.ShapeDtypeStruct((bsz, seq, D_MODEL), F32),
        compiler_params=cparams(dimension_semantics=("parallel", "parallel")),
        name="merge_ffn",
    )(x, ps_t, y_att, g1, w_gates.astype(BF16), w_proj_attn[0].astype(BF16),
      w_out[0].astype(BF16), norm2_g[0].reshape(1, D_MODEL), w_ffn_gate[0].astype(BF16),
      w_ffn_up[0].astype(BF16), w_ffn_down[0].astype(BF16))
    return out
```

```python
import functools
import math

import jax
import jax.numpy as jnp
from jax import lax
from jax.experimental import pallas as pl
from jax.experimental.pallas import tpu as pltpu

F32 = jnp.float32
BF16 = jnp.bfloat16

D_MODEL = 1024
SSM_WIDTH = 512
SSM_GROUP = 16
SSM_GROUPS = 32
SSM_STATE = 64
N_HEADS = 8
HEAD_DIM = 64
ATTN_WIDTH = 512
IDX_HEADS = 4
IDX_DIM = 64
INDEX_TOPK = 256
D_FF = 2816
RMS_EPS = 1e-6

LANES = 128
SUBLANES = 8
VMEM_LIMIT = 56 * 1024 * 1024

TL_IN = 512
TC_SCAN = 64
QB = 128
TK = 256
TL_FFN = 256

W1_COLS = 1536
NEG_BIG = -1e30
assert QB == LANES


def _dot(a, b):
    return jnp.dot(a, b, preferred_element_type=F32)


def _rms(x, g):
    return x * lax.rsqrt(jnp.mean(x * x, axis=-1, keepdims=True) + RMS_EPS) * g


def _disc_kernel(are_ref, aim_ref, ldt_ref, bre_ref, bim_ref, abr_ref, abi_ref, bpr_ref, bpi_ref):
    ar = are_ref[...]
    ai = aim_ref[...]
    dt = jnp.exp(ldt_ref[...])
    mag = jnp.exp(ar * dt)
    abar_r = mag * jnp.cos(ai * dt)
    abar_i = mag * jnp.sin(ai * dt)
    den = ar * ar + ai * ai
    nr = abar_r - 1.0
    coef_r = (nr * ar + abar_i * ai) / den
    coef_i = (abar_i * ar - nr * ai) / den
    abr_ref[...] = abar_r
    abi_ref[...] = abar_i
    br = bre_ref[...]
    bi = bim_ref[...]
    bpr_ref[...] = coef_r * br - coef_i * bi
    bpi_ref[...] = coef_r * bi + coef_i * br


def _inproj_kernel(x_ref, g1_ref, w1_ref, qg_ref, kg_ref, seg_ref,
                   u_ref, q_ref, qs_ref, sg_ref, k_ref, ki_ref, v_ref):
    x = x_ref[0]
    h = _rms(x, g1_ref[...]).astype(BF16)
    proj = _dot(h, w1_ref[...])
    u_ref[0] = proj[:, 0:512]
    seg = seg_ref[...]

    def head_rms(v, g):
        sq = (v * v).astype(BF16)
        n = v.shape[1]
        w = min(n, seg.shape[0])
        ss = jnp.concatenate([_dot(sq[:, c:c + w], seg[0:w, 0:w]) for c in range(0, n, w)], axis=1)
        return v * lax.rsqrt(ss * (1.0 / HEAD_DIM) + RMS_EPS) * g

    qscale = (HEAD_DIM ** -0.5) * math.log2(math.e)
    qn = head_rms(proj[:, 512:1024], qg_ref[...]) * qscale

    def store_heads_transposed(ref, v):
        for g in range(v.shape[1] // LANES):
            tg = v[:, g * LANES:(g + 1) * LANES].T
            ref[0, 2 * g] = tg[0:HEAD_DIM].astype(BF16)
            ref[0, 2 * g + 1] = tg[HEAD_DIM:2 * HEAD_DIM].astype(BF16)

    store_heads_transposed(q_ref, qn)
    w_scale = (IDX_HEADS ** -0.5) * (IDX_DIM ** -0.5)
    wabs = jnp.abs(proj[:, 1408:1536]) * w_scale
    sshape = (2 * HEAD_DIM, IDX_HEADS * IDX_DIM)
    row = lax.broadcasted_iota(jnp.int32, sshape, 0)
    col = lax.broadcasted_iota(jnp.int32, sshape, 1)
    spread = jnp.where(row == col // IDX_DIM, 1.0, 0.0).astype(BF16)
    whi = wabs.astype(BF16)
    wlo = (wabs - whi.astype(F32)).astype(BF16)
    wrep = _dot(whi, spread) + _dot(wlo, spread)
    store_heads_transposed(qs_ref, proj[:, 1024:1280] * wrep)
    kk = head_rms(proj[:, 1280:1408], kg_ref[...]).astype(BF16)
    k_ref[0] = kk[:, 0:HEAD_DIM]
    ki_ref[0] = kk[:, HEAD_DIM:2 * HEAD_DIM]
    t = proj[:, 1408:1536].T
    sg_ref[0] = jnp.where(t[0:SUBLANES] >= 0, 1.0, -1.0)
    for c in range(TL_IN // TK):
        v_ref[0, c] = t[HEAD_DIM:2 * HEAD_DIM, c * TK:(c + 1) * TK].astype(BF16)


def _s5_kernel(u_ref, bplo_ref, bphi_ref, cplo_ref, cphi_ref, ar_ref, ai_ref, d_ref, wglu_ref, bglu_ref,
               wps_ref, o_ref, bu_scr, st_scr):
    @pl.when(pl.program_id(0) == 0)
    def _():
        st_scr[...] = jnp.zeros_like(st_scr)

    u = jnp.transpose(u_ref[...], (1, 0, 2)).reshape(SUBLANES * TC_SCAN, SSM_WIDTH)
    ub = u.astype(BF16)
    bu_scr[:, 0:2048] = _dot(ub[:, 0:256], bplo_ref[...])
    bu_scr[:, 2048:4096] = _dot(ub[:, 256:512], bphi_ref[...])
    ar = ar_ref[...]
    ai = ai_ref[...]

    def step(t, s):
        r0 = pl.multiple_of(t * SUBLANES, SUBLANES)
        b = bu_scr[pl.ds(r0, SUBLANES), :]
        parts = []
        for half in range(2):
            o = half * 2048
            sr, si = s[:, o:o + 1024], s[:, o + 1024:o + 2048]
            br, bi = b[:, o:o + 1024], b[:, o + 1024:o + 2048]
            a_r, a_i = ar[:, half * 1024:(half + 1) * 1024], ai[:, half * 1024:(half + 1) * 1024]
            parts.append(a_r * sr - a_i * si + br)
            parts.append(a_r * si + a_i * sr + bi)
        s2 = jnp.concatenate(parts, axis=1)
        bu_scr[pl.ds(r0, SUBLANES), :] = s2
        return s2

    s_fin = lax.fori_loop(0, TC_SCAN, step, st_scr[...])
    st_scr[...] = s_fin
    sb = bu_scr[...].astype(BF16)
    y = jnp.concatenate([_dot(sb[:, 0:2048], cplo_ref[...]), _dot(sb[:, 2048:4096], cphi_ref[...])], axis=1)
    y = jax.nn.gelu(y + d_ref[...] * u)
    z = _dot(y.astype(BF16), wglu_ref[...]) + bglu_ref[...]
    y = y * jax.nn.sigmoid(z)
    ps = _dot(y.astype(BF16), wps_ref[...])
    o_ref[...] = jnp.transpose(ps.reshape(TC_SCAN, SUBLANES, D_MODEL), (1, 0, 2)).astype(BF16)


def _key_to_f32(key):
    bits = jnp.where(key < 0, key & jnp.int32(0x7FFFFFFF), ~key)
    return pltpu.bitcast(bits, F32)


def _attn_kernel(qs_ref, sg_ref, q_ref, ki_ref, k_ref, vt_ref, o_ref, s_scr, m_scr, l_scr, acc_scr, sp_scr):
    i = pl.program_id(1)
    n_tiles = (i * QB + QB + TK - 1) // TK
    nv = TK // SUBLANES
    shape3 = (nv, SUBLANES, QB)
    k_in_tile = (lax.broadcasted_iota(jnp.int32, shape3, 0) * SUBLANES
                 + lax.broadcasted_iota(jnp.int32, shape3, 1))
    q_pos = i * QB + lax.broadcasted_iota(jnp.int32, shape3, 2)
    ksel = float(INDEX_TOPK)

    def all_sublanes(a, op):
        for sh in (4, 2, 1):
            a = op(a, pltpu.roll(a, sh, 0))
        return a

    sg = sg_ref[0]

    def idx_body(j, carry):
        kt = ki_ref[0, pl.ds(pl.multiple_of(j * TK, TK), TK), :]
        acc = jnp.zeros((TK, QB), F32)
        for pr in range(IDX_HEADS // 2):
            x = _dot(kt, jnp.concatenate([qs_ref[0, 2 * pr], qs_ref[0, 2 * pr + 1]], axis=1))
            for e in range(2):
                hd = 2 * pr + e
                acc = acc + sg[hd:hd + 1, :] * jnp.maximum(x[:, e * QB:(e + 1) * QB], 0.0)
        vis = j * TK + k_in_tile <= q_pos
        s_scr[j] = jnp.where(vis, acc.reshape(shape3), -jnp.inf).reshape(TK, QB)
        return carry

    lax.fori_loop(0, n_tiles, idx_body, 0)

    def count(pred):
        def body(j, acc):
            hit = jnp.where(pred(s_scr[j].reshape(shape3), j), 1.0, 0.0)
            return acc + jnp.sum(hit.reshape(nv // 4, 4, SUBLANES, QB), axis=0)
        acc = lax.fori_loop(0, n_tiles, body, jnp.zeros((4, SUBLANES, QB), F32))
        return all_sublanes(jnp.sum(acc, axis=0), jnp.add)

    def bit_body(it, key):
        cand = key | lax.shift_left(jnp.int32(1), 31 - it)
        tc = _key_to_f32(cand)
        cnt = count(lambda s, j: s >= tc)
        below_ninf = lax.shift_right_logical(cand, 23) == 0
        return jnp.where(below_ninf | (cnt >= ksel), cand, key)

    key = lax.fori_loop(0, 32, bit_body, jnp.zeros((SUBLANES, QB), jnp.int32))
    thr = _key_to_f32(key)
    cnt_ge = count(lambda s, j: s >= thr)
    has_excess = jnp.max(cnt_ge) > ksel

    def write_bias(select):
        def body(j, carry):
            s = s_scr[j].reshape(shape3)
            vis = j * TK + k_in_tile <= q_pos
            bias = jnp.where(vis, jnp.where(select(s, j), 0.0, -jnp.inf), -jnp.inf)
            s_scr[j] = bias.reshape(TK, QB)
            return carry
        lax.fori_loop(0, n_tiles, body, 0)

    @pl.when(jnp.logical_not(has_excess))
    def _():
        write_bias(lambda s, j: s >= thr)

    @pl.when(has_excess)
    def _():
        cnt_gt = count(lambda s, j: s > thr)
        need = ksel - cnt_gt

        def jbit_body(it, jkey):
            cand = jkey | lax.shift_left(jnp.int32(1), 10 - it)
            cnt = count(lambda s, j: (s == thr) & (j * TK + k_in_tile < cand))
            return jnp.where(cnt < need, cand, jkey)

        jkey = lax.fori_loop(0, 11, jbit_body, jnp.zeros((SUBLANES, QB), jnp.int32))
        write_bias(lambda s, j: (s > thr) | ((s == thr) & (j * TK + k_in_tile <= jkey)))

    m_scr[...] = jnp.full(m_scr.shape, NEG_BIG, F32)
    l_scr[...] = jnp.zeros(l_scr.shape, F32)
    acc_scr[...] = jnp.zeros(acc_scr.shape, F32)

    def att_body(j, carry):
        kt = k_ref[0, pl.ds(pl.multiple_of(j * TK, TK), TK), :]
        vt = vt_ref[0, j]
        bias = s_scr[j].reshape(shape3)
        for pr in range(N_HEADS // 2):
            sp_scr[pr] = _dot(kt, jnp.concatenate([q_ref[0, 2 * pr], q_ref[0, 2 * pr + 1]], axis=1))
        for pr in range(N_HEADS // 2):
            for e in range(2):
                hd = 2 * pr + e
                s = sp_scr[pr, :, e * QB:(e + 1) * QB].reshape(shape3) + bias
                m_old = m_scr[hd]
                m_new = jnp.maximum(m_old, all_sublanes(jnp.max(s, axis=0), jnp.maximum))
                alpha = jnp.exp2(m_old - m_new)
                p = jnp.exp2(s - m_new)
                l_scr[hd] = alpha * l_scr[hd] + jnp.sum(p, axis=0)
                pv = _dot(vt, p.reshape(TK, QB).astype(BF16))
                acc = acc_scr[hd].reshape(HEAD_DIM // SUBLANES, SUBLANES, QB) * alpha
                acc_scr[hd] = acc.reshape(HEAD_DIM, QB) + pv
                m_scr[hd] = m_new
        return carry

    lax.fori_loop(0, n_tiles, att_body, 0)

    outs = []
    for hd in range(N_HEADS):
        l = all_sublanes(l_scr[hd], jnp.add)
        o = acc_scr[hd].reshape(HEAD_DIM // SUBLANES, SUBLANES, QB) / l
        outs.append(o.reshape(HEAD_DIM, QB))
    o_ref[0] = jnp.concatenate(outs, axis=0).T.astype(BF16)


def _ffn_kernel(x_ref, ps_ref, ya_ref, g1_ref, wgate_ref, wpa_ref, wo_ref, g2_ref, wfg_ref, wfu_ref, wfd_ref,
                o_ref):
    x = x_ref[0]
    h = _rms(x, g1_ref[...]).astype(BF16)
    gates = _dot(h, wgate_ref[...])
    pa = _dot(ya_ref[0], wpa_ref[...])
    merged = (jax.nn.sigmoid(gates[:, 0:D_MODEL]) * ps_ref[0].astype(F32)
              + jax.nn.sigmoid(gates[:, D_MODEL:2 * D_MODEL]) * pa)
    x1 = x + _dot(merged.astype(BF16), wo_ref[...])
    h2 = _rms(x1, g2_ref[...]).astype(BF16)
    hid = jax.nn.silu(_dot(h2, wfg_ref[...])) * _dot(h2, wfu_ref[...])
    o_ref[0] = x1 + _dot(hid.astype(BF16), wfd_ref[...])


def _const_spec(shape):
    nd = len(shape)
    return pl.BlockSpec(shape, lambda *_: (0,) * nd, pipeline_mode=pl.Buffered(1))


def _block_diag(blocks):
    g, r, c = blocks.shape
    eye = jnp.eye(g, dtype=blocks.dtype)
    return jnp.einsum('grc,gh->grhc', blocks, eye).reshape(g * r, g * c)


def kernel(x, norm1_g, w_in, A_re, A_im, log_dt, B_re, B_im, C_re, C_im, D_skip, w_glu, b_glu, q_norm_g, k_norm_g,
           idx_k_norm_g, w_proj_ssm, w_proj_attn, w_out, norm2_g, w_ffn_gate, w_ffn_up, w_ffn_down):
    bsz, seq, _ = x.shape
    assert x.shape == (8, 2048, D_MODEL) and w_in.shape[0] == 1
    G, N, P = SSM_GROUPS, SSM_STATE, SSM_GROUP
    cparams = functools.partial(pltpu.CompilerParams, vmem_limit_bytes=VMEM_LIMIT)

    w = w_in[0]
    w_u, w_q, w_k, w_v, w_qi, w_ki, w_wi, w_gates = jnp.split(
        w, (512, 1024, 1088, 1152, 1408, 1472, 1476), axis=1)
    w1 = jnp.concatenate([w_u, w_q, w_qi, w_k, w_ki, w_wi, jnp.zeros((D_MODEL, HEAD_DIM - IDX_HEADS), F32), w_v],
                         axis=1).astype(BF16)
    g1 = norm1_g[0].reshape(1, D_MODEL)

    rep = lambda a: jnp.repeat(a, P, axis=0)
    gpn = jax.ShapeDtypeStruct((G * P, N), F32)
    abar_r, abar_i, bp_r, bp_i = pl.pallas_call(
        _disc_kernel, out_shape=(gpn, gpn, gpn, gpn), name="s5_discretise",
    )(rep(A_re[0]), rep(A_im[0]), rep(jnp.broadcast_to(log_dt[0].reshape(G, 1), (G, N))),
      B_re[0].transpose(0, 2, 1).reshape(G * P, N), B_im[0].transpose(0, 2, 1).reshape(G * P, N))
    abar_r, abar_i = abar_r[::P], abar_i[::P]
    bp_r, bp_i = bp_r.reshape(G, P, N), bp_i.reshape(G, P, N)
    hg = G // 2
    bp = [jnp.concatenate([_block_diag(bp_r[sl]), _block_diag(bp_i[sl])], axis=1).astype(BF16)
          for sl in (slice(0, hg), slice(hg, G))]
    c_r = C_re[0].transpose(0, 2, 1)
    c_i = C_im[0].transpose(0, 2, 1)
    cp = [jnp.concatenate([_block_diag(c_r[sl]), -_block_diag(c_i[sl])], axis=0).astype(BF16)
          for sl in (slice(0, hg), slice(hg, G))]
    ar8 = jnp.broadcast_to(abar_r.reshape(1, G * N), (SUBLANES, G * N))
    ai8 = jnp.broadcast_to(abar_i.reshape(1, G * N), (SUBLANES, G * N))

    n_in = seq // TL_IN
    u_t, q_h, qs_h, sg_t, k_n, ki_n, v_t = pl.pallas_call(
        _inproj_kernel,
        grid=(bsz, n_in),
        in_specs=[
            pl.BlockSpec((1, TL_IN, D_MODEL), lambda b, i: (b, i, 0)),
            _const_spec((1, D_MODEL)),
            _const_spec((D_MODEL, W1_COLS)),
            _const_spec((1, ATTN_WIDTH)), _const_spec((1, 2 * HEAD_DIM)), _const_spec((2 * LANES, 2 * LANES)),
        ],
        out_specs=[
            pl.BlockSpec((1, TL_IN, SSM_WIDTH), lambda b, i: (b, i, 0)),
            pl.BlockSpec((1, N_HEADS, HEAD_DIM, TL_IN), lambda b, i: (b, 0, 0, i)),
            pl.BlockSpec((1, IDX_HEADS, IDX_DIM, TL_IN), lambda b, i: (b, 0, 0, i)),
            pl.BlockSpec((1, SUBLANES, TL_IN), lambda b, i: (b, 0, i)),
            pl.BlockSpec((1, TL_IN, HEAD_DIM), lambda b, i: (b, i, 0)),
            pl.BlockSpec((1, TL_IN, IDX_DIM), lambda b, i: (b, i, 0)),
            pl.BlockSpec((1, TL_IN // TK, HEAD_DIM, TK), lambda b, i: (b, i, 0, 0)),
        ],
        out_shape=(
            jax.ShapeDtypeStruct((bsz, seq, SSM_WIDTH), F32),
            jax.ShapeDtypeStruct((bsz, N_HEADS, HEAD_DIM, seq), BF16),
            jax.ShapeDtypeStruct((bsz, IDX_HEADS, IDX_DIM, seq), BF16),
            jax.ShapeDtypeStruct((bsz, SUBLANES, seq), F32),
            jax.ShapeDtypeStruct((bsz, seq, HEAD_DIM), BF16),
            jax.ShapeDtypeStruct((bsz, seq, IDX_DIM), BF16),
            jax.ShapeDtypeStruct((bsz, seq // TK, HEAD_DIM, TK), BF16),
        ),
        compiler_params=cparams(dimension_semantics=("parallel", "parallel")),
        name="in_projection",
    )(x, g1, w1, jnp.tile(q_norm_g[0], N_HEADS).reshape(1, ATTN_WIDTH),
      jnp.concatenate([k_norm_g[0], idx_k_norm_g[0]]).reshape(1, 2 * HEAD_DIM),
      jnp.kron(jnp.eye(2 * LANES // HEAD_DIM, dtype=BF16), jnp.ones((HEAD_DIM, HEAD_DIM), BF16)))

    rows = SUBLANES * TC_SCAN
    ps_t = pl.pallas_call(
        _s5_kernel,
        grid=(seq // TC_SCAN,),
        in_specs=[
            pl.BlockSpec((bsz, TC_SCAN, SSM_WIDTH), lambda c: (0, c, 0)),
            _const_spec((256, 2048)), _const_spec((256, 2048)),
            _const_spec((2048, 256)), _const_spec((2048, 256)),
            _const_spec((SUBLANES, G * N)), _const_spec((SUBLANES, G * N)),
            _const_spec((1, SSM_WIDTH)), _const_spec((SSM_WIDTH, SSM_WIDTH)), _const_spec((1, SSM_WIDTH)),
            _const_spec((SSM_WIDTH, D_MODEL)),
        ],
        out_specs=pl.BlockSpec((bsz, TC_SCAN, D_MODEL), lambda c: (0, c, 0)),
        out_shape=jax.ShapeDtypeStruct((bsz, seq, D_MODEL), BF16),
        scratch_shapes=[pltpu.VMEM((rows, 2 * G * N), F32), pltpu.VMEM((SUBLANES, 2 * G * N), F32)],
        compiler_params=cparams(dimension_semantics=("arbitrary",)),
        name="s5_branch",
    )(u_t, bp[0], bp[1], cp[0], cp[1], ar8, ai8,
      D_skip[0].reshape(1, SSM_WIDTH), w_glu[0].astype(BF16), b_glu[0].reshape(1, SSM_WIDTH),
      w_proj_ssm[0].astype(BF16))

    y_att = pl.pallas_call(
        _attn_kernel,
        grid=(bsz, seq // QB),
        in_specs=[
            pl.BlockSpec((1, IDX_HEADS, IDX_DIM, QB), lambda b, i: (b, 0, 0, i)),
            pl.BlockSpec((1, SUBLANES, QB), lambda b, i: (b, 0, i)),
            pl.BlockSpec((1, N_HEADS, HEAD_DIM, QB), lambda b, i: (b, 0, 0, i)),
            pl.BlockSpec((1, seq, IDX_DIM), lambda b, i: (b, 0, 0)),
            pl.BlockSpec((1, seq, HEAD_DIM), lambda b, i: (b, 0, 0)),
            pl.BlockSpec((1, seq // TK, HEAD_DIM, TK), lambda b, i: (b, 0, 0, 0)),
        ],
        out_specs=pl.BlockSpec((1, QB, ATTN_WIDTH), lambda b, i: (b, i, 0)),
        out_shape=jax.ShapeDtypeStruct((bsz, seq, ATTN_WIDTH), BF16),
        scratch_shapes=[pltpu.VMEM((seq // TK, TK, QB), F32),
                        pltpu.VMEM((N_HEADS, SUBLANES, QB), F32),
                        pltpu.VMEM((N_HEADS, SUBLANES, QB), F32),
                        pltpu.VMEM((N_HEADS, HEAD_DIM, QB), F32),
                        pltpu.VMEM((N_HEADS // 2, TK, 2 * QB), F32)],
        compiler_params=cparams(dimension_semantics=("parallel", "arbitrary")),
        name="sparse_attention",
    )(qs_h, sg_t, q_h, ki_n, k_n, v_t)

    out = pl.pallas_call(
        _ffn_kernel,
        grid=(bsz, seq // TL_FFN),
        in_specs=[
            pl.BlockSpec((1, TL_FFN, D_MODEL), lambda b, i: (b, i, 0)),
            pl.BlockSpec((1, TL_FFN, D_MODEL), lambda b, i: (b, i, 0)),
            pl.BlockSpec((1, TL_FFN, ATTN_WIDTH), lambda b, i: (b, i, 0)),
            _const_spec((1, D_MODEL)),
            _const_spec((D_MODEL, 2 * D_MODEL)),
            _const_spec((ATTN_WIDTH, D_MODEL)),
            _const_spec((D_MODEL, D_MODEL)),
            _const_spec((1, D_MODEL)),
            _const_spec((D_MODEL, D_FF)), _const_spec((D_MODEL, D_FF)), _const_spec((D_FF, D_MODEL)),
        ],
        out_specs=pl.BlockSpec((1, TL_FFN, D_MODEL), lambda b, i: (b, i, 0)),
        out_shape=jax.ShapeDtypeStruct((bsz, seq, D_MODEL), F32),
        compiler_params=cparams(dimension_semantics=("parallel", "parallel")),
        name="merge_ffn",
    )(x, ps_t, y_att, g1, w_gates.astype(BF16), w_proj_attn[0].astype(BF16),
      w_out[0].astype(BF16), norm2_g[0].reshape(1, D_MODEL), w_ffn_gate[0].astype(BF16),
      w_ffn_up[0].astype(BF16), w_ffn_down[0].astype(BF16))
    return out
```

```python
import functools
import math

import jax
import jax.numpy as jnp
from jax import lax
from jax.experimental import pallas as pl
from jax.experimental.pallas import tpu as pltpu

F32 = jnp.float32
BF16 = jnp.bfloat16

D_MODEL = 1024
SSM_WIDTH = 512
SSM_GROUP = 16
SSM_GROUPS = 32
SSM_STATE = 64
N_HEADS = 8
HEAD_DIM = 64
ATTN_WIDTH = 512
IDX_HEADS = 4
IDX_DIM = 64
INDEX_TOPK = 256
D_FF = 2816
RMS_EPS = 1e-6

LANES = 128
SUBLANES = 8
VMEM_LIMIT = 56 * 1024 * 1024

TL_IN = 512
TC_SCAN = 64
QB = 256
TK = 256
TL_FFN = 256

W1_COLS = 1536
NEG_BIG = -1e30
assert QB % LANES == 0


def _dot(a, b):
    return jnp.dot(a, b, preferred_element_type=F32)


def _rms(x, g):
    return x * lax.rsqrt(jnp.mean(x * x, axis=-1, keepdims=True) + RMS_EPS) * g


def _disc_kernel(are_ref, aim_ref, ldt_ref, bre_ref, bim_ref, abr_ref, abi_ref, bpr_ref, bpi_ref):
    ar = are_ref[...]
    ai = aim_ref[...]
    dt = jnp.exp(ldt_ref[...])
    mag = jnp.exp(ar * dt)
    abar_r = mag * jnp.cos(ai * dt)
    abar_i = mag * jnp.sin(ai * dt)
    den = ar * ar + ai * ai
    nr = abar_r - 1.0
    coef_r = (nr * ar + abar_i * ai) / den
    coef_i = (abar_i * ar - nr * ai) / den
    abr_ref[...] = abar_r
    abi_ref[...] = abar_i
    br = bre_ref[...]
    bi = bim_ref[...]
    bpr_ref[...] = coef_r * br - coef_i * bi
    bpi_ref[...] = coef_r * bi + coef_i * br


def _inproj_kernel(x_ref, g1_ref, w1_ref, qg_ref, kg_ref, seg_ref,
                   u_ref, q_ref, qs_ref, sg_ref, k_ref, ki_ref, v_ref):
    x = x_ref[0]
    h = _rms(x, g1_ref[...]).astype(BF16)
    proj = _dot(h, w1_ref[...])
    u_ref[0] = proj[:, 0:512]
    seg = seg_ref[...]

    def head_rms(v, g):
        sq = (v * v).astype(BF16)
        n = v.shape[1]
        w = min(n, seg.shape[0])
        ss = jnp.concatenate([_dot(sq[:, c:c + w], seg[0:w, 0:w]) for c in range(0, n, w)], axis=1)
        return v * lax.rsqrt(ss * (1.0 / HEAD_DIM) + RMS_EPS) * g

    qscale = (HEAD_DIM ** -0.5) * math.log2(math.e)
    qn = head_rms(proj[:, 512:1024], qg_ref[...]) * qscale

    def store_heads_transposed(ref, v):
        for g in range(v.shape[1] // LANES):
            tg = v[:, g * LANES:(g + 1) * LANES].T
            ref[0, 2 * g] = tg[0:HEAD_DIM].astype(BF16)
            ref[0, 2 * g + 1] = tg[HEAD_DIM:2 * HEAD_DIM].astype(BF16)

    store_heads_transposed(q_ref, qn)
    w_scale = (IDX_HEADS ** -0.5) * (IDX_DIM ** -0.5)
    wabs = jnp.abs(proj[:, 1408:1536]) * w_scale
    sshape = (2 * HEAD_DIM, IDX_HEADS * IDX_DIM)
    row = lax.broadcasted_iota(jnp.int32, sshape, 0)
    col = lax.broadcasted_iota(jnp.int32, sshape, 1)
    spread = jnp.where(row == col // IDX_DIM, 1.0, 0.0).astype(BF16)
    whi = wabs.astype(BF16)
    wlo = (wabs - whi.astype(F32)).astype(BF16)
    wrep = _dot(whi, spread) + _dot(wlo, spread)
    store_heads_transposed(qs_ref, proj[:, 1024:1280] * wrep)
    kk = head_rms(proj[:, 1280:1408], kg_ref[...]).astype(BF16)
    k_ref[0] = kk[:, 0:HEAD_DIM]
    ki_ref[0] = kk[:, HEAD_DIM:2 * HEAD_DIM]
    t = proj[:, 1408:1536].T
    sg_ref[0] = jnp.where(t[0:SUBLANES] >= 0, 1.0, -1.0)
    for c in range(TL_IN // TK):
        v_ref[0, c] = t[HEAD_DIM:2 * HEAD_DIM, c * TK:(c + 1) * TK].astype(BF16)


def _s5_kernel(u_ref, bplo_ref, bphi_ref, cplo_ref, cphi_ref, ar_ref, ai_ref, d_ref, wglu_ref, bglu_ref,
               wps_ref, o_ref, bu_scr, st_scr):
    @pl.when(pl.program_id(0) == 0)
    def _():
        st_scr[...] = jnp.zeros_like(st_scr)

    u = jnp.transpose(u_ref[...], (1, 0, 2)).reshape(SUBLANES * TC_SCAN, SSM_WIDTH)
    ub = u.astype(BF16)
    bu_scr[:, 0:2048] = _dot(ub[:, 0:256], bplo_ref[...])
    bu_scr[:, 2048:4096] = _dot(ub[:, 256:512], bphi_ref[...])
    ar = ar_ref[...]
    ai = ai_ref[...]

    def step(t, s):
        r0 = pl.multiple_of(t * SUBLANES, SUBLANES)
        b = bu_scr[pl.ds(r0, SUBLANES), :]
        parts = []
        for half in range(2):
            o = half * 2048
            sr, si = s[:, o:o + 1024], s[:, o + 1024:o + 2048]
            br, bi = b[:, o:o + 1024], b[:, o + 1024:o + 2048]
            a_r, a_i = ar[:, half * 1024:(half + 1) * 1024], ai[:, half * 1024:(half + 1) * 1024]
            parts.append(a_r * sr - a_i * si + br)
            parts.append(a_r * si + a_i * sr + bi)
        s2 = jnp.concatenate(parts, axis=1)
        bu_scr[pl.ds(r0, SUBLANES), :] = s2
        return s2

    s_fin = lax.fori_loop(0, TC_SCAN, step, st_scr[...])
    st_scr[...] = s_fin
    sb = bu_scr[...].astype(BF16)
    y = jnp.concatenate([_dot(sb[:, 0:2048], cplo_ref[...]), _dot(sb[:, 2048:4096], cphi_ref[...])], axis=1)
    y = jax.nn.gelu(y + d_ref[...] * u)
    z = _dot(y.astype(BF16), wglu_ref[...]) + bglu_ref[...]
    y = y * jax.nn.sigmoid(z)
    ps = _dot(y.astype(BF16), wps_ref[...])
    o_ref[...] = jnp.transpose(ps.reshape(TC_SCAN, SUBLANES, D_MODEL), (1, 0, 2)).astype(BF16)


def _key_to_f32(key):
    bits = jnp.where(key < 0, key & jnp.int32(0x7FFFFFFF), ~key)
    return pltpu.bitcast(bits, F32)


def _attn_kernel(qs_ref, sg_ref, q_ref, ki_ref, k_ref, vt_ref, o_ref, s_scr, m_scr, l_scr, acc_scr, sp_scr):
    i = pl.program_id(1)
    n_tiles = (i * QB + QB + TK - 1) // TK
    nv = TK // SUBLANES
    shape3 = (nv, SUBLANES, QB)
    k_in_tile = (lax.broadcasted_iota(jnp.int32, shape3, 0) * SUBLANES
                 + lax.broadcasted_iota(jnp.int32, shape3, 1))
    q_pos = i * QB + lax.broadcasted_iota(jnp.int32, shape3, 2)
    ksel = float(INDEX_TOPK)

    def all_sublanes(a, op):
        for sh in (4, 2, 1):
            a = op(a, pltpu.roll(a, sh, 0))
        return a

    sg = sg_ref[0]

    def idx_body(j, carry):
        kt = ki_ref[0, pl.ds(pl.multiple_of(j * TK, TK), TK), :]
        acc = jnp.zeros((TK, QB), F32)
        for pr in range(IDX_HEADS // 2):
            x = _dot(kt, jnp.concatenate([qs_ref[0, 2 * pr], qs_ref[0, 2 * pr + 1]], axis=1))
            for e in range(2):
                hd = 2 * pr + e
                acc = acc + sg[hd:hd + 1, :] * jnp.maximum(x[:, e * QB:(e + 1) * QB], 0.0)
        vis = j * TK + k_in_tile <= q_pos
        s_scr[j] = jnp.where(vis, acc.reshape(shape3), -jnp.inf).reshape(TK, QB)
        return carry

    lax.fori_loop(0, n_tiles, idx_body, 0)

    def count(pred):
        def body(j, acc):
            hit = jnp.where(pred(s_scr[j].reshape(shape3), j), 1.0, 0.0)
            return acc + jnp.sum(hit.reshape(nv // 4, 4, SUBLANES, QB), axis=0)
        acc = lax.fori_loop(0, n_tiles, body, jnp.zeros((4, SUBLANES, QB), F32))
        return all_sublanes(jnp.sum(acc, axis=0), jnp.add)

    def bit_body(it, key):
        cand = key | lax.shift_left(jnp.int32(1), 31 - it)
        tc = _key_to_f32(cand)
        cnt = count(lambda s, j: s >= tc)
        below_ninf = lax.shift_right_logical(cand, 23) == 0
        return jnp.where(below_ninf | (cnt >= ksel), cand, key)

    key = lax.fori_loop(0, 32, bit_body, jnp.zeros((SUBLANES, QB), jnp.int32))
    thr = _key_to_f32(key)
    cnt_ge = count(lambda s, j: s >= thr)
    has_excess = jnp.max(cnt_ge) > ksel

    def write_bias(select):
        def body(j, carry):
            s = s_scr[j].reshape(shape3)
            vis = j * TK + k_in_tile <= q_pos
            bias = jnp.where(vis, jnp.where(select(s, j), 0.0, -jnp.inf), -jnp.inf)
            s_scr[j] = bias.reshape(TK, QB)
            return carry
        lax.fori_loop(0, n_tiles, body, 0)

    @pl.when(jnp.logical_not(has_excess))
    def _():
        write_bias(lambda s, j: s >= thr)

    @pl.when(has_excess)
    def _():
        cnt_gt = count(lambda s, j: s > thr)
        need = ksel - cnt_gt

        def jbit_body(it, jkey):
            cand = jkey | lax.shift_left(jnp.int32(1), 10 - it)
            cnt = count(lambda s, j: (s == thr) & (j * TK + k_in_tile < cand))
            return jnp.where(cnt < need, cand, jkey)

        jkey = lax.fori_loop(0, 11, jbit_body, jnp.zeros((SUBLANES, QB), jnp.int32))
        write_bias(lambda s, j: (s > thr) | ((s == thr) & (j * TK + k_in_tile <= jkey)))

    m_scr[...] = jnp.full(m_scr.shape, NEG_BIG, F32)
    l_scr[...] = jnp.zeros(l_scr.shape, F32)
    acc_scr[...] = jnp.zeros(acc_scr.shape, F32)

    def att_body(j, carry):
        kt = k_ref[0, pl.ds(pl.multiple_of(j * TK, TK), TK), :]
        vt = vt_ref[0, j]
        bias = s_scr[j].reshape(shape3)
        for pr in range(N_HEADS // 2):
            sp_scr[pr] = _dot(kt, jnp.concatenate([q_ref[0, 2 * pr], q_ref[0, 2 * pr + 1]], axis=1))
        for pr in range(N_HEADS // 2):
            for e in range(2):
                hd = 2 * pr + e
                s = sp_scr[pr, :, e * QB:(e + 1) * QB].reshape(shape3) + bias
                m_old = m_scr[hd]
                m_new = jnp.maximum(m_old, all_sublanes(jnp.max(s, axis=0), jnp.maximum))
                alpha = jnp.exp2(m_old - m_new)
                p = jnp.exp2(s - m_new)
                l_scr[hd] = alpha * l_scr[hd] + jnp.sum(p, axis=0)
                pv = _dot(vt, p.reshape(TK, QB).astype(BF16))
                acc = acc_scr[hd].reshape(HEAD_DIM // SUBLANES, SUBLANES, QB) * alpha
                acc_scr[hd] = acc.reshape(HEAD_DIM, QB) + pv
                m_scr[hd] = m_new
        return carry

    lax.fori_loop(0, n_tiles, att_body, 0)

    outs = []
    for hd in range(N_HEADS):
        l = all_sublanes(l_scr[hd], jnp.add)
        o = acc_scr[hd].reshape(HEAD_DIM // SUBLANES, SUBLANES, QB) / l
        outs.append(o.reshape(HEAD_DIM, QB))
    o_ref[0] = jnp.concatenate(outs, axis=0).T.astype(BF16)


def _ffn_kernel(x_ref, ps_ref, ya_ref, g1_ref, wgate_ref, wpa_ref, wo_ref, g2_ref, wfg_ref, wfu_ref, wfd_ref,
                o_ref):
    x = x_ref[0]
    h = _rms(x, g1_ref[...]).astype(BF16)
    gates = _dot(h, wgate_ref[...])
    pa = _dot(ya_ref[0], wpa_ref[...])
    merged = (jax.nn.sigmoid(gates[:, 0:D_MODEL]) * ps_ref[0].astype(F32)
              + jax.nn.sigmoid(gates[:, D_MODEL:2 * D_MODEL]) * pa)
    x1 = x + _dot(merged.astype(BF16), wo_ref[...])
    h2 = _rms(x1, g2_ref[...]).astype(BF16)
    hid = jax.nn.silu(_dot(h2, wfg_ref[...])) * _dot(h2, wfu_ref[...])
    o_ref[0] = x1 + _dot(hid.astype(BF16), wfd_ref[...])


def _const_spec(shape):
    nd = len(shape)
    return pl.BlockSpec(shape, lambda *_: (0,) * nd, pipeline_mode=pl.Buffered(1))


def _block_diag(blocks):
    g, r, c = blocks.shape
    eye = jnp.eye(g, dtype=blocks.dtype)
    return jnp.einsum('grc,gh->grhc', blocks, eye).reshape(g * r, g * c)


def kernel(x, norm1_g, w_in, A_re, A_im, log_dt, B_re, B_im, C_re, C_im, D_skip, w_glu, b_glu, q_norm_g, k_norm_g,
           idx_k_norm_g, w_proj_ssm, w_proj_attn, w_out, norm2_g, w_ffn_gate, w_ffn_up, w_ffn_down):
    bsz, seq, _ = x.shape
    assert x.shape == (8, 2048, D_MODEL) and w_in.shape[0] == 1
    G, N, P = SSM_GROUPS, SSM_STATE, SSM_GROUP
    cparams = functools.partial(pltpu.CompilerParams, vmem_limit_bytes=VMEM_LIMIT)

    w = w_in[0]
    w_u, w_q, w_k, w_v, w_qi, w_ki, w_wi, w_gates = jnp.split(
        w, (512, 1024, 1088, 1152, 1408, 1472, 1476), axis=1)
    w1 = jnp.concatenate([w_u, w_q, w_qi, w_k, w_ki, w_wi, jnp.zeros((D_MODEL, HEAD_DIM - IDX_HEADS), F32), w_v],
                         axis=1).astype(BF16)
    g1 = norm1_g[0].reshape(1, D_MODEL)

    rep = lambda a: jnp.repeat(a, P, axis=0)
    gpn = jax.ShapeDtypeStruct((G * P, N), F32)
    abar_r, abar_i, bp_r, bp_i = pl.pallas_call(
        _disc_kernel, out_shape=(gpn, gpn, gpn, gpn), name="s5_discretise",
    )(rep(A_re[0]), rep(A_im[0]), rep(jnp.broadcast_to(log_dt[0].reshape(G, 1), (G, N))),
      B_re[0].transpose(0, 2, 1).reshape(G * P, N), B_im[0].transpose(0, 2, 1).reshape(G * P, N))
    abar_r, abar_i = abar_r[::P], abar_i[::P]
    bp_r, bp_i = bp_r.reshape(G, P, N), bp_i.reshape(G, P, N)
    hg = G // 2
    bp = [jnp.concatenate([_block_diag(bp_r[sl]), _block_diag(bp_i[sl])], axis=1).astype(BF16)
          for sl in (slice(0, hg), slice(hg, G))]
    c_r = C_re[0].transpose(0, 2, 1)
    c_i = C_im[0].transpose(0, 2, 1)
    cp = [jnp.concatenate([_block_diag(c_r[sl]), -_block_diag(c_i[sl])], axis=0).astype(BF16)
          for sl in (slice(0, hg), slice(hg, G))]
    ar8 = jnp.broadcast_to(abar_r.reshape(1, G * N), (SUBLANES, G * N))
    ai8 = jnp.broadcast_to(abar_i.reshape(1, G * N), (SUBLANES, G * N))

    n_in = seq // TL_IN
    u_t, q_h, qs_h, sg_t, k_n, ki_n, v_t = pl.pallas_call(
        _inproj_kernel,
        grid=(bsz, n_in),
        in_specs=[
            pl.BlockSpec((1, TL_IN, D_MODEL), lambda b, i: (b, i, 0)),
            _const_spec((1, D_MODEL)),
            _const_spec((D_MODEL, W1_COLS)),
            _const_spec((1, ATTN_WIDTH)), _const_spec((1, 2 * HEAD_DIM)), _const_spec((2 * LANES, 2 * LANES)),
        ],
        out_specs=[
            pl.BlockSpec((1, TL_IN, SSM_WIDTH), lambda b, i: (b, i, 0)),
            pl.BlockSpec((1, N_HEADS, HEAD_DIM, TL_IN), lambda b, i: (b, 0, 0, i)),
            pl.BlockSpec((1, IDX_HEADS, IDX_DIM, TL_IN), lambda b, i: (b, 0, 0, i)),
            pl.BlockSpec((1, SUBLANES, TL_IN), lambda b, i: (b, 0, i)),
            pl.BlockSpec((1, TL_IN, HEAD_DIM), lambda b, i: (b, i, 0)),
            pl.BlockSpec((1, TL_IN, IDX_DIM), lambda b, i: (b, i, 0)),
            pl.BlockSpec((1, TL_IN // TK, HEAD_DIM, TK), lambda b, i: (b, i, 0, 0)),
        ],
        out_shape=(
            jax.ShapeDtypeStruct((bsz, seq, SSM_WIDTH), F32),
            jax.ShapeDtypeStruct((bsz, N_HEADS, HEAD_DIM, seq), BF16),
            jax.ShapeDtypeStruct((bsz, IDX_HEADS, IDX_DIM, seq), BF16),
            jax.ShapeDtypeStruct((bsz, SUBLANES, seq), F32),
            jax.ShapeDtypeStruct((bsz, seq, HEAD_DIM), BF16),
            jax.ShapeDtypeStruct((bsz, seq, IDX_DIM), BF16),
            jax.ShapeDtypeStruct((bsz, seq // TK, HEAD_DIM, TK), BF16),
        ),
        compiler_params=cparams(dimension_semantics=("parallel", "parallel")),
        name="in_projection",
    )(x, g1, w1, jnp.tile(q_norm_g[0], N_HEADS).reshape(1, ATTN_WIDTH),
      jnp.concatenate([k_norm_g[0], idx_k_norm_g[0]]).reshape(1, 2 * HEAD_DIM),
      jnp.kron(jnp.eye(2 * LANES // HEAD_DIM, dtype=BF16), jnp.ones((HEAD_DIM, HEAD_DIM), BF16)))

    rows = SUBLANES * TC_SCAN
    ps_t = pl.pallas_call(
        _s5_kernel,
        grid=(seq // TC_SCAN,),
        in_specs=[
            pl.BlockSpec((bsz, TC_SCAN, SSM_WIDTH), lambda c: (0, c, 0)),
            _const_spec((256, 2048)), _const_spec((256, 2048)),
            _const_spec((2048, 256)), _const_spec((2048, 256)),
            _const_spec((SUBLANES, G * N)), _const_spec((SUBLANES, G * N)),
            _const_spec((1, SSM_WIDTH)), _const_spec((SSM_WIDTH, SSM_WIDTH)), _const_spec((1, SSM_WIDTH)),
            _const_spec((SSM_WIDTH, D_MODEL)),
        ],
        out_specs=pl.BlockSpec((bsz, TC_SCAN, D_MODEL), lambda c: (0, c, 0)),
        out_shape=jax.ShapeDtypeStruct((bsz, seq, D_MODEL), BF16),
        scratch_shapes=[pltpu.VMEM((rows, 2 * G * N), F32), pltpu.VMEM((SUBLANES, 2 * G * N), F32)],
        compiler_params=cparams(dimension_semantics=("arbitrary",)),
        name="s5_branch",
    )(u_t, bp[0], bp[1], cp[0], cp[1], ar8, ai8,
      D_skip[0].reshape(1, SSM_WIDTH), w_glu[0].astype(BF16), b_glu[0].reshape(1, SSM_WIDTH),
      w_proj_ssm[0].astype(BF16))

    y_att = pl.pallas_call(
        _attn_kernel,
        grid=(bsz, seq // QB),
        in_specs=[
            pl.BlockSpec((1, IDX_HEADS, IDX_DIM, QB), lambda b, i: (b, 0, 0, i)),
            pl.BlockSpec((1, SUBLANES, QB), lambda b, i: (b, 0, i)),
            pl.BlockSpec((1, N_HEADS, HEAD_DIM, QB), lambda b, i: (b, 0, 0, i)),
            pl.BlockSpec((1, seq, IDX_DIM), lambda b, i: (b, 0, 0)),
            pl.BlockSpec((1, seq, HEAD_DIM), lambda b, i: (b, 0, 0)),
            pl.BlockSpec((1, seq // TK, HEAD_DIM, TK), lambda b, i: (b, 0, 0, 0)),
        ],
        out_specs=pl.BlockSpec((1, QB, ATTN_WIDTH), lambda b, i: (b, i, 0)),
        out_shape=jax.ShapeDtypeStruct((bsz, seq, ATTN_WIDTH), BF16),
        scratch_shapes=[pltpu.VMEM((seq // TK, TK, QB), F32),
                        pltpu.VMEM((N_HEADS, SUBLANES, QB), F32),
                        pltpu.VMEM((N_HEADS, SUBLANES, QB), F32),
                        pltpu.VMEM((N_HEADS, HEAD_DIM, QB), F32),
                        pltpu.VMEM((N_HEADS // 2, TK, 2 * QB), F32)],
        compiler_params=cparams(dimension_semantics=("parallel", "arbitrary")),
        name="sparse_attention",
    )(qs_h, sg_t, q_h, ki_n, k_n, v_t)

    out = pl.pallas_call(
        _ffn_kernel,
        grid=(bsz, seq // TL_FFN),
        in_specs=[
            pl.BlockSpec((1, TL_FFN, D_MODEL), lambda b, i: (b, i, 0)),
            pl.BlockSpec((1, TL_FFN, D_MODEL), lambda b, i: (b, i, 0)),
            pl.BlockSpec((1, TL_FFN, ATTN_WIDTH), lambda b, i: (b, i, 0)),
            _const_spec((1, D_MODEL)),
            _const_spec((D_MODEL, 2 * D_MODEL)),
            _const_spec((ATTN_WIDTH, D_MODEL)),
            _const_spec((D_MODEL, D_MODEL)),
            _const_spec((1, D_MODEL)),
            _const_spec((D_MODEL, D_FF)), _const_spec((D_MODEL, D_FF)), _const_spec((D_FF, D_MODEL)),
        ],
        out_specs=pl.BlockSpec((1, TL_FFN, D_MODEL), lambda b, i: (b, i, 0)),
        out_shape=jax.ShapeDtypeStruct((bsz, seq, D_MODEL), F32),
        compiler_params=cparams(dimension_semantics=("parallel", "parallel")),
        name="merge_ffn",
    )(x, ps_t, y_att, g1, w_gates.astype(BF16), w_proj_attn[0].astype(BF16),
      w_out[0].astype(BF16), norm2_g[0].reshape(1, D_MODEL), w_ffn_gate[0].astype(BF16),
      w_ffn_up[0].astype(BF16), w_ffn_down[0].astype(BF16))
    return out
```

```python
import functools
import math

import jax
import jax.numpy as jnp
from jax import lax
from jax.experimental import pallas as pl
from jax.experimental.pallas import tpu as pltpu

F32 = jnp.float32
BF16 = jnp.bfloat16

D_MODEL = 1024
SSM_WIDTH = 512
SSM_GROUP = 16
SSM_GROUPS = 32
SSM_STATE = 64
N_HEADS = 8
HEAD_DIM = 64
ATTN_WIDTH = 512
IDX_HEADS = 4
IDX_DIM = 64
INDEX_TOPK = 256
D_FF = 2816
RMS_EPS = 1e-6

LANES = 128
SUBLANES = 8
VMEM_LIMIT = 56 * 1024 * 1024

TL_IN = 512
TC_SCAN = 64
QB = 256
TK = 256
TL_FFN = 256

W1_COLS = 1536
NEG_BIG = -1e30
assert QB % LANES == 0


def _dot(a, b):
    return jnp.dot(a, b, preferred_element_type=F32)


def _rms(x, g):
    return x * lax.rsqrt(jnp.mean(x * x, axis=-1, keepdims=True) + RMS_EPS) * g


def _disc_kernel(are_ref, aim_ref, ldt_ref, bre_ref, bim_ref, abr_ref, abi_ref, bpr_ref, bpi_ref):
    ar = are_ref[...]
    ai = aim_ref[...]
    dt = jnp.exp(ldt_ref[...])
    mag = jnp.exp(ar * dt)
    abar_r = mag * jnp.cos(ai * dt)
    abar_i = mag * jnp.sin(ai * dt)
    den = ar * ar + ai * ai
    nr = abar_r - 1.0
    coef_r = (nr * ar + abar_i * ai) / den
    coef_i = (abar_i * ar - nr * ai) / den
    abr_ref[...] = abar_r
    abi_ref[...] = abar_i
    br = bre_ref[...]
    bi = bim_ref[...]
    bpr_ref[...] = coef_r * br - coef_i * bi
    bpi_ref[...] = coef_r * bi + coef_i * br


def _inproj_kernel(x_ref, g1_ref, w1_ref, qg_ref, kg_ref, seg_ref,
                   u_ref, q_ref, qs_ref, sg_ref, k_ref, ki_ref, v_ref):
    x = x_ref[0]
    h = _rms(x, g1_ref[...]).astype(BF16)
    proj = _dot(h, w1_ref[...])
    u_ref[0] = proj[:, 0:512]
    seg = seg_ref[...]

    def head_rms(v, g):
        sq = (v * v).astype(BF16)
        n = v.shape[1]
        w = min(n, seg.shape[0])
        ss = jnp.concatenate([_dot(sq[:, c:c + w], seg[0:w, 0:w]) for c in range(0, n, w)], axis=1)
        return v * lax.rsqrt(ss * (1.0 / HEAD_DIM) + RMS_EPS) * g

    qscale = (HEAD_DIM ** -0.5) * math.log2(math.e)
    qn = head_rms(proj[:, 512:1024], qg_ref[...]) * qscale

    def store_heads_transposed(ref, v):
        for g in range(v.shape[1] // LANES):
            tg = v[:, g * LANES:(g + 1) * LANES].T
            ref[0, 2 * g] = tg[0:HEAD_DIM].astype(BF16)
            ref[0, 2 * g + 1] = tg[HEAD_DIM:2 * HEAD_DIM].astype(BF16)

    store_heads_transposed(q_ref, qn)
    w_scale = (IDX_HEADS ** -0.5) * (IDX_DIM ** -0.5)
    wabs = jnp.abs(proj[:, 1408:1536]) * w_scale
    sshape = (2 * HEAD_DIM, IDX_HEADS * IDX_DIM)
    row = lax.broadcasted_iota(jnp.int32, sshape, 0)
    col = lax.broadcasted_iota(jnp.int32, sshape, 1)
    spread = jnp.where(row == col // IDX_DIM, 1.0, 0.0).astype(BF16)
    whi = wabs.astype(BF16)
    wlo = (wabs - whi.astype(F32)).astype(BF16)
    wrep = _dot(whi, spread) + _dot(wlo, spread)
    store_heads_transposed(qs_ref, proj[:, 1024:1280] * wrep)
    kk = head_rms(proj[:, 1280:1408], kg_ref[...]).astype(BF16)
    k_ref[0] = kk[:, 0:HEAD_DIM]
    ki_ref[0] = kk[:, HEAD_DIM:2 * HEAD_DIM]
    t = proj[:, 1408:1536].T
    sg_ref[0] = jnp.where(t[0:SUBLANES] >= 0, 1.0, -1.0)
    for c in range(TL_IN // TK):
        v_ref[0, c] = t[HEAD_DIM:2 * HEAD_DIM, c * TK:(c + 1) * TK].astype(BF16)


def _s5_kernel(u_ref, bplo_ref, bphi_ref, cplo_ref, cphi_ref, ar_ref, ai_ref, d_ref, wglu_ref, bglu_ref,
               wps_ref, o_ref, bu_scr, st_scr):
    @pl.when(pl.program_id(0) == 0)
    def _():
        st_scr[...] = jnp.zeros_like(st_scr)

    u = jnp.transpose(u_ref[...], (1, 0, 2)).reshape(SUBLANES * TC_SCAN, SSM_WIDTH)
    ub = u.astype(BF16)
    half_w = 2 * (SSM_GROUPS // 2) * SSM_STATE
    n_re = half_w // 2

    def scan_half(h):
        c0 = h * half_w
        a_r = ar_ref[:, h * n_re:(h + 1) * n_re]
        a_i = ai_ref[:, h * n_re:(h + 1) * n_re]
        sr = st_scr[:, c0:c0 + n_re]
        si = st_scr[:, c0 + n_re:c0 + half_w]
        for t in range(TC_SCAN):
            rows = slice(t * SUBLANES, (t + 1) * SUBLANES)
            br = bu_scr[rows, c0:c0 + n_re]
            bi = bu_scr[rows, c0 + n_re:c0 + half_w]
            sr, si = a_r * sr - a_i * si + br, a_r * si + a_i * sr + bi
            bu_scr[rows, c0:c0 + n_re] = sr
            bu_scr[rows, c0 + n_re:c0 + half_w] = si
        st_scr[:, c0:c0 + n_re] = sr
        st_scr[:, c0 + n_re:c0 + half_w] = si

    bu_scr[:, 0:half_w] = _dot(ub[:, 0:256], bplo_ref[...])
    bu_scr[:, half_w:2 * half_w] = _dot(ub[:, 256:512], bphi_ref[...])
    scan_half(0)
    y_lo = _dot(bu_scr[:, 0:half_w].astype(BF16), cplo_ref[...])
    scan_half(1)
    y_hi = _dot(bu_scr[:, half_w:2 * half_w].astype(BF16), cphi_ref[...])
    y = jnp.concatenate([y_lo, y_hi], axis=1)
    y = jax.nn.gelu(y + d_ref[...] * u)
    z = _dot(y.astype(BF16), wglu_ref[...]) + bglu_ref[...]
    y = y * jax.nn.sigmoid(z)
    ps = _dot(y.astype(BF16), wps_ref[...])
    o_ref[...] = jnp.transpose(ps.reshape(TC_SCAN, SUBLANES, D_MODEL), (1, 0, 2)).astype(BF16)


def _key_to_f32(key):
    bits = jnp.where(key < 0, key & jnp.int32(0x7FFFFFFF), ~key)
    return pltpu.bitcast(bits, F32)


def _attn_kernel(qs_ref, sg_ref, q_ref, ki_ref, k_ref, vt_ref, o_ref, s_scr, m_scr, l_scr, acc_scr, sp_scr):
    i = pl.program_id(1)
    n_tiles = (i * QB + QB + TK - 1) // TK
    nv = TK // SUBLANES
    shape3 = (nv, SUBLANES, QB)
    k_in_tile = (lax.broadcasted_iota(jnp.int32, shape3, 0) * SUBLANES
                 + lax.broadcasted_iota(jnp.int32, shape3, 1))
    q_pos = i * QB + lax.broadcasted_iota(jnp.int32, shape3, 2)
    ksel = float(INDEX_TOPK)

    def all_sublanes(a, op):
        for sh in (4, 2, 1):
            a = op(a, pltpu.roll(a, sh, 0))
        return a

    sg = sg_ref[0]

    def idx_body(j, carry):
        kt = ki_ref[0, pl.ds(pl.multiple_of(j * TK, TK), TK), :]
        acc = jnp.zeros((TK, QB), F32)
        for pr in range(IDX_HEADS // 2):
            x = _dot(kt, jnp.concatenate([qs_ref[0, 2 * pr], qs_ref[0, 2 * pr + 1]], axis=1))
            for e in range(2):
                hd = 2 * pr + e
                acc = acc + sg[hd:hd + 1, :] * jnp.maximum(x[:, e * QB:(e + 1) * QB], 0.0)
        vis = j * TK + k_in_tile <= q_pos
        s_scr[j] = jnp.where(vis, acc.reshape(shape3), -jnp.inf).reshape(TK, QB)
        return carry

    lax.fori_loop(0, n_tiles, idx_body, 0)

    def count(pred):
        def body(j, acc):
            hit = jnp.where(pred(s_scr[j].reshape(shape3), j), 1.0, 0.0)
            return acc + jnp.sum(hit.reshape(nv // 4, 4, SUBLANES, QB), axis=0)
        acc = lax.fori_loop(0, n_tiles, body, jnp.zeros((4, SUBLANES, QB), F32))
        return all_sublanes(jnp.sum(acc, axis=0), jnp.add)

    def bit_body(it, key):
        cand = key | lax.shift_left(jnp.int32(1), 31 - it)
        tc = _key_to_f32(cand)
        cnt = count(lambda s, j: s >= tc)
        below_ninf = lax.shift_right_logical(cand, 23) == 0
        return jnp.where(below_ninf | (cnt >= ksel), cand, key)

    key = lax.fori_loop(0, 32, bit_body, jnp.zeros((SUBLANES, QB), jnp.int32))
    thr = _key_to_f32(key)
    cnt_ge = count(lambda s, j: s >= thr)
    has_excess = jnp.max(cnt_ge) > ksel

    def write_bias(select):
        def body(j, carry):
            s = s_scr[j].reshape(shape3)
            vis = j * TK + k_in_tile <= q_pos
            bias = jnp.where(vis, jnp.where(select(s, j), 0.0, -jnp.inf), -jnp.inf)
            s_scr[j] = bias.reshape(TK, QB)
            return carry
        lax.fori_loop(0, n_tiles, body, 0)

    @pl.when(jnp.logical_not(has_excess))
    def _():
        write_bias(lambda s, j: s >= thr)

    @pl.when(has_excess)
    def _():
        cnt_gt = count(lambda s, j: s > thr)
        need = ksel - cnt_gt

        def jbit_body(it, jkey):
            cand = jkey | lax.shift_left(jnp.int32(1), 10 - it)
            cnt = count(lambda s, j: (s == thr) & (j * TK + k_in_tile < cand))
            return jnp.where(cnt < need, cand, jkey)

        jkey = lax.fori_loop(0, 11, jbit_body, jnp.zeros((SUBLANES, QB), jnp.int32))
        write_bias(lambda s, j: (s > thr) | ((s == thr) & (j * TK + k_in_tile <= jkey)))

    m_scr[...] = jnp.full(m_scr.shape, NEG_BIG, F32)
    l_scr[...] = jnp.zeros(l_scr.shape, F32)
    acc_scr[...] = jnp.zeros(acc_scr.shape, F32)

    def att_body(j, carry):
        kt = k_ref[0, pl.ds(pl.multiple_of(j * TK, TK), TK), :]
        vt = vt_ref[0, j]
        bias = s_scr[j].reshape(shape3)
        for pr in range(N_HEADS // 2):
            sp_scr[pr] = _dot(kt, jnp.concatenate([q_ref[0, 2 * pr], q_ref[0, 2 * pr + 1]], axis=1))
        for pr in range(N_HEADS // 2):
            for e in range(2):
                hd = 2 * pr + e
                s = sp_scr[pr, :, e * QB:(e + 1) * QB].reshape(shape3) + bias
                m_old = m_scr[hd]
                m_new = jnp.maximum(m_old, all_sublanes(jnp.max(s, axis=0), jnp.maximum))
                alpha = jnp.exp2(m_old - m_new)
                p = jnp.exp2(s - m_new)
                l_scr[hd] = alpha * l_scr[hd] + jnp.sum(p, axis=0)
                pv = _dot(vt, p.reshape(TK, QB).astype(BF16))
                acc = acc_scr[hd].reshape(HEAD_DIM // SUBLANES, SUBLANES, QB) * alpha
                acc_scr[hd] = acc.reshape(HEAD_DIM, QB) + pv
                m_scr[hd] = m_new
        return carry

    lax.fori_loop(0, n_tiles, att_body, 0)

    outs = []
    for hd in range(N_HEADS):
        l = all_sublanes(l_scr[hd], jnp.add)
        o = acc_scr[hd].reshape(HEAD_DIM // SUBLANES, SUBLANES, QB) / l
        outs.append(o.reshape(HEAD_DIM, QB))
    o_ref[0] = jnp.concatenate(outs, axis=0).T.astype(BF16)


def _ffn_kernel(x_ref, ps_ref, ya_ref, g1_ref, wgate_ref, wpa_ref, wo_ref, g2_ref, wfg_ref, wfu_ref, wfd_ref,
                o_ref):
    x = x_ref[0]
    h = _rms(x, g1_ref[...]).astype(BF16)
    gates = _dot(h, wgate_ref[...])
    pa = _dot(ya_ref[0], wpa_ref[...])
    merged = (jax.nn.sigmoid(gates[:, 0:D_MODEL]) * ps_ref[0].astype(F32)
              + jax.nn.sigmoid(gates[:, D_MODEL:2 * D_MODEL]) * pa)
    x1 = x + _dot(merged.astype(BF16), wo_ref[...])
    h2 = _rms(x1, g2_ref[...]).astype(BF16)
    hid = jax.nn.silu(_dot(h2, wfg_ref[...])) * _dot(h2, wfu_ref[...])
    o_ref[0] = x1 + _dot(hid.astype(BF16), wfd_ref[...])


def _const_spec(shape):
    nd = len(shape)
    return pl.BlockSpec(shape, lambda *_: (0,) * nd, pipeline_mode=pl.Buffered(1))


def _block_diag(blocks):
    g, r, c = blocks.shape
    eye = jnp.eye(g, dtype=blocks.dtype)
    return jnp.einsum('grc,gh->grhc', blocks, eye).reshape(g * r, g * c)


def kernel(x, norm1_g, w_in, A_re, A_im, log_dt, B_re, B_im, C_re, C_im, D_skip, w_glu, b_glu, q_norm_g, k_norm_g,
           idx_k_norm_g, w_proj_ssm, w_proj_attn, w_out, norm2_g, w_ffn_gate, w_ffn_up, w_ffn_down):
    bsz, seq, _ = x.shape
    assert x.shape == (8, 2048, D_MODEL) and w_in.shape[0] == 1
    G, N, P = SSM_GROUPS, SSM_STATE, SSM_GROUP
    cparams = functools.partial(pltpu.CompilerParams, vmem_limit_bytes=VMEM_LIMIT)

    w = w_in[0]
    w_u, w_q, w_k, w_v, w_qi, w_ki, w_wi, w_gates = jnp.split(
        w, (512, 1024, 1088, 1152, 1408, 1472, 1476), axis=1)
    w1 = jnp.concatenate([w_u, w_q, w_qi, w_k, w_ki, w_wi, jnp.zeros((D_MODEL, HEAD_DIM - IDX_HEADS), F32), w_v],
                         axis=1).astype(BF16)
    g1 = norm1_g[0].reshape(1, D_MODEL)

    rep = lambda a: jnp.repeat(a, P, axis=0)
    gpn = jax.ShapeDtypeStruct((G * P, N), F32)
    abar_r, abar_i, bp_r, bp_i = pl.pallas_call(
        _disc_kernel, out_shape=(gpn, gpn, gpn, gpn), name="s5_discretise",
    )(rep(A_re[0]), rep(A_im[0]), rep(jnp.broadcast_to(log_dt[0].reshape(G, 1), (G, N))),
      B_re[0].transpose(0, 2, 1).reshape(G * P, N), B_im[0].transpose(0, 2, 1).reshape(G * P, N))
    abar_r, abar_i = abar_r[::P], abar_i[::P]
    bp_r, bp_i = bp_r.reshape(G, P, N), bp_i.reshape(G, P, N)
    hg = G // 2
    bp = [jnp.concatenate([_block_diag(bp_r[sl]), _block_diag(bp_i[sl])], axis=1).astype(BF16)
          for sl in (slice(0, hg), slice(hg, G))]
    c_r = C_re[0].transpose(0, 2, 1)
    c_i = C_im[0].transpose(0, 2, 1)
    cp = [jnp.concatenate([_block_diag(c_r[sl]), -_block_diag(c_i[sl])], axis=0).astype(BF16)
          for sl in (slice(0, hg), slice(hg, G))]
    ar8 = jnp.broadcast_to(abar_r.reshape(1, G * N), (SUBLANES, G * N))
    ai8 = jnp.broadcast_to(abar_i.reshape(1, G * N), (SUBLANES, G * N))

    n_in = seq // TL_IN
    u_t, q_h, qs_h, sg_t, k_n, ki_n, v_t = pl.pallas_call(
        _inproj_kernel,
        grid=(bsz, n_in),
        in_specs=[
            pl.BlockSpec((1, TL_IN, D_MODEL), lambda b, i: (b, i, 0)),
            _const_spec((1, D_MODEL)),
            _const_spec((D_MODEL, W1_COLS)),
            _const_spec((1, ATTN_WIDTH)), _const_spec((1, 2 * HEAD_DIM)), _const_spec((2 * LANES, 2 * LANES)),
        ],
        out_specs=[
            pl.BlockSpec((1, TL_IN, SSM_WIDTH), lambda b, i: (b, i, 0)),
            pl.BlockSpec((1, N_HEADS, HEAD_DIM, TL_IN), lambda b, i: (b, 0, 0, i)),
            pl.BlockSpec((1, IDX_HEADS, IDX_DIM, TL_IN), lambda b, i: (b, 0, 0, i)),
            pl.BlockSpec((1, SUBLANES, TL_IN), lambda b, i: (b, 0, i)),
            pl.BlockSpec((1, TL_IN, HEAD_DIM), lambda b, i: (b, i, 0)),
            pl.BlockSpec((1, TL_IN, IDX_DIM), lambda b, i: (b, i, 0)),
            pl.BlockSpec((1, TL_IN // TK, HEAD_DIM, TK), lambda b, i: (b, i, 0, 0)),
        ],
        out_shape=(
            jax.ShapeDtypeStruct((bsz, seq, SSM_WIDTH), F32),
            jax.ShapeDtypeStruct((bsz, N_HEADS, HEAD_DIM, seq), BF16),
            jax.ShapeDtypeStruct((bsz, IDX_HEADS, IDX_DIM, seq), BF16),
            jax.ShapeDtypeStruct((bsz, SUBLANES, seq), F32),
            jax.ShapeDtypeStruct((bsz, seq, HEAD_DIM), BF16),
            jax.ShapeDtypeStruct((bsz, seq, IDX_DIM), BF16),
            jax.ShapeDtypeStruct((bsz, seq // TK, HEAD_DIM, TK), BF16),
        ),
        compiler_params=cparams(dimension_semantics=("parallel", "parallel")),
        name="in_projection",
    )(x, g1, w1, jnp.tile(q_norm_g[0], N_HEADS).reshape(1, ATTN_WIDTH),
      jnp.concatenate([k_norm_g[0], idx_k_norm_g[0]]).reshape(1, 2 * HEAD_DIM),
      jnp.kron(jnp.eye(2 * LANES // HEAD_DIM, dtype=BF16), jnp.ones((HEAD_DIM, HEAD_DIM), BF16)))

    rows = SUBLANES * TC_SCAN
    ps_t = pl.pallas_call(
        _s5_kernel,
        grid=(seq // TC_SCAN,),
        in_specs=[
            pl.BlockSpec((bsz, TC_SCAN, SSM_WIDTH), lambda c: (0, c, 0)),
            _const_spec((256, 2048)), _const_spec((256, 2048)),
            _const_spec((2048, 256)), _const_spec((2048, 256)),
            _const_spec((SUBLANES, G * N)), _const_spec((SUBLANES, G * N)),
            _const_spec((1, SSM_WIDTH)), _const_spec((SSM_WIDTH, SSM_WIDTH)), _const_spec((1, SSM_WIDTH)),
            _const_spec((SSM_WIDTH, D_MODEL)),
        ],
        out_specs=pl.BlockSpec((bsz, TC_SCAN, D_MODEL), lambda c: (0, c, 0)),
        out_shape=jax.ShapeDtypeStruct((bsz, seq, D_MODEL), BF16),
        scratch_shapes=[pltpu.VMEM((rows, 2 * G * N), F32), pltpu.VMEM((SUBLANES, 2 * G * N), F32)],
        compiler_params=cparams(dimension_semantics=("arbitrary",)),
        name="s5_branch",
    )(u_t, bp[0], bp[1], cp[0], cp[1], ar8, ai8,
      D_skip[0].reshape(1, SSM_WIDTH), w_glu[0].astype(BF16), b_glu[0].reshape(1, SSM_WIDTH),
      w_proj_ssm[0].astype(BF16))

    y_att = pl.pallas_call(
        _attn_kernel,
        grid=(bsz, seq // QB),
        in_specs=[
            pl.BlockSpec((1, IDX_HEADS, IDX_DIM, QB), lambda b, i: (b, 0, 0, i)),
            pl.BlockSpec((1, SUBLANES, QB), lambda b, i: (b, 0, i)),
            pl.BlockSpec((1, N_HEADS, HEAD_DIM, QB), lambda b, i: (b, 0, 0, i)),
            pl.BlockSpec((1, seq, IDX_DIM), lambda b, i: (b, 0, 0)),
            pl.BlockSpec((1, seq, HEAD_DIM), lambda b, i: (b, 0, 0)),
            pl.BlockSpec((1, seq // TK, HEAD_DIM, TK), lambda b, i: (b, 0, 0, 0)),
        ],
        out_specs=pl.BlockSpec((1, QB, ATTN_WIDTH), lambda b, i: (b, i, 0)),
        out_shape=jax.ShapeDtypeStruct((bsz, seq, ATTN_WIDTH), BF16),
        scratch_shapes=[pltpu.VMEM((seq // TK, TK, QB), F32),
                        pltpu.VMEM((N_HEADS, SUBLANES, QB), F32),
                        pltpu.VMEM((N_HEADS, SUBLANES, QB), F32),
                        pltpu.VMEM((N_HEADS, HEAD_DIM, QB), F32),
                        pltpu.VMEM((N_HEADS // 2, TK, 2 * QB), F32)],
        compiler_params=cparams(dimension_semantics=("parallel", "arbitrary")),
        name="sparse_attention",
    )(qs_h, sg_t, q_h, ki_n, k_n, v_t)

    out = pl.pallas_call(
        _ffn_kernel,
        grid=(bsz, seq // TL_FFN),
        in_specs=[
            pl.BlockSpec((1, TL_FFN, D_MODEL), lambda b, i: (b, i, 0)),
            pl.BlockSpec((1, TL_FFN, D_MODEL), lambda b, i: (b, i, 0)),
            pl.BlockSpec((1, TL_FFN, ATTN_WIDTH), lambda b, i: (b, i, 0)),
            _const_spec((1, D_MODEL)),
            _const_spec((D_MODEL, 2 * D_MODEL)),
            _const_spec((ATTN_WIDTH, D_MODEL)),
            _const_spec((D_MODEL, D_MODEL)),
            _const_spec((1, D_MODEL)),
            _const_spec((D_MODEL, D_FF)), _const_spec((D_MODEL, D_FF)), _const_spec((D_FF, D_MODEL)),
        ],
        out_specs=pl.BlockSpec((1, TL_FFN, D_MODEL), lambda b, i: (b, i, 0)),
        out_shape=jax.ShapeDtypeStruct((bsz, seq, D_MODEL), F32),
        compiler_params=cparams(dimension_semantics=("parallel", "parallel")),
        name="merge_ffn",
    )(x, ps_t, y_att, g1, w_gates.astype(BF16), w_proj_attn[0].astype(BF16),
      w_out[0].astype(BF16), norm2_g[0].reshape(1, D_MODEL), w_ffn_gate[0].astype(BF16),
      w_ffn_up[0].astype(BF16), w_ffn_down[0].astype(BF16))
    return out
```

```python
import functools
import math

import jax
import jax.numpy as jnp
from jax import lax
from jax.experimental import pallas as pl
from jax.experimental.pallas import tpu as pltpu

F32 = jnp.float32
BF16 = jnp.bfloat16

D_MODEL = 1024
SSM_WIDTH = 512
SSM_GROUP = 16
SSM_GROUPS = 32
SSM_STATE = 64
N_HEADS = 8
HEAD_DIM = 64
ATTN_WIDTH = 512
IDX_HEADS = 4
IDX_DIM = 64
INDEX_TOPK = 256
D_FF = 2816
RMS_EPS = 1e-6

LANES = 128
SUBLANES = 8
VMEM_LIMIT = 56 * 1024 * 1024

TL_IN = 512
TC_SCAN = 64
QB = 256
TK = 256
TL_FFN = 256

W1_COLS = 1536
GATE_COL0 = 1476
NEG_BIG = -1e30
assert QB % LANES == 0


def _dot(a, b):
    return jnp.dot(a, b, preferred_element_type=F32)


def _rms(x, g):
    return x * lax.rsqrt(jnp.mean(x * x, axis=-1, keepdims=True) + RMS_EPS) * g


def _disc_kernel(are_ref, aim_ref, ldt_ref, bre_ref, bim_ref, abr_ref, abi_ref, bpr_ref, bpi_ref):
    ar = are_ref[...]
    ai = aim_ref[...]
    dt = jnp.exp(ldt_ref[...])
    mag = jnp.exp(ar * dt)
    abar_r = mag * jnp.cos(ai * dt)
    abar_i = mag * jnp.sin(ai * dt)
    den = ar * ar + ai * ai
    nr = abar_r - 1.0
    coef_r = (nr * ar + abar_i * ai) / den
    coef_i = (abar_i * ar - nr * ai) / den
    abr_ref[...] = abar_r
    abi_ref[...] = abar_i
    br = bre_ref[...]
    bi = bim_ref[...]
    bpr_ref[...] = coef_r * br - coef_i * bi
    bpi_ref[...] = coef_r * bi + coef_i * br


def _inproj_kernel(x_ref, g1_ref, w_ref, qg_ref, kg_ref, ikg_ref, seg_ref,
                   u_ref, q_ref, qs_ref, sg_ref, k_ref, ki_ref, v_ref, w_scr):
    @pl.when(jnp.logical_and(pl.program_id(0) == 0, pl.program_id(1) == 0))
    def _():
        w_scr[...] = w_ref[0].astype(BF16)

    x = x_ref[0]
    h = _rms(x, g1_ref[...]).astype(BF16)
    proj = _dot(h, w_scr[...])
    u_ref[0] = proj[:, 0:512]
    seg = seg_ref[...]

    def head_rms(v, g):
        sq = (v * v).astype(BF16)
        n = v.shape[1]
        w = min(n, seg.shape[0])
        ss = jnp.concatenate([_dot(sq[:, c:c + w], seg[0:w, 0:w]) for c in range(0, n, w)], axis=1)
        return v * lax.rsqrt(ss * (1.0 / HEAD_DIM) + RMS_EPS) * g

    def store_heads_transposed(ref, v):
        for g in range(v.shape[1] // LANES):
            tg = v[:, g * LANES:(g + 1) * LANES].T
            ref[0, 2 * g] = tg[0:HEAD_DIM].astype(BF16)
            ref[0, 2 * g + 1] = tg[HEAD_DIM:2 * HEAD_DIM].astype(BF16)

    qscale = (HEAD_DIM ** -0.5) * math.log2(math.e)
    store_heads_transposed(q_ref, head_rms(proj[:, 512:1024], qg_ref[...]) * qscale)

    kv = proj[:, 1024:1152]
    kiw = proj[:, 1408:1536]
    w_scale = (IDX_HEADS ** -0.5) * (IDX_DIM ** -0.5)
    wabs = jnp.abs(kiw) * w_scale
    sshape = (LANES, IDX_HEADS * IDX_DIM)
    row = lax.broadcasted_iota(jnp.int32, sshape, 0)
    col = lax.broadcasted_iota(jnp.int32, sshape, 1)
    spread = jnp.where(row == IDX_DIM + col // IDX_DIM, 1.0, 0.0).astype(BF16)
    whi = wabs.astype(BF16)
    wlo = (wabs - whi.astype(F32)).astype(BF16)
    wrep = _dot(whi, spread) + _dot(wlo, spread)
    store_heads_transposed(qs_ref, proj[:, 1152:1408] * wrep)

    k_ref[0] = head_rms(kv, kg_ref[...])[:, 0:HEAD_DIM].astype(BF16)
    ki_ref[0] = head_rms(kiw, ikg_ref[...])[:, 0:IDX_DIM].astype(BF16)
    t_kv = kv.T
    t_kiw = kiw.T
    sg_ref[0] = jnp.where(t_kiw[IDX_DIM:IDX_DIM + SUBLANES] >= 0, 1.0, -1.0)
    for c in range(TL_IN // TK):
        v_ref[0, c] = t_kv[HEAD_DIM:2 * HEAD_DIM, c * TK:(c + 1) * TK].astype(BF16)


def _s5_kernel(u_ref, bplo_ref, bphi_ref, cplo_ref, cphi_ref, ar_ref, ai_ref, d_ref, wglu_ref, bglu_ref,
               wps_ref, o_ref, bu_scr, st_scr):
    @pl.when(pl.program_id(0) == 0)
    def _():
        st_scr[...] = jnp.zeros_like(st_scr)

    u = jnp.transpose(u_ref[...], (1, 0, 2)).reshape(SUBLANES * TC_SCAN, SSM_WIDTH)
    ub = u.astype(BF16)
    half_w = 2 * (SSM_GROUPS // 2) * SSM_STATE
    n_re = half_w // 2

    def scan_half(h):
        c0 = h * half_w
        a_r = ar_ref[:, h * n_re:(h + 1) * n_re]
        a_i = ai_ref[:, h * n_re:(h + 1) * n_re]
        sr = st_scr[:, c0:c0 + n_re]
        si = st_scr[:, c0 + n_re:c0 + half_w]
        for t in range(TC_SCAN):
            rows = slice(t * SUBLANES, (t + 1) * SUBLANES)
            br = bu_scr[rows, c0:c0 + n_re]
            bi = bu_scr[rows, c0 + n_re:c0 + half_w]
            sr, si = a_r * sr - a_i * si + br, a_r * si + a_i * sr + bi
            bu_scr[rows, c0:c0 + n_re] = sr
            bu_scr[rows, c0 + n_re:c0 + half_w] = si
        st_scr[:, c0:c0 + n_re] = sr
        st_scr[:, c0 + n_re:c0 + half_w] = si

    bu_scr[:, 0:half_w] = _dot(ub[:, 0:256], bplo_ref[...])
    bu_scr[:, half_w:2 * half_w] = _dot(ub[:, 256:512], bphi_ref[...])
    scan_half(0)
    y_lo = _dot(bu_scr[:, 0:half_w].astype(BF16), cplo_ref[...])
    scan_half(1)
    y_hi = _dot(bu_scr[:, half_w:2 * half_w].astype(BF16), cphi_ref[...])
    y = jnp.concatenate([y_lo, y_hi], axis=1)
    y = jax.nn.gelu(y + d_ref[...] * u)
    z = _dot(y.astype(BF16), wglu_ref[...]) + bglu_ref[...]
    y = y * jax.nn.sigmoid(z)
    ps = _dot(y.astype(BF16), wps_ref[...])
    o_ref[...] = jnp.transpose(ps.reshape(TC_SCAN, SUBLANES, D_MODEL), (1, 0, 2)).astype(BF16)


def _key_to_f32(key):
    bits = jnp.where(key < 0, key & jnp.int32(0x7FFFFFFF), ~key)
    return pltpu.bitcast(bits, F32)


def _attn_kernel(qs_ref, sg_ref, q_ref, ki_ref, k_ref, vt_ref, o_ref, s_scr, m_scr, l_scr, acc_scr, sp_scr):
    i = pl.program_id(1)
    n_tiles = (i * QB + QB + TK - 1) // TK
    nv = TK // SUBLANES
    shape3 = (nv, SUBLANES, QB)
    k_in_tile = (lax.broadcasted_iota(jnp.int32, shape3, 0) * SUBLANES
                 + lax.broadcasted_iota(jnp.int32, shape3, 1))
    q_pos = i * QB + lax.broadcasted_iota(jnp.int32, shape3, 2)
    ksel = float(INDEX_TOPK)

    def all_sublanes(a, op):
        for sh in (4, 2, 1):
            a = op(a, pltpu.roll(a, sh, 0))
        return a

    sg = sg_ref[0]

    def idx_body(j, carry):
        kt = ki_ref[0, pl.ds(pl.multiple_of(j * TK, TK), TK), :]
        acc = jnp.zeros((TK, QB), F32)
        for pr in range(IDX_HEADS // 2):
            x = _dot(kt, jnp.concatenate([qs_ref[0, 2 * pr], qs_ref[0, 2 * pr + 1]], axis=1))
            for e in range(2):
                hd = 2 * pr + e
                acc = acc + sg[hd:hd + 1, :] * jnp.maximum(x[:, e * QB:(e + 1) * QB], 0.0)
        vis = j * TK + k_in_tile <= q_pos
        s_scr[j] = jnp.where(vis, acc.reshape(shape3), -jnp.inf).reshape(TK, QB)
        return carry

    lax.fori_loop(0, n_tiles, idx_body, 0)

    def count(pred):
        def body(j, acc):
            hit = jnp.where(pred(s_scr[j].reshape(shape3), j), 1.0, 0.0)
            return acc + jnp.sum(hit.reshape(nv // 4, 4, SUBLANES, QB), axis=0)
        acc = lax.fori_loop(0, n_tiles, body, jnp.zeros((4, SUBLANES, QB), F32))
        return all_sublanes(jnp.sum(acc, axis=0), jnp.add)

    def bit_body(it, key):
        cand = key | lax.shift_left(jnp.int32(1), 31 - it)
        tc = _key_to_f32(cand)
        cnt = count(lambda s, j: s >= tc)
        below_ninf = lax.shift_right_logical(cand, 23) == 0
        return jnp.where(below_ninf | (cnt >= ksel), cand, key)

    key = lax.fori_loop(0, 32, bit_body, jnp.zeros((SUBLANES, QB), jnp.int32))
    thr = _key_to_f32(key)
    cnt_ge = count(lambda s, j: s >= thr)
    has_excess = jnp.max(cnt_ge) > ksel

    def write_bias(select):
        def body(j, carry):
            s = s_scr[j].reshape(shape3)
            vis = j * TK + k_in_tile <= q_pos
            bias = jnp.where(vis, jnp.where(select(s, j), 0.0, -jnp.inf), -jnp.inf)
            s_scr[j] = bias.reshape(TK, QB)
            return carry
        lax.fori_loop(0, n_tiles, body, 0)

    @pl.when(jnp.logical_not(has_excess))
    def _():
        write_bias(lambda s, j: s >= thr)

    @pl.when(has_excess)
    def _():
        cnt_gt = count(lambda s, j: s > thr)
        need = ksel - cnt_gt

        def jbit_body(it, jkey):
            cand = jkey | lax.shift_left(jnp.int32(1), 10 - it)
            cnt = count(lambda s, j: (s == thr) & (j * TK + k_in_tile < cand))
            return jnp.where(cnt < need, cand, jkey)

        jkey = lax.fori_loop(0, 11, jbit_body, jnp.zeros((SUBLANES, QB), jnp.int32))
        write_bias(lambda s, j: (s > thr) | ((s == thr) & (j * TK + k_in_tile <= jkey)))

    m_scr[...] = jnp.full(m_scr.shape, NEG_BIG, F32)
    l_scr[...] = jnp.zeros(l_scr.shape, F32)
    acc_scr[...] = jnp.zeros(acc_scr.shape, F32)

    def att_body(j, carry):
        kt = k_ref[0, pl.ds(pl.multiple_of(j * TK, TK), TK), :]
        vt = vt_ref[0, j]
        bias = s_scr[j].reshape(shape3)
        for pr in range(N_HEADS // 2):
            sp_scr[pr] = _dot(kt, jnp.concatenate([q_ref[0, 2 * pr], q_ref[0, 2 * pr + 1]], axis=1))
        for pr in range(N_HEADS // 2):
            for e in range(2):
                hd = 2 * pr + e
                s = sp_scr[pr, :, e * QB:(e + 1) * QB].reshape(shape3) + bias
                m_old = m_scr[hd]
                m_new = jnp.maximum(m_old, all_sublanes(jnp.max(s, axis=0), jnp.maximum))
                alpha = jnp.exp2(m_old - m_new)
                p = jnp.exp2(s - m_new)
                l_scr[hd] = alpha * l_scr[hd] + jnp.sum(p, axis=0)
                pv = _dot(vt, p.reshape(TK, QB).astype(BF16))
                acc = acc_scr[hd].reshape(HEAD_DIM // SUBLANES, SUBLANES, QB) * alpha
                acc_scr[hd] = acc.reshape(HEAD_DIM, QB) + pv
                m_scr[hd] = m_new
        return carry

    lax.fori_loop(0, n_tiles, att_body, 0)

    outs = []
    for hd in range(N_HEADS):
        l = all_sublanes(l_scr[hd], jnp.add)
        o = acc_scr[hd].reshape(HEAD_DIM // SUBLANES, SUBLANES, QB) / l
        outs.append(o.reshape(HEAD_DIM, QB))
    o_ref[0] = jnp.concatenate(outs, axis=0).T.astype(BF16)


def _ffn_kernel(x_ref, ps_ref, ya_ref, g1_ref, wgate_ref, wpa_ref, wo_ref, g2_ref, wfg_ref, wfu_ref, wfd_ref,
                o_ref):
    x = x_ref[0]
    h = _rms(x, g1_ref[...]).astype(BF16)
    gates = _dot(h, wgate_ref[...])
    pa = _dot(ya_ref[0], wpa_ref[...])
    merged = (jax.nn.sigmoid(gates[:, 0:D_MODEL]) * ps_ref[0].astype(F32)
              + jax.nn.sigmoid(gates[:, D_MODEL:2 * D_MODEL]) * pa)
    x1 = x + _dot(merged.astype(BF16), wo_ref[...])
    h2 = _rms(x1, g2_ref[...]).astype(BF16)
    hid = jax.nn.silu(_dot(h2, wfg_ref[...])) * _dot(h2, wfu_ref[...])
    o_ref[0] = x1 + _dot(hid.astype(BF16), wfd_ref[...])


def _const_spec(shape):
    nd = len(shape)
    return pl.BlockSpec(shape, lambda *_: (0,) * nd, pipeline_mode=pl.Buffered(1))


def _block_diag(blocks):
    g, r, c = blocks.shape
    eye = jnp.eye(g, dtype=blocks.dtype)
    return jnp.einsum('grc,gh->grhc', blocks, eye).reshape(g * r, g * c)


def kernel(x, norm1_g, w_in, A_re, A_im, log_dt, B_re, B_im, C_re, C_im, D_skip, w_glu, b_glu, q_norm_g, k_norm_g,
           idx_k_norm_g, w_proj_ssm, w_proj_attn, w_out, norm2_g, w_ffn_gate, w_ffn_up, w_ffn_down):
    bsz, seq, _ = x.shape
    assert x.shape == (8, 2048, D_MODEL) and w_in.shape[0] == 1
    G, N, P = SSM_GROUPS, SSM_STATE, SSM_GROUP
    cparams = functools.partial(pltpu.CompilerParams, vmem_limit_bytes=VMEM_LIMIT)

    w_gates = w_in[0, :, GATE_COL0:GATE_COL0 + 2 * D_MODEL]
    g1 = norm1_g[0].reshape(1, D_MODEL)
    ones64 = jnp.ones((HEAD_DIM,), F32)

    rep = lambda a: jnp.repeat(a, P, axis=0)
    gpn = jax.ShapeDtypeStruct((G * P, N), F32)
    abar_r, abar_i, bp_r, bp_i = pl.pallas_call(
        _disc_kernel, out_shape=(gpn, gpn, gpn, gpn), name="s5_discretise",
    )(rep(A_re[0]), rep(A_im[0]), rep(jnp.broadcast_to(log_dt[0].reshape(G, 1), (G, N))),
      B_re[0].transpose(0, 2, 1).reshape(G * P, N), B_im[0].transpose(0, 2, 1).reshape(G * P, N))
    abar_r, abar_i = abar_r[::P], abar_i[::P]
    bp_r, bp_i = bp_r.reshape(G, P, N), bp_i.reshape(G, P, N)
    hg = G // 2
    bp = [jnp.concatenate([_block_diag(bp_r[sl]), _block_diag(bp_i[sl])], axis=1).astype(BF16)
          for sl in (slice(0, hg), slice(hg, G))]
    c_r = C_re[0].transpose(0, 2, 1)
    c_i = C_im[0].transpose(0, 2, 1)
    cp = [jnp.concatenate([_block_diag(c_r[sl]), -_block_diag(c_i[sl])], axis=0).astype(BF16)
          for sl in (slice(0, hg), slice(hg, G))]
    ar8 = jnp.broadcast_to(abar_r.reshape(1, G * N), (SUBLANES, G * N))
    ai8 = jnp.broadcast_to(abar_i.reshape(1, G * N), (SUBLANES, G * N))

    n_in = seq // TL_IN
    u_t, q_h, qs_h, sg_t, k_n, ki_n, v_t = pl.pallas_call(
        _inproj_kernel,
        grid=(bsz, n_in),
        in_specs=[
            pl.BlockSpec((1, TL_IN, D_MODEL), lambda b, i: (b, i, 0)),
            _const_spec((1, D_MODEL)),
            _const_spec((1, D_MODEL, W1_COLS)),
            _const_spec((1, ATTN_WIDTH)), _const_spec((1, LANES)), _const_spec((1, LANES)),
            _const_spec((2 * LANES, 2 * LANES)),
        ],
        out_specs=[
            pl.BlockSpec((1, TL_IN, SSM_WIDTH), lambda b, i: (b, i, 0)),
            pl.BlockSpec((1, N_HEADS, HEAD_DIM, TL_IN), lambda b, i: (b, 0, 0, i)),
            pl.BlockSpec((1, IDX_HEADS, IDX_DIM, TL_IN), lambda b, i: (b, 0, 0, i)),
            pl.BlockSpec((1, SUBLANES, TL_IN), lambda b, i: (b, 0, i)),
            pl.BlockSpec((1, TL_IN, HEAD_DIM), lambda b, i: (b, i, 0)),
            pl.BlockSpec((1, TL_IN, IDX_DIM), lambda b, i: (b, i, 0)),
            pl.BlockSpec((1, TL_IN // TK, HEAD_DIM, TK), lambda b, i: (b, i, 0, 0)),
        ],
        out_shape=(
            jax.ShapeDtypeStruct((bsz, seq, SSM_WIDTH), F32),
            jax.ShapeDtypeStruct((bsz, N_HEADS, HEAD_DIM, seq), BF16),
            jax.ShapeDtypeStruct((bsz, IDX_HEADS, IDX_DIM, seq), BF16),
            jax.ShapeDtypeStruct((bsz, SUBLANES, seq), F32),
            jax.ShapeDtypeStruct((bsz, seq, HEAD_DIM), BF16),
            jax.ShapeDtypeStruct((bsz, seq, IDX_DIM), BF16),
            jax.ShapeDtypeStruct((bsz, seq // TK, HEAD_DIM, TK), BF16),
        ),
        scratch_shapes=[pltpu.VMEM((D_MODEL, W1_COLS), BF16)],
        compiler_params=cparams(dimension_semantics=("arbitrary", "arbitrary")),
        name="in_projection",
    )(x, g1, w_in, jnp.tile(q_norm_g[0], N_HEADS).reshape(1, ATTN_WIDTH),
      jnp.concatenate([k_norm_g[0], ones64]).reshape(1, LANES),
      jnp.concatenate([idx_k_norm_g[0], ones64]).reshape(1, LANES),
      jnp.kron(jnp.eye(2 * LANES // HEAD_DIM, dtype=BF16), jnp.ones((HEAD_DIM, HEAD_DIM), BF16)))

    rows = SUBLANES * TC_SCAN
    ps_t = pl.pallas_call(
        _s5_kernel,
        grid=(seq // TC_SCAN,),
        in_specs=[
            pl.BlockSpec((bsz, TC_SCAN, SSM_WIDTH), lambda c: (0, c, 0)),
            _const_spec((256, 2048)), _const_spec((256, 2048)),
            _const_spec((2048, 256)), _const_spec((2048, 256)),
            _const_spec((SUBLANES, G * N)), _const_spec((SUBLANES, G * N)),
            _const_spec((1, SSM_WIDTH)), _const_spec((SSM_WIDTH, SSM_WIDTH)), _const_spec((1, SSM_WIDTH)),
            _const_spec((SSM_WIDTH, D_MODEL)),
        ],
        out_specs=pl.BlockSpec((bsz, TC_SCAN, D_MODEL), lambda c: (0, c, 0)),
        out_shape=jax.ShapeDtypeStruct((bsz, seq, D_MODEL), BF16),
        scratch_shapes=[pltpu.VMEM((rows, 2 * G * N), F32), pltpu.VMEM((SUBLANES, 2 * G * N), F32)],
        compiler_params=cparams(dimension_semantics=("arbitrary",)),
        name="s5_branch",
    )(u_t, bp[0], bp[1], cp[0], cp[1], ar8, ai8,
      D_skip[0].reshape(1, SSM_WIDTH), w_glu[0].astype(BF16), b_glu[0].reshape(1, SSM_WIDTH),
      w_proj_ssm[0].astype(BF16))

    y_att = pl.pallas_call(
        _attn_kernel,
        grid=(bsz, seq // QB),
        in_specs=[
            pl.BlockSpec((1, IDX_HEADS, IDX_DIM, QB), lambda b, i: (b, 0, 0, i)),
            pl.BlockSpec((1, SUBLANES, QB), lambda b, i: (b, 0, i)),
            pl.BlockSpec((1, N_HEADS, HEAD_DIM, QB), lambda b, i: (b, 0, 0, i)),
            pl.BlockSpec((1, seq, IDX_DIM), lambda b, i: (b, 0, 0)),
            pl.BlockSpec((1, seq, HEAD_DIM), lambda b, i: (b, 0, 0)),
            pl.BlockSpec((1, seq // TK, HEAD_DIM, TK), lambda b, i: (b, 0, 0, 0)),
        ],
        out_specs=pl.BlockSpec((1, QB, ATTN_WIDTH), lambda b, i: (b, i, 0)),
        out_shape=jax.ShapeDtypeStruct((bsz, seq, ATTN_WIDTH), BF16),
        scratch_shapes=[pltpu.VMEM((seq // TK, TK, QB), F32),
                        pltpu.VMEM((N_HEADS, SUBLANES, QB), F32),
                        pltpu.VMEM((N_HEADS, SUBLANES, QB), F32),
                        pltpu.VMEM((N_HEADS, HEAD_DIM, QB), F32),
                        pltpu.VMEM((N_HEADS // 2, TK, 2 * QB), F32)],
        compiler_params=cparams(dimension_semantics=("parallel", "arbitrary")),
        name="sparse_attention",
    )(qs_h, sg_t, q_h, ki_n, k_n, v_t)

    out = pl.pallas_call(
        _ffn_kernel,
        grid=(bsz, seq // TL_FFN),
        in_specs=[
            pl.BlockSpec((1, TL_FFN, D_MODEL), lambda b, i: (b, i, 0)),
            pl.BlockSpec((1, TL_FFN, D_MODEL), lambda b, i: (b, i, 0)),
            pl.BlockSpec((1, TL_FFN, ATTN_WIDTH), lambda b, i: (b, i, 0)),
            _const_spec((1, D_MODEL)),
            _const_spec((D_MODEL, 2 * D_MODEL)),
            _const_spec((ATTN_WIDTH, D_MODEL)),
            _const_spec((D_MODEL, D_MODEL)),
            _const_spec((1, D_MODEL)),
            _const_spec((D_MODEL, D_FF)), _const_spec((D_MODEL, D_FF)), _const_spec((D_FF, D_MODEL)),
        ],
        out_specs=pl.BlockSpec((1, TL_FFN, D_MODEL), lambda b, i: (b, i, 0)),
        out_shape=jax.ShapeDtypeStruct((bsz, seq, D_MODEL), F32),
        compiler_params=cparams(dimension_semantics=("parallel", "parallel")),
        name="merge_ffn",
    )(x, ps_t, y_att, g1, w_gates.astype(BF16), w_proj_attn[0].astype(BF16),
      w_out[0].astype(BF16), norm2_g[0].reshape(1, D_MODEL), w_ffn_gate[0].astype(BF16),
      w_ffn_up[0].astype(BF16), w_ffn_down[0].astype(BF16))
    return out
```

```python
import functools
import math

import jax
import jax.numpy as jnp
from jax import lax
from jax.experimental import pallas as pl
from jax.experimental.pallas import tpu as pltpu

F32 = jnp.float32
BF16 = jnp.bfloat16

D_MODEL = 1024
SSM_WIDTH = 512
SSM_GROUP = 16
SSM_GROUPS = 32
SSM_STATE = 64
N_HEADS = 8
HEAD_DIM = 64
ATTN_WIDTH = 512
IDX_HEADS = 4
IDX_DIM = 64
INDEX_TOPK = 256
D_FF = 2816
RMS_EPS = 1e-6

LANES = 128
SUBLANES = 8
VMEM_LIMIT = 56 * 1024 * 1024

TL_IN = 512
TC_SCAN = 64
QB = 256
TK = 256
TL_FFN = 256

W1_COLS = 1536
GATE_COL0 = 1476
GATE_RB = 64
NEG_BIG = -1e30
assert QB % LANES == 0


def _dot(a, b):
    return jnp.dot(a, b, preferred_element_type=F32)


def _rms(x, g):
    return x * lax.rsqrt(jnp.mean(x * x, axis=-1, keepdims=True) + RMS_EPS) * g


def _disc_kernel(are_ref, aim_ref, ldt_ref, bre_ref, bim_ref, abr_ref, abi_ref, bpr_ref, bpi_ref):
    ar = are_ref[...]
    ai = aim_ref[...]
    dt = jnp.exp(ldt_ref[...])
    mag = jnp.exp(ar * dt)
    abar_r = mag * jnp.cos(ai * dt)
    abar_i = mag * jnp.sin(ai * dt)
    den = ar * ar + ai * ai
    nr = abar_r - 1.0
    coef_r = (nr * ar + abar_i * ai) / den
    coef_i = (abar_i * ar - nr * ai) / den
    abr_ref[...] = abar_r
    abi_ref[...] = abar_i
    br = bre_ref[...]
    bi = bim_ref[...]
    bpr_ref[...] = coef_r * br - coef_i * bi
    bpi_ref[...] = coef_r * bi + coef_i * br


def _inproj_kernel(x_ref, g1_ref, w_ref, qg_ref, kg_ref, ikg_ref, seg_ref,
                   u_ref, q_ref, qs_ref, sg_ref, k_ref, ki_ref, v_ref, w_scr):
    @pl.when(jnp.logical_and(pl.program_id(0) == 0, pl.program_id(1) == 0))
    def _():
        w_scr[...] = w_ref[...].T.astype(BF16)

    x = x_ref[0]
    h = _rms(x, g1_ref[...]).astype(BF16)
    proj = _dot(h, w_scr[...])
    u_ref[0] = proj[:, 0:512]
    seg = seg_ref[...]

    def head_rms(v, g):
        sq = (v * v).astype(BF16)
        n = v.shape[1]
        w = min(n, seg.shape[0])
        ss = jnp.concatenate([_dot(sq[:, c:c + w], seg[0:w, 0:w]) for c in range(0, n, w)], axis=1)
        return v * lax.rsqrt(ss * (1.0 / HEAD_DIM) + RMS_EPS) * g

    def store_heads_transposed(ref, v):
        for g in range(v.shape[1] // LANES):
            tg = v[:, g * LANES:(g + 1) * LANES].T
            ref[0, 2 * g] = tg[0:HEAD_DIM].astype(BF16)
            ref[0, 2 * g + 1] = tg[HEAD_DIM:2 * HEAD_DIM].astype(BF16)

    qscale = (HEAD_DIM ** -0.5) * math.log2(math.e)
    store_heads_transposed(q_ref, head_rms(proj[:, 512:1024], qg_ref[...]) * qscale)

    kv = proj[:, 1024:1152]
    kiw = proj[:, 1408:1536]
    w_scale = (IDX_HEADS ** -0.5) * (IDX_DIM ** -0.5)
    wabs = jnp.abs(kiw) * w_scale
    sshape = (LANES, IDX_HEADS * IDX_DIM)
    row = lax.broadcasted_iota(jnp.int32, sshape, 0)
    col = lax.broadcasted_iota(jnp.int32, sshape, 1)
    spread = jnp.where(row == IDX_DIM + col // IDX_DIM, 1.0, 0.0).astype(BF16)
    whi = wabs.astype(BF16)
    wlo = (wabs - whi.astype(F32)).astype(BF16)
    wrep = _dot(whi, spread) + _dot(wlo, spread)
    store_heads_transposed(qs_ref, proj[:, 1152:1408] * wrep)

    k_ref[0] = head_rms(kv, kg_ref[...])[:, 0:HEAD_DIM].astype(BF16)
    ki_ref[0] = head_rms(kiw, ikg_ref[...])[:, 0:IDX_DIM].astype(BF16)
    t_kv = kv.T
    t_kiw = kiw.T
    sg_ref[0] = jnp.where(t_kiw[IDX_DIM:IDX_DIM + SUBLANES] >= 0, 1.0, -1.0)
    for c in range(TL_IN // TK):
        v_ref[0, c] = t_kv[HEAD_DIM:2 * HEAD_DIM, c * TK:(c + 1) * TK].astype(BF16)


def _s5_kernel(u_ref, bplo_ref, bphi_ref, cplo_ref, cphi_ref, ar_ref, ai_ref, d_ref, wglu_ref, bglu_ref,
               wps_ref, o_ref, bu_scr, st_scr):
    @pl.when(pl.program_id(0) == 0)
    def _():
        st_scr[...] = jnp.zeros_like(st_scr)

    u = jnp.transpose(u_ref[...], (1, 0, 2)).reshape(SUBLANES * TC_SCAN, SSM_WIDTH)
    ub = u.astype(BF16)
    half_w = 2 * (SSM_GROUPS // 2) * SSM_STATE
    n_re = half_w // 2

    def scan_half(h):
        c0 = h * half_w
        a_r = ar_ref[:, h * n_re:(h + 1) * n_re]
        a_i = ai_ref[:, h * n_re:(h + 1) * n_re]
        sr = st_scr[:, c0:c0 + n_re]
        si = st_scr[:, c0 + n_re:c0 + half_w]
        for t in range(TC_SCAN):
            rows = slice(t * SUBLANES, (t + 1) * SUBLANES)
            br = bu_scr[rows, c0:c0 + n_re]
            bi = bu_scr[rows, c0 + n_re:c0 + half_w]
            sr, si = a_r * sr - a_i * si + br, a_r * si + a_i * sr + bi
            bu_scr[rows, c0:c0 + n_re] = sr
            bu_scr[rows, c0 + n_re:c0 + half_w] = si
        st_scr[:, c0:c0 + n_re] = sr
        st_scr[:, c0 + n_re:c0 + half_w] = si

    bu_scr[:, 0:half_w] = _dot(ub[:, 0:256], bplo_ref[...])
    bu_scr[:, half_w:2 * half_w] = _dot(ub[:, 256:512], bphi_ref[...])
    scan_half(0)
    y_lo = _dot(bu_scr[:, 0:half_w].astype(BF16), cplo_ref[...])
    scan_half(1)
    y_hi = _dot(bu_scr[:, half_w:2 * half_w].astype(BF16), cphi_ref[...])
    y = jnp.concatenate([y_lo, y_hi], axis=1)
    y = jax.nn.gelu(y + d_ref[...] * u)
    z = _dot(y.astype(BF16), wglu_ref[...]) + bglu_ref[...]
    y = y * jax.nn.sigmoid(z)
    ps = _dot(y.astype(BF16), wps_ref[...])
    o_ref[...] = jnp.transpose(ps.reshape(TC_SCAN, SUBLANES, D_MODEL), (1, 0, 2)).astype(BF16)


def _key_to_f32(key):
    bits = jnp.where(key < 0, key & jnp.int32(0x7FFFFFFF), ~key)
    return pltpu.bitcast(bits, F32)


def _attn_kernel(qs_ref, sg_ref, q_ref, ki_ref, k_ref, vt_ref, o_ref, s_scr, m_scr, l_scr, acc_scr, sp_scr):
    i = pl.program_id(1)
    n_tiles = (i * QB + QB + TK - 1) // TK
    nv = TK // SUBLANES
    shape3 = (nv, SUBLANES, QB)
    k_in_tile = (lax.broadcasted_iota(jnp.int32, shape3, 0) * SUBLANES
                 + lax.broadcasted_iota(jnp.int32, shape3, 1))
    q_pos = i * QB + lax.broadcasted_iota(jnp.int32, shape3, 2)
    ksel = float(INDEX_TOPK)

    def all_sublanes(a, op):
        for sh in (4, 2, 1):
            a = op(a, pltpu.roll(a, sh, 0))
        return a

    sg = sg_ref[0]

    def idx_body(j, carry):
        kt = ki_ref[0, pl.ds(pl.multiple_of(j * TK, TK), TK), :]
        acc = jnp.zeros((TK, QB), F32)
        for pr in range(IDX_HEADS // 2):
            x = _dot(kt, jnp.concatenate([qs_ref[0, 2 * pr], qs_ref[0, 2 * pr + 1]], axis=1))
            for e in range(2):
                hd = 2 * pr + e
                acc = acc + sg[hd:hd + 1, :] * jnp.maximum(x[:, e * QB:(e + 1) * QB], 0.0)
        vis = j * TK + k_in_tile <= q_pos
        s_scr[j] = jnp.where(vis, acc.reshape(shape3), -jnp.inf).reshape(TK, QB)
        return carry

    lax.fori_loop(0, n_tiles, idx_body, 0)

    def count(pred):
        def body(j, acc):
            hit = jnp.where(pred(s_scr[j].reshape(shape3), j), 1.0, 0.0)
            return acc + jnp.sum(hit.reshape(nv // 4, 4, SUBLANES, QB), axis=0)
        acc = lax.fori_loop(0, n_tiles, body, jnp.zeros((4, SUBLANES, QB), F32))
        return all_sublanes(jnp.sum(acc, axis=0), jnp.add)

    def bit_body(it, key):
        cand = key | lax.shift_left(jnp.int32(1), 31 - it)
        tc = _key_to_f32(cand)
        cnt = count(lambda s, j: s >= tc)
        below_ninf = lax.shift_right_logical(cand, 23) == 0
        return jnp.where(below_ninf | (cnt >= ksel), cand, key)

    key = lax.fori_loop(0, 32, bit_body, jnp.zeros((SUBLANES, QB), jnp.int32))
    thr = _key_to_f32(key)
    cnt_ge = count(lambda s, j: s >= thr)
    has_excess = jnp.max(cnt_ge) > ksel

    def write_bias(select):
        def body(j, carry):
            s = s_scr[j].reshape(shape3)
            vis = j * TK + k_in_tile <= q_pos
            bias = jnp.where(vis, jnp.where(select(s, j), 0.0, -jnp.inf), -jnp.inf)
            s_scr[j] = bias.reshape(TK, QB)
            return carry
        lax.fori_loop(0, n_tiles, body, 0)

    @pl.when(jnp.logical_not(has_excess))
    def _():
        write_bias(lambda s, j: s >= thr)

    @pl.when(has_excess)
    def _():
        cnt_gt = count(lambda s, j: s > thr)
        need = ksel - cnt_gt

        def jbit_body(it, jkey):
            cand = jkey | lax.shift_left(jnp.int32(1), 10 - it)
            cnt = count(lambda s, j: (s == thr) & (j * TK + k_in_tile < cand))
            return jnp.where(cnt < need, cand, jkey)

        jkey = lax.fori_loop(0, 11, jbit_body, jnp.zeros((SUBLANES, QB), jnp.int32))
        write_bias(lambda s, j: (s > thr) | ((s == thr) & (j * TK + k_in_tile <= jkey)))

    m_scr[...] = jnp.full(m_scr.shape, NEG_BIG, F32)
    l_scr[...] = jnp.zeros(l_scr.shape, F32)
    acc_scr[...] = jnp.zeros(acc_scr.shape, F32)

    def att_body(j, carry):
        kt = k_ref[0, pl.ds(pl.multiple_of(j * TK, TK), TK), :]
        vt = vt_ref[0, j]
        bias = s_scr[j].reshape(shape3)
        for pr in range(N_HEADS // 2):
            sp_scr[pr] = _dot(kt, jnp.concatenate([q_ref[0, 2 * pr], q_ref[0, 2 * pr + 1]], axis=1))
        for pr in range(N_HEADS // 2):
            for e in range(2):
                hd = 2 * pr + e
                s = sp_scr[pr, :, e * QB:(e + 1) * QB].reshape(shape3) + bias
                m_old = m_scr[hd]
                m_new = jnp.maximum(m_old, all_sublanes(jnp.max(s, axis=0), jnp.maximum))
                alpha = jnp.exp2(m_old - m_new)
                p = jnp.exp2(s - m_new)
                l_scr[hd] = alpha * l_scr[hd] + jnp.sum(p, axis=0)
                pv = _dot(vt, p.reshape(TK, QB).astype(BF16))
                acc = acc_scr[hd].reshape(HEAD_DIM // SUBLANES, SUBLANES, QB) * alpha
                acc_scr[hd] = acc.reshape(HEAD_DIM, QB) + pv
                m_scr[hd] = m_new
        return carry

    lax.fori_loop(0, n_tiles, att_body, 0)

    outs = []
    for hd in range(N_HEADS):
        l = all_sublanes(l_scr[hd], jnp.add)
        o = acc_scr[hd].reshape(HEAD_DIM // SUBLANES, SUBLANES, QB) / l
        outs.append(o.reshape(HEAD_DIM, QB))
    o_ref[0] = jnp.concatenate(outs, axis=0).T.astype(BF16)


def _ffn_kernel(x_ref, ps_ref, ya_ref, g1_ref, wgate_ref, wpa_ref, wo_ref, g2_ref, wfg_ref, wfu_ref, wfd_ref,
                o_ref):
    x = x_ref[0]
    h = _rms(x, g1_ref[...]).astype(BF16)
    gates = _dot(h, wgate_ref[...])
    pa = _dot(ya_ref[0], wpa_ref[...])
    merged = (jax.nn.sigmoid(gates[:, 0:D_MODEL]) * ps_ref[0].astype(F32)
              + jax.nn.sigmoid(gates[:, D_MODEL:2 * D_MODEL]) * pa)
    x1 = x + _dot(merged.astype(BF16), wo_ref[...])
    h2 = _rms(x1, g2_ref[...]).astype(BF16)
    hid = jax.nn.silu(_dot(h2, wfg_ref[...])) * _dot(h2, wfu_ref[...])
    o_ref[0] = x1 + _dot(hid.astype(BF16), wfd_ref[...])


def _gate_repack_kernel(a_ref, b_ref, c_ref, o_ref):
    off = GATE_COL0 % GATE_RB
    rows = jnp.concatenate([a_ref[off:], b_ref[...], c_ref[:off]], axis=0)
    o_ref[...] = rows.T.astype(BF16)


def _const_spec(shape):
    nd = len(shape)
    return pl.BlockSpec(shape, lambda *_: (0,) * nd, pipeline_mode=pl.Buffered(1))


def _block_diag(blocks):
    g, r, c = blocks.shape
    eye = jnp.eye(g, dtype=blocks.dtype)
    return jnp.einsum('grc,gh->grhc', blocks, eye).reshape(g * r, g * c)


def kernel(x, norm1_g, w_in, A_re, A_im, log_dt, B_re, B_im, C_re, C_im, D_skip, w_glu, b_glu, q_norm_g, k_norm_g,
           idx_k_norm_g, w_proj_ssm, w_proj_attn, w_out, norm2_g, w_ffn_gate, w_ffn_up, w_ffn_down):
    bsz, seq, _ = x.shape
    assert x.shape == (8, 2048, D_MODEL) and w_in.shape[0] == 1
    G, N, P = SSM_GROUPS, SSM_STATE, SSM_GROUP
    cparams = functools.partial(pltpu.CompilerParams, vmem_limit_bytes=VMEM_LIMIT)

    w_in_t = w_in[0].T
    gblk = GATE_COL0 // GATE_RB
    spec = lambda k: pl.BlockSpec((GATE_RB, D_MODEL), lambda j: (gblk + 2 * j + k, 0))
    w_gates = pl.pallas_call(
        _gate_repack_kernel,
        grid=(2 * D_MODEL // LANES,),
        in_specs=[spec(0), spec(1), spec(2)],
        out_specs=pl.BlockSpec((D_MODEL, LANES), lambda j: (0, j)),
        out_shape=jax.ShapeDtypeStruct((D_MODEL, 2 * D_MODEL), BF16),
        name="gate_weight_repack",
    )(w_in_t, w_in_t, w_in_t)
    g1 = norm1_g[0].reshape(1, D_MODEL)
    ones64 = jnp.ones((HEAD_DIM,), F32)

    rep = lambda a: jnp.repeat(a, P, axis=0)
    gpn = jax.ShapeDtypeStruct((G * P, N), F32)
    abar_r, abar_i, bp_r, bp_i = pl.pallas_call(
        _disc_kernel, out_shape=(gpn, gpn, gpn, gpn), name="s5_discretise",
    )(rep(A_re[0]), rep(A_im[0]), rep(jnp.broadcast_to(log_dt[0].reshape(G, 1), (G, N))),
      B_re[0].transpose(0, 2, 1).reshape(G * P, N), B_im[0].transpose(0, 2, 1).reshape(G * P, N))
    abar_r, abar_i = abar_r[::P], abar_i[::P]
    bp_r, bp_i = bp_r.reshape(G, P, N), bp_i.reshape(G, P, N)
    hg = G // 2
    bp = [jnp.concatenate([_block_diag(bp_r[sl]), _block_diag(bp_i[sl])], axis=1).astype(BF16)
          for sl in (slice(0, hg), slice(hg, G))]
    c_r = C_re[0].transpose(0, 2, 1)
    c_i = C_im[0].transpose(0, 2, 1)
    cp = [jnp.concatenate([_block_diag(c_r[sl]), -_block_diag(c_i[sl])], axis=0).astype(BF16)
          for sl in (slice(0, hg), slice(hg, G))]
    ar8 = jnp.broadcast_to(abar_r.reshape(1, G * N), (SUBLANES, G * N))
    ai8 = jnp.broadcast_to(abar_i.reshape(1, G * N), (SUBLANES, G * N))

    n_in = seq // TL_IN
    u_t, q_h, qs_h, sg_t, k_n, ki_n, v_t = pl.pallas_call(
        _inproj_kernel,
        grid=(bsz, n_in),
        in_specs=[
            pl.BlockSpec((1, TL_IN, D_MODEL), lambda b, i: (b, i, 0)),
            _const_spec((1, D_MODEL)),
            _const_spec((W1_COLS, D_MODEL)),
            _const_spec((1, ATTN_WIDTH)), _const_spec((1, LANES)), _const_spec((1, LANES)),
            _const_spec((2 * LANES, 2 * LANES)),
        ],
        out_specs=[
            pl.BlockSpec((1, TL_IN, SSM_WIDTH), lambda b, i: (b, i, 0)),
            pl.BlockSpec((1, N_HEADS, HEAD_DIM, TL_IN), lambda b, i: (b, 0, 0, i)),
            pl.BlockSpec((1, IDX_HEADS, IDX_DIM, TL_IN), lambda b, i: (b, 0, 0, i)),
            pl.BlockSpec((1, SUBLANES, TL_IN), lambda b, i: (b, 0, i)),
            pl.BlockSpec((1, TL_IN, HEAD_DIM), lambda b, i: (b, i, 0)),
            pl.BlockSpec((1, TL_IN, IDX_DIM), lambda b, i: (b, i, 0)),
            pl.BlockSpec((1, TL_IN // TK, HEAD_DIM, TK), lambda b, i: (b, i, 0, 0)),
        ],
        out_shape=(
            jax.ShapeDtypeStruct((bsz, seq, SSM_WIDTH), F32),
            jax.ShapeDtypeStruct((bsz, N_HEADS, HEAD_DIM, seq), BF16),
            jax.ShapeDtypeStruct((bsz, IDX_HEADS, IDX_DIM, seq), BF16),
            jax.ShapeDtypeStruct((bsz, SUBLANES, seq), F32),
            jax.ShapeDtypeStruct((bsz, seq, HEAD_DIM), BF16),
            jax.ShapeDtypeStruct((bsz, seq, IDX_DIM), BF16),
            jax.ShapeDtypeStruct((bsz, seq // TK, HEAD_DIM, TK), BF16),
        ),
        scratch_shapes=[pltpu.VMEM((D_MODEL, W1_COLS), BF16)],
        compiler_params=cparams(dimension_semantics=("arbitrary", "arbitrary")),
        name="in_projection",
    )(x, g1, w_in_t, jnp.tile(q_norm_g[0], N_HEADS).reshape(1, ATTN_WIDTH),
      jnp.concatenate([k_norm_g[0], ones64]).reshape(1, LANES),
      jnp.concatenate([idx_k_norm_g[0], ones64]).reshape(1, LANES),
      jnp.kron(jnp.eye(2 * LANES // HEAD_DIM, dtype=BF16), jnp.ones((HEAD_DIM, HEAD_DIM), BF16)))

    rows = SUBLANES * TC_SCAN
    ps_t = pl.pallas_call(
        _s5_kernel,
        grid=(seq // TC_SCAN,),
        in_specs=[
            pl.BlockSpec((bsz, TC_SCAN, SSM_WIDTH), lambda c: (0, c, 0)),
            _const_spec((256, 2048)), _const_spec((256, 2048)),
            _const_spec((2048, 256)), _const_spec((2048, 256)),
            _const_spec((SUBLANES, G * N)), _const_spec((SUBLANES, G * N)),
            _const_spec((1, SSM_WIDTH)), _const_spec((SSM_WIDTH, SSM_WIDTH)), _const_spec((1, SSM_WIDTH)),
            _const_spec((SSM_WIDTH, D_MODEL)),
        ],
        out_specs=pl.BlockSpec((bsz, TC_SCAN, D_MODEL), lambda c: (0, c, 0)),
        out_shape=jax.ShapeDtypeStruct((bsz, seq, D_MODEL), BF16),
        scratch_shapes=[pltpu.VMEM((rows, 2 * G * N), F32), pltpu.VMEM((SUBLANES, 2 * G * N), F32)],
        compiler_params=cparams(dimension_semantics=("arbitrary",)),
        name="s5_branch",
    )(u_t, bp[0], bp[1], cp[0], cp[1], ar8, ai8,
      D_skip[0].reshape(1, SSM_WIDTH), w_glu[0].astype(BF16), b_glu[0].reshape(1, SSM_WIDTH),
      w_proj_ssm[0].astype(BF16))

    y_att = pl.pallas_call(
        _attn_kernel,
        grid=(bsz, seq // QB),
        in_specs=[
            pl.BlockSpec((1, IDX_HEADS, IDX_DIM, QB), lambda b, i: (b, 0, 0, i)),
            pl.BlockSpec((1, SUBLANES, QB), lambda b, i: (b, 0, i)),
            pl.BlockSpec((1, N_HEADS, HEAD_DIM, QB), lambda b, i: (b, 0, 0, i)),
            pl.BlockSpec((1, seq, IDX_DIM), lambda b, i: (b, 0, 0)),
            pl.BlockSpec((1, seq, HEAD_DIM), lambda b, i: (b, 0, 0)),
            pl.BlockSpec((1, seq // TK, HEAD_DIM, TK), lambda b, i: (b, 0, 0, 0)),
        ],
        out_specs=pl.BlockSpec((1, QB, ATTN_WIDTH), lambda b, i: (b, i, 0)),
        out_shape=jax.ShapeDtypeStruct((bsz, seq, ATTN_WIDTH), BF16),
        scratch_shapes=[pltpu.VMEM((seq // TK, TK, QB), F32),
                        pltpu.VMEM((N_HEADS, SUBLANES, QB), F32),
                        pltpu.VMEM((N_HEADS, SUBLANES, QB), F32),
                        pltpu.VMEM((N_HEADS, HEAD_DIM, QB), F32),
                        pltpu.VMEM((N_HEADS // 2, TK, 2 * QB), F32)],
        compiler_params=cparams(dimension_semantics=("parallel", "arbitrary")),
        name="sparse_attention",
    )(qs_h, sg_t, q_h, ki_n, k_n, v_t)

    out = pl.pallas_call(
        _ffn_kernel,
        grid=(bsz, seq // TL_FFN),
        in_specs=[
            pl.BlockSpec((1, TL_FFN, D_MODEL), lambda b, i: (b, i, 0)),
            pl.BlockSpec((1, TL_FFN, D_MODEL), lambda b, i: (b, i, 0)),
            pl.BlockSpec((1, TL_FFN, ATTN_WIDTH), lambda b, i: (b, i, 0)),
            _const_spec((1, D_MODEL)),
            _const_spec((D_MODEL, 2 * D_MODEL)),
            _const_spec((ATTN_WIDTH, D_MODEL)),
            _const_spec((D_MODEL, D_MODEL)),
            _const_spec((1, D_MODEL)),
            _const_spec((D_MODEL, D_FF)), _const_spec((D_MODEL, D_FF)), _const_spec((D_FF, D_MODEL)),
        ],
        out_specs=pl.BlockSpec((1, TL_FFN, D_MODEL), lambda b, i: (b, i, 0)),
        out_shape=jax.ShapeDtypeStruct((bsz, seq, D_MODEL), F32),
        compiler_params=cparams(dimension_semantics=("parallel", "parallel")),
        name="merge_ffn",
    )(x, ps_t, y_att, g1, w_gates, w_proj_attn[0].astype(BF16),
      w_out[0].astype(BF16), norm2_g[0].reshape(1, D_MODEL), w_ffn_gate[0].astype(BF16),
      w_ffn_up[0].astype(BF16), w_ffn_down[0].astype(BF16))
    return out
```

```python
import functools
import math

import jax
import jax.numpy as jnp
from jax import lax
from jax.experimental import pallas as pl
from jax.experimental.pallas import tpu as pltpu

F32 = jnp.float32
BF16 = jnp.bfloat16

D_MODEL = 1024
SSM_WIDTH = 512
SSM_GROUP = 16
SSM_GROUPS = 32
SSM_STATE = 64
N_HEADS = 8
HEAD_DIM = 64
ATTN_WIDTH = 512
IDX_HEADS = 4
IDX_DIM = 64
INDEX_TOPK = 256
D_FF = 2816
RMS_EPS = 1e-6

LANES = 128
SUBLANES = 8
VMEM_LIMIT = 56 * 1024 * 1024

TL_IN = 512
TC_SCAN = 64
QB = 256
TK = 256
TL_FFN = 256

W1_COLS = 1536
GATE_COL0 = 1476
GATE_RB = 64
NEG_BIG = -1e30
K_EXT = 80
Q_SHIFT_ROW = 64
NORM_UP = 1.0 + 2.0 ** -7
L_FLOOR = 2.0 ** -100
assert QB % LANES == 0


def _dot(a, b):
    return jnp.dot(a, b, preferred_element_type=F32)


def _rms(x, g):
    return x * lax.rsqrt(jnp.mean(x * x, axis=-1, keepdims=True) + RMS_EPS) * g


def _disc_kernel(are_ref, aim_ref, ldt_ref, bre_ref, bim_ref, abr_ref, abi_ref, bpr_ref, bpi_ref):
    ar = are_ref[...]
    ai = aim_ref[...]
    dt = jnp.exp(ldt_ref[...])
    mag = jnp.exp(ar * dt)
    abar_r = mag * jnp.cos(ai * dt)
    abar_i = mag * jnp.sin(ai * dt)
    den = ar * ar + ai * ai
    nr = abar_r - 1.0
    coef_r = (nr * ar + abar_i * ai) / den
    coef_i = (abar_i * ar - nr * ai) / den
    abr_ref[...] = abar_r
    abi_ref[...] = abar_i
    br = bre_ref[...]
    bi = bim_ref[...]
    bpr_ref[...] = coef_r * br - coef_i * bi
    bpi_ref[...] = coef_r * bi + coef_i * br


def _inproj_kernel(x_ref, g1_ref, w_ref, qg_ref, kg_ref, ikg_ref, seg_ref,
                   u_ref, q_ref, qs_ref, sg_ref, k_ref, ki_ref, v_ref, w_scr):
    @pl.when(jnp.logical_and(pl.program_id(0) == 0, pl.program_id(1) == 0))
    def _():
        w_scr[...] = w_ref[...].T.astype(BF16)

    x = x_ref[0]
    h = _rms(x, g1_ref[...]).astype(BF16)
    proj = _dot(h, w_scr[...])
    u_ref[0] = proj[:, 0:512]
    seg = seg_ref[...]

    def head_rms(v, g):
        sq = (v * v).astype(BF16)
        n = v.shape[1]
        w = min(n, seg.shape[0])
        ss = jnp.concatenate([_dot(sq[:, c:c + w], seg[0:w, 0:w]) for c in range(0, n, w)], axis=1)
        return v * lax.rsqrt(ss * (1.0 / HEAD_DIM) + RMS_EPS) * g

    def store_heads_transposed(ref, v, with_shift_rows=False):
        for g in range(v.shape[1] // LANES):
            tg = v[:, g * LANES:(g + 1) * LANES].T
            for e in range(2):
                th = tg[e * HEAD_DIM:(e + 1) * HEAD_DIM].astype(BF16)
                ref[0, 2 * g + e, 0:HEAD_DIM] = th
                if with_shift_rows:
                    tf = th.astype(F32)
                    nrm = jnp.sqrt(jnp.sum(tf * tf, axis=0, keepdims=True)) * NORM_UP
                    pad = (K_EXT - HEAD_DIM, th.shape[1])
                    first = lax.broadcasted_iota(jnp.int32, pad, 0) == 0
                    ref[0, 2 * g + e, HEAD_DIM:K_EXT] = jnp.where(first, -nrm, 0.0).astype(BF16)

    qscale = (HEAD_DIM ** -0.5) * math.log2(math.e)
    store_heads_transposed(q_ref, head_rms(proj[:, 512:1024], qg_ref[...]) * qscale, with_shift_rows=True)

    kv = proj[:, 1024:1152]
    kiw = proj[:, 1408:1536]
    w_scale = (IDX_HEADS ** -0.5) * (IDX_DIM ** -0.5)
    wabs = jnp.abs(kiw) * w_scale
    sshape = (LANES, IDX_HEADS * IDX_DIM)
    row = lax.broadcasted_iota(jnp.int32, sshape, 0)
    col = lax.broadcasted_iota(jnp.int32, sshape, 1)
    spread = jnp.where(row == IDX_DIM + col // IDX_DIM, 1.0, 0.0).astype(BF16)
    whi = wabs.astype(BF16)
    wlo = (wabs - whi.astype(F32)).astype(BF16)
    wrep = _dot(whi, spread) + _dot(wlo, spread)
    store_heads_transposed(qs_ref, proj[:, 1152:1408] * wrep)

    lane = lax.broadcasted_iota(jnp.int32, kv.shape, 1)
    k_ref[0] = jnp.where(lane < HEAD_DIM, head_rms(kv, kg_ref[...]), 0.0)[:, 0:K_EXT].astype(BF16)
    ki_ref[0] = head_rms(kiw, ikg_ref[...])[:, 0:IDX_DIM].astype(BF16)
    t_kv = kv.T
    t_kiw = kiw.T
    sg_ref[0] = jnp.where(t_kiw[IDX_DIM:IDX_DIM + SUBLANES] >= 0, 1.0, -1.0)
    for c in range(TL_IN // TK):
        v_ref[0, c] = t_kv[HEAD_DIM:2 * HEAD_DIM, c * TK:(c + 1) * TK].astype(BF16)


def _s5_kernel(u_ref, bplo_ref, bphi_ref, cplo_ref, cphi_ref, ar_ref, ai_ref, d_ref, wglu_ref, bglu_ref,
               wps_ref, o_ref, bu_scr, st_scr):
    @pl.when(pl.program_id(0) == 0)
    def _():
        st_scr[...] = jnp.zeros_like(st_scr)

    u = jnp.transpose(u_ref[...], (1, 0, 2)).reshape(SUBLANES * TC_SCAN, SSM_WIDTH)
    ub = u.astype(BF16)
    half_w = 2 * (SSM_GROUPS // 2) * SSM_STATE
    n_re = half_w // 2

    def scan_half(h):
        c0 = h * half_w
        a_r = ar_ref[:, h * n_re:(h + 1) * n_re]
        a_i = ai_ref[:, h * n_re:(h + 1) * n_re]
        sr = st_scr[:, c0:c0 + n_re]
        si = st_scr[:, c0 + n_re:c0 + half_w]
        for t in range(TC_SCAN):
            rows = slice(t * SUBLANES, (t + 1) * SUBLANES)
            br = bu_scr[rows, c0:c0 + n_re]
            bi = bu_scr[rows, c0 + n_re:c0 + half_w]
            sr, si = a_r * sr - a_i * si + br, a_r * si + a_i * sr + bi
            bu_scr[rows, c0:c0 + n_re] = sr
            bu_scr[rows, c0 + n_re:c0 + half_w] = si
        st_scr[:, c0:c0 + n_re] = sr
        st_scr[:, c0 + n_re:c0 + half_w] = si

    bu_scr[:, 0:half_w] = _dot(ub[:, 0:256], bplo_ref[...])
    bu_scr[:, half_w:2 * half_w] = _dot(ub[:, 256:512], bphi_ref[...])
    scan_half(0)
    y_lo = _dot(bu_scr[:, 0:half_w].astype(BF16), cplo_ref[...])
    scan_half(1)
    y_hi = _dot(bu_scr[:, half_w:2 * half_w].astype(BF16), cphi_ref[...])
    y = jnp.concatenate([y_lo, y_hi], axis=1)
    y = jax.nn.gelu(y + d_ref[...] * u)
    z = _dot(y.astype(BF16), wglu_ref[...]) + bglu_ref[...]
    y = y * jax.nn.sigmoid(z)
    ps = _dot(y.astype(BF16), wps_ref[...])
    o_ref[...] = jnp.transpose(ps.reshape(TC_SCAN, SUBLANES, D_MODEL), (1, 0, 2)).astype(BF16)


def _key_to_f32(key):
    bits = jnp.where(key < 0, key & jnp.int32(0x7FFFFFFF), ~key)
    return pltpu.bitcast(bits, F32)


def _attn_kernel(qs_ref, sg_ref, q_ref, ki_ref, k_ref, vt_ref, o_ref,
                 s_scr, m_scr, l_scr, acc_scr, sp_scr, qmod_scr, kmax_scr):
    i = pl.program_id(1)
    n_tiles = (i * QB + QB + TK - 1) // TK
    nv = TK // SUBLANES
    shape3 = (nv, SUBLANES, QB)
    k_in_tile = (lax.broadcasted_iota(jnp.int32, shape3, 0) * SUBLANES
                 + lax.broadcasted_iota(jnp.int32, shape3, 1))
    q_pos = i * QB + lax.broadcasted_iota(jnp.int32, shape3, 2)
    ksel = float(INDEX_TOPK)

    def all_sublanes(a, op):
        for sh in (4, 2, 1):
            a = op(a, pltpu.roll(a, sh, 0))
        return a

    sg = sg_ref[0]

    def idx_body(j, carry):
        kt = ki_ref[0, pl.ds(pl.multiple_of(j * TK, TK), TK), :]
        acc = jnp.zeros((TK, QB), F32)
        for pr in range(IDX_HEADS // 2):
            x = _dot(kt, jnp.concatenate([qs_ref[0, 2 * pr], qs_ref[0, 2 * pr + 1]], axis=1))
            for e in range(2):
                hd = 2 * pr + e
                acc = acc + sg[hd:hd + 1, :] * jnp.maximum(x[:, e * QB:(e + 1) * QB], 0.0)
        vis = j * TK + k_in_tile <= q_pos
        s_scr[j] = jnp.where(vis, acc.reshape(shape3), -jnp.inf).reshape(TK, QB)
        return carry

    lax.fori_loop(0, n_tiles, idx_body, 0)

    def count(pred):
        def body(j, acc):
            hit = jnp.where(pred(s_scr[j].reshape(shape3), j), 1.0, 0.0)
            return acc + jnp.sum(hit.reshape(nv // 4, 4, SUBLANES, QB), axis=0)
        acc = lax.fori_loop(0, n_tiles, body, jnp.zeros((4, SUBLANES, QB), F32))
        return all_sublanes(jnp.sum(acc, axis=0), jnp.add)

    def bit_body(it, key):
        cand = key | lax.shift_left(jnp.int32(1), 31 - it)
        tc = _key_to_f32(cand)
        cnt = count(lambda s, j: s >= tc)
        below_ninf = lax.shift_right_logical(cand, 23) == 0
        return jnp.where(below_ninf | (cnt >= ksel), cand, key)

    key = lax.fori_loop(0, 32, bit_body, jnp.zeros((SUBLANES, QB), jnp.int32))
    thr = _key_to_f32(key)
    cnt_ge = count(lambda s, j: s >= thr)
    has_excess = jnp.max(cnt_ge) > ksel

    def write_bias(select):
        def body(j, carry):
            s = s_scr[j].reshape(shape3)
            vis = j * TK + k_in_tile <= q_pos
            bias = jnp.where(vis, jnp.where(select(s, j), 0.0, -jnp.inf), -jnp.inf)
            s_scr[j] = bias.reshape(TK, QB)
            return carry
        lax.fori_loop(0, n_tiles, body, 0)

    @pl.when(jnp.logical_not(has_excess))
    def _():
        write_bias(lambda s, j: s >= thr)

    @pl.when(has_excess)
    def _():
        cnt_gt = count(lambda s, j: s > thr)
        need = ksel - cnt_gt

        def jbit_body(it, jkey):
            cand = jkey | lax.shift_left(jnp.int32(1), 10 - it)
            cnt = count(lambda s, j: (s == thr) & (j * TK + k_in_tile < cand))
            return jnp.where(cnt < need, cand, jkey)

        jkey = lax.fori_loop(0, 11, jbit_body, jnp.zeros((SUBLANES, QB), jnp.int32))
        write_bias(lambda s, j: (s > thr) | ((s == thr) & (j * TK + k_in_tile <= jkey)))

    lane_k = lax.broadcasted_iota(jnp.int32, (TK, K_EXT), 1)

    def attend(get_q, k_fill):
        l_scr[...] = jnp.zeros(l_scr.shape, F32)
        acc_scr[...] = jnp.zeros(acc_scr.shape, F32)

        def body(j, carry):
            kt = k_ref[0, pl.ds(pl.multiple_of(j * TK, TK), TK), :]
            kt = jnp.where(lane_k == Q_SHIFT_ROW, jnp.full((TK, K_EXT), k_fill, F32).astype(BF16), kt)
            vt = vt_ref[0, j]
            bias = s_scr[j].reshape(shape3)
            for pr in range(N_HEADS // 2):
                sp_scr[pr] = _dot(kt, jnp.concatenate([get_q(2 * pr), get_q(2 * pr + 1)], axis=1))
            for pr in range(N_HEADS // 2):
                for e in range(2):
                    hd = 2 * pr + e
                    p = jnp.exp2(sp_scr[pr, :, e * QB:(e + 1) * QB].reshape(shape3) + bias)
                    l_scr[hd] = l_scr[hd] + jnp.sum(p, axis=0)
                    acc_scr[hd] = acc_scr[hd] + _dot(vt, p.reshape(TK, QB).astype(BF16))
            return carry

        lax.fori_loop(0, n_tiles, body, 0)

    @pl.when(i == 0)
    def _():
        kf = k_ref[0].astype(F32)
        kmax_scr[0] = jnp.max(jnp.sqrt(jnp.sum(kf * kf, axis=1, keepdims=True))) * NORM_UP

    attend(lambda hd: q_ref[0, hd], kmax_scr[0])

    l_min = all_sublanes(l_scr[0], jnp.add)
    for hd in range(1, N_HEADS):
        l_min = jnp.minimum(l_min, all_sublanes(l_scr[hd], jnp.add))
    underflow = jnp.logical_not(jnp.min(l_min) >= L_FLOOR)

    @pl.when(underflow)
    def _():
        row_q = lax.broadcasted_iota(jnp.int32, (K_EXT, QB), 0)
        m_scr[...] = jnp.full(m_scr.shape, NEG_BIG, F32)

        def max_body(j, carry):
            kt = k_ref[0, pl.ds(pl.multiple_of(j * TK, TK), TK), :]
            kt = jnp.where(lane_k == Q_SHIFT_ROW, jnp.zeros((TK, K_EXT), BF16), kt)
            bias = s_scr[j].reshape(shape3)
            for hd in range(N_HEADS):
                s = _dot(kt, q_ref[0, hd]).reshape(shape3) + bias
                m_scr[hd] = jnp.maximum(m_scr[hd], all_sublanes(jnp.max(s, axis=0), jnp.maximum))
            return carry

        lax.fori_loop(0, n_tiles, max_body, 0)
        for hd in range(N_HEADS):
            shift = jnp.broadcast_to(-m_scr[hd][0:1, :], (K_EXT, QB)).astype(BF16)
            qmod_scr[hd] = jnp.where(row_q == Q_SHIFT_ROW, shift, q_ref[0, hd])
        attend(lambda hd: qmod_scr[hd], 1.0)

    outs = []
    for hd in range(N_HEADS):
        l = all_sublanes(l_scr[hd], jnp.add)
        o = acc_scr[hd].reshape(HEAD_DIM // SUBLANES, SUBLANES, QB) / l
        outs.append(o.reshape(HEAD_DIM, QB))
    o_ref[0] = jnp.concatenate(outs, axis=0).T.astype(BF16)


def _ffn_kernel(x_ref, ps_ref, ya_ref, g1_ref, wgate_ref, wpa_ref, wo_ref, g2_ref, wfg_ref, wfu_ref, wfd_ref,
                o_ref):
    x = x_ref[0]
    h = _rms(x, g1_ref[...]).astype(BF16)
    gates = _dot(h, wgate_ref[...])
    pa = _dot(ya_ref[0], wpa_ref[...])
    merged = (jax.nn.sigmoid(gates[:, 0:D_MODEL]) * ps_ref[0].astype(F32)
              + jax.nn.sigmoid(gates[:, D_MODEL:2 * D_MODEL]) * pa)
    x1 = x + _dot(merged.astype(BF16), wo_ref[...])
    h2 = _rms(x1, g2_ref[...]).astype(BF16)
    hid = jax.nn.silu(_dot(h2, wfg_ref[...])) * _dot(h2, wfu_ref[...])
    o_ref[0] = x1 + _dot(hid.astype(BF16), wfd_ref[...])


def _gate_repack_kernel(a_ref, b_ref, c_ref, o_ref):
    off = GATE_COL0 % GATE_RB
    rows = jnp.concatenate([a_ref[off:], b_ref[...], c_ref[:off]], axis=0)
    o_ref[...] = rows.T.astype(BF16)


def _const_spec(shape):
    nd = len(shape)
    return pl.BlockSpec(shape, lambda *_: (0,) * nd, pipeline_mode=pl.Buffered(1))


def _block_diag(blocks):
    g, r, c = blocks.shape
    eye = jnp.eye(g, dtype=blocks.dtype)
    return jnp.einsum('grc,gh->grhc', blocks, eye).reshape(g * r, g * c)


def kernel(x, norm1_g, w_in, A_re, A_im, log_dt, B_re, B_im, C_re, C_im, D_skip, w_glu, b_glu, q_norm_g, k_norm_g,
           idx_k_norm_g, w_proj_ssm, w_proj_attn, w_out, norm2_g, w_ffn_gate, w_ffn_up, w_ffn_down):
    bsz, seq, _ = x.shape
    assert x.shape == (8, 2048, D_MODEL) and w_in.shape[0] == 1
    G, N, P = SSM_GROUPS, SSM_STATE, SSM_GROUP
    cparams = functools.partial(pltpu.CompilerParams, vmem_limit_bytes=VMEM_LIMIT)

    w_in_t = w_in[0].T
    gblk = GATE_COL0 // GATE_RB
    spec = lambda k: pl.BlockSpec((GATE_RB, D_MODEL), lambda j: (gblk + 2 * j + k, 0))
    w_gates = pl.pallas_call(
        _gate_repack_kernel,
        grid=(2 * D_MODEL // LANES,),
        in_specs=[spec(0), spec(1), spec(2)],
        out_specs=pl.BlockSpec((D_MODEL, LANES), lambda j: (0, j)),
        out_shape=jax.ShapeDtypeStruct((D_MODEL, 2 * D_MODEL), BF16),
        name="gate_weight_repack",
    )(w_in_t, w_in_t, w_in_t)
    g1 = norm1_g[0].reshape(1, D_MODEL)
    ones64 = jnp.ones((HEAD_DIM,), F32)

    rep = lambda a: jnp.repeat(a, P, axis=0)
    gpn = jax.ShapeDtypeStruct((G * P, N), F32)
    abar_r, abar_i, bp_r, bp_i = pl.pallas_call(
        _disc_kernel, out_shape=(gpn, gpn, gpn, gpn), name="s5_discretise",
    )(rep(A_re[0]), rep(A_im[0]), rep(jnp.broadcast_to(log_dt[0].reshape(G, 1), (G, N))),
      B_re[0].transpose(0, 2, 1).reshape(G * P, N), B_im[0].transpose(0, 2, 1).reshape(G * P, N))
    abar_r, abar_i = abar_r[::P], abar_i[::P]
    bp_r, bp_i = bp_r.reshape(G, P, N), bp_i.reshape(G, P, N)
    hg = G // 2
    bp = [jnp.concatenate([_block_diag(bp_r[sl]), _block_diag(bp_i[sl])], axis=1).astype(BF16)
          for sl in (slice(0, hg), slice(hg, G))]
    c_r = C_re[0].transpose(0, 2, 1)
    c_i = C_im[0].transpose(0, 2, 1)
    cp = [jnp.concatenate([_block_diag(c_r[sl]), -_block_diag(c_i[sl])], axis=0).astype(BF16)
          for sl in (slice(0, hg), slice(hg, G))]
    ar8 = jnp.broadcast_to(abar_r.reshape(1, G * N), (SUBLANES, G * N))
    ai8 = jnp.broadcast_to(abar_i.reshape(1, G * N), (SUBLANES, G * N))

    n_in = seq // TL_IN
    u_t, q_h, qs_h, sg_t, k_n, ki_n, v_t = pl.pallas_call(
        _inproj_kernel,
        grid=(bsz, n_in),
        in_specs=[
            pl.BlockSpec((1, TL_IN, D_MODEL), lambda b, i: (b, i, 0)),
            _const_spec((1, D_MODEL)),
            _const_spec((W1_COLS, D_MODEL)),
            _const_spec((1, ATTN_WIDTH)), _const_spec((1, LANES)), _const_spec((1, LANES)),
            _const_spec((2 * LANES, 2 * LANES)),
        ],
        out_specs=[
            pl.BlockSpec((1, TL_IN, SSM_WIDTH), lambda b, i: (b, i, 0)),
            pl.BlockSpec((1, N_HEADS, K_EXT, TL_IN), lambda b, i: (b, 0, 0, i)),
            pl.BlockSpec((1, IDX_HEADS, IDX_DIM, TL_IN), lambda b, i: (b, 0, 0, i)),
            pl.BlockSpec((1, SUBLANES, TL_IN), lambda b, i: (b, 0, i)),
            pl.BlockSpec((1, TL_IN, K_EXT), lambda b, i: (b, i, 0)),
            pl.BlockSpec((1, TL_IN, IDX_DIM), lambda b, i: (b, i, 0)),
            pl.BlockSpec((1, TL_IN // TK, HEAD_DIM, TK), lambda b, i: (b, i, 0, 0)),
        ],
        out_shape=(
            jax.ShapeDtypeStruct((bsz, seq, SSM_WIDTH), F32),
            jax.ShapeDtypeStruct((bsz, N_HEADS, K_EXT, seq), BF16),
            jax.ShapeDtypeStruct((bsz, IDX_HEADS, IDX_DIM, seq), BF16),
            jax.ShapeDtypeStruct((bsz, SUBLANES, seq), F32),
            jax.ShapeDtypeStruct((bsz, seq, K_EXT), BF16),
            jax.ShapeDtypeStruct((bsz, seq, IDX_DIM), BF16),
            jax.ShapeDtypeStruct((bsz, seq // TK, HEAD_DIM, TK), BF16),
        ),
        scratch_shapes=[pltpu.VMEM((D_MODEL, W1_COLS), BF16)],
        compiler_params=cparams(dimension_semantics=("arbitrary", "arbitrary")),
        name="in_projection",
    )(x, g1, w_in_t, jnp.tile(q_norm_g[0], N_HEADS).reshape(1, ATTN_WIDTH),
      jnp.concatenate([k_norm_g[0], ones64]).reshape(1, LANES),
      jnp.concatenate([idx_k_norm_g[0], ones64]).reshape(1, LANES),
      jnp.kron(jnp.eye(2 * LANES // HEAD_DIM, dtype=BF16), jnp.ones((HEAD_DIM, HEAD_DIM), BF16)))

    rows = SUBLANES * TC_SCAN
    ps_t = pl.pallas_call(
        _s5_kernel,
        grid=(seq // TC_SCAN,),
        in_specs=[
            pl.BlockSpec((bsz, TC_SCAN, SSM_WIDTH), lambda c: (0, c, 0)),
            _const_spec((256, 2048)), _const_spec((256, 2048)),
            _const_spec((2048, 256)), _const_spec((2048, 256)),
            _const_spec((SUBLANES, G * N)), _const_spec((SUBLANES, G * N)),
            _const_spec((1, SSM_WIDTH)), _const_spec((SSM_WIDTH, SSM_WIDTH)), _const_spec((1, SSM_WIDTH)),
            _const_spec((SSM_WIDTH, D_MODEL)),
        ],
        out_specs=pl.BlockSpec((bsz, TC_SCAN, D_MODEL), lambda c: (0, c, 0)),
        out_shape=jax.ShapeDtypeStruct((bsz, seq, D_MODEL), BF16),
        scratch_shapes=[pltpu.VMEM((rows, 2 * G * N), F32), pltpu.VMEM((SUBLANES, 2 * G * N), F32)],
        compiler_params=cparams(dimension_semantics=("arbitrary",)),
        name="s5_branch",
    )(u_t, bp[0], bp[1], cp[0], cp[1], ar8, ai8,
      D_skip[0].reshape(1, SSM_WIDTH), w_glu[0].astype(BF16), b_glu[0].reshape(1, SSM_WIDTH),
      w_proj_ssm[0].astype(BF16))

    y_att = pl.pallas_call(
        _attn_kernel,
        grid=(bsz, seq // QB),
        in_specs=[
            pl.BlockSpec((1, IDX_HEADS, IDX_DIM, QB), lambda b, i: (b, 0, 0, i)),
            pl.BlockSpec((1, SUBLANES, QB), lambda b, i: (b, 0, i)),
            pl.BlockSpec((1, N_HEADS, K_EXT, QB), lambda b, i: (b, 0, 0, i)),
            pl.BlockSpec((1, seq, IDX_DIM), lambda b, i: (b, 0, 0)),
            pl.BlockSpec((1, seq, K_EXT), lambda b, i: (b, 0, 0)),
            pl.BlockSpec((1, seq // TK, HEAD_DIM, TK), lambda b, i: (b, 0, 0, 0)),
        ],
        out_specs=pl.BlockSpec((1, QB, ATTN_WIDTH), lambda b, i: (b, i, 0)),
        out_shape=jax.ShapeDtypeStruct((bsz, seq, ATTN_WIDTH), BF16),
        scratch_shapes=[pltpu.VMEM((seq // TK, TK, QB), F32),
                        pltpu.VMEM((N_HEADS, SUBLANES, QB), F32),
                        pltpu.VMEM((N_HEADS, SUBLANES, QB), F32),
                        pltpu.VMEM((N_HEADS, HEAD_DIM, QB), F32),
                        pltpu.VMEM((N_HEADS // 2, TK, 2 * QB), F32),
                        pltpu.VMEM((N_HEADS, K_EXT, QB), BF16),
                        pltpu.SMEM((1,), F32)],
        compiler_params=cparams(dimension_semantics=("parallel", "arbitrary")),
        name="sparse_attention",
    )(qs_h, sg_t, q_h, ki_n, k_n, v_t)

    out = pl.pallas_call(
        _ffn_kernel,
        grid=(bsz, seq // TL_FFN),
        in_specs=[
            pl.BlockSpec((1, TL_FFN, D_MODEL), lambda b, i: (b, i, 0)),
            pl.BlockSpec((1, TL_FFN, D_MODEL), lambda b, i: (b, i, 0)),
            pl.BlockSpec((1, TL_FFN, ATTN_WIDTH), lambda b, i: (b, i, 0)),
            _const_spec((1, D_MODEL)),
            _const_spec((D_MODEL, 2 * D_MODEL)),
            _const_spec((ATTN_WIDTH, D_MODEL)),
            _const_spec((D_MODEL, D_MODEL)),
            _const_spec((1, D_MODEL)),
            _const_spec((D_MODEL, D_FF)), _const_spec((D_MODEL, D_FF)), _const_spec((D_FF, D_MODEL)),
        ],
        out_specs=pl.BlockSpec((1, TL_FFN, D_MODEL), lambda b, i: (b, i, 0)),
        out_shape=jax.ShapeDtypeStruct((bsz, seq, D_MODEL), F32),
        compiler_params=cparams(dimension_semantics=("parallel", "parallel")),
        name="merge_ffn",
    )(x, ps_t, y_att, g1, w_gates, w_proj_attn[0].astype(BF16),
      w_out[0].astype(BF16), norm2_g[0].reshape(1, D_MODEL), w_ffn_gate[0].astype(BF16),
      w_ffn_up[0].astype(BF16), w_ffn_down[0].astype(BF16))
    return out
```

```python
import functools
import math

import jax
import jax.numpy as jnp
from jax import lax
from jax.experimental import pallas as pl
from jax.experimental.pallas import tpu as pltpu

F32 = jnp.float32
BF16 = jnp.bfloat16

D_MODEL = 1024
SSM_WIDTH = 512
SSM_GROUP = 16
SSM_GROUPS = 32
SSM_STATE = 64
N_HEADS = 8
HEAD_DIM = 64
ATTN_WIDTH = 512
IDX_HEADS = 4
IDX_DIM = 64
INDEX_TOPK = 256
D_FF = 2816
RMS_EPS = 1e-6

LANES = 128
SUBLANES = 8
VMEM_LIMIT = 56 * 1024 * 1024

TL_IN = 512
TC_SCAN = 64
QB = 256
TK = 256
TL_FFN = 256

W1_COLS = 1536
GATE_COL0 = 1476
GATE_RB = 64
NEG_BIG = -1e30
K_EXT = 80
Q_SHIFT_ROW = 64
NORM_UP = 1.0 + 2.0 ** -7
L_FLOOR = 2.0 ** -100
assert QB % LANES == 0


def _dot(a, b):
    return jnp.dot(a, b, preferred_element_type=F32)


def _rms(x, g):
    return x * lax.rsqrt(jnp.mean(x * x, axis=-1, keepdims=True) + RMS_EPS) * g


def _disc_kernel(are_ref, aim_ref, ldt_ref, bre_ref, bim_ref, abr_ref, abi_ref, bpr_ref, bpi_ref):
    ar = are_ref[...]
    ai = aim_ref[...]
    dt = jnp.exp(ldt_ref[...])
    mag = jnp.exp(ar * dt)
    abar_r = mag * jnp.cos(ai * dt)
    abar_i = mag * jnp.sin(ai * dt)
    den = ar * ar + ai * ai
    nr = abar_r - 1.0
    coef_r = (nr * ar + abar_i * ai) / den
    coef_i = (abar_i * ar - nr * ai) / den
    abr_ref[...] = abar_r
    abi_ref[...] = abar_i
    br = bre_ref[...]
    bi = bim_ref[...]
    bpr_ref[...] = coef_r * br - coef_i * bi
    bpi_ref[...] = coef_r * bi + coef_i * br


def _inproj_kernel(x_ref, g1_ref, w_ref, qg_ref, kg_ref, ikg_ref, seg_ref,
                   u_ref, q_ref, qs_ref, sg_ref, k_ref, ki_ref, v_ref, w_scr):
    @pl.when(jnp.logical_and(pl.program_id(0) == 0, pl.program_id(1) == 0))
    def _():
        w_scr[...] = w_ref[...].T.astype(BF16)

    x = x_ref[0]
    h = _rms(x, g1_ref[...]).astype(BF16)
    proj = _dot(h, w_scr[...])
    u_ref[0] = proj[:, 0:512]
    seg = seg_ref[...]

    def head_rms(v, g):
        sq = (v * v).astype(BF16)
        n = v.shape[1]
        w = min(n, seg.shape[0])
        ss = jnp.concatenate([_dot(sq[:, c:c + w], seg[0:w, 0:w]) for c in range(0, n, w)], axis=1)
        return v * lax.rsqrt(ss * (1.0 / HEAD_DIM) + RMS_EPS) * g

    def store_heads_transposed(ref, v, with_shift_rows=False):
        for g in range(v.shape[1] // LANES):
            tg = v[:, g * LANES:(g + 1) * LANES].T
            for e in range(2):
                th = tg[e * HEAD_DIM:(e + 1) * HEAD_DIM].astype(BF16)
                ref[0, 2 * g + e, 0:HEAD_DIM] = th
                if with_shift_rows:
                    tf = th.astype(F32)
                    nrm = jnp.sqrt(jnp.sum(tf * tf, axis=0, keepdims=True)) * NORM_UP
                    pad = (K_EXT - HEAD_DIM, th.shape[1])
                    first = lax.broadcasted_iota(jnp.int32, pad, 0) == 0
                    ref[0, 2 * g + e, HEAD_DIM:K_EXT] = jnp.where(first, -nrm, 0.0).astype(BF16)

    qscale = (HEAD_DIM ** -0.5) * math.log2(math.e)
    store_heads_transposed(q_ref, head_rms(proj[:, 512:1024], qg_ref[...]) * qscale, with_shift_rows=True)

    kv = proj[:, 1024:1152]
    kiw = proj[:, 1408:1536]
    w_scale = (IDX_HEADS ** -0.5) * (IDX_DIM ** -0.5)
    wabs = jnp.abs(kiw) * w_scale
    sshape = (LANES, IDX_HEADS * IDX_DIM)
    row = lax.broadcasted_iota(jnp.int32, sshape, 0)
    col = lax.broadcasted_iota(jnp.int32, sshape, 1)
    spread = jnp.where(row == IDX_DIM + col // IDX_DIM, 1.0, 0.0).astype(BF16)
    whi = wabs.astype(BF16)
    wlo = (wabs - whi.astype(F32)).astype(BF16)
    wrep = _dot(whi, spread) + _dot(wlo, spread)
    store_heads_transposed(qs_ref, proj[:, 1152:1408] * wrep)

    lane = lax.broadcasted_iota(jnp.int32, kv.shape, 1)
    k_ref[0] = jnp.where(lane < HEAD_DIM, head_rms(kv, kg_ref[...]), 0.0)[:, 0:K_EXT].astype(BF16)
    ki_ref[0] = head_rms(kiw, ikg_ref[...])[:, 0:IDX_DIM].astype(BF16)
    t_kv = kv.T
    t_kiw = kiw.T
    sg_ref[0] = jnp.where(t_kiw[IDX_DIM:IDX_DIM + SUBLANES] >= 0, 1.0, -1.0)
    for c in range(TL_IN // TK):
        v_ref[0, c] = t_kv[HEAD_DIM:2 * HEAD_DIM, c * TK:(c + 1) * TK].astype(BF16)


def _s5_kernel(u_ref, bplo_ref, bphi_ref, cplo_ref, cphi_ref, ar_ref, ai_ref, d_ref, wglu_ref, bglu_ref,
               wps_ref, o_ref, bu_scr, st_scr):
    @pl.when(pl.program_id(0) == 0)
    def _():
        st_scr[...] = jnp.zeros_like(st_scr)

    u = jnp.transpose(u_ref[...], (1, 0, 2)).reshape(SUBLANES * TC_SCAN, SSM_WIDTH)
    ub = u.astype(BF16)
    half_w = 2 * (SSM_GROUPS // 2) * SSM_STATE
    n_re = half_w // 2

    def scan_half(h):
        c0 = h * half_w
        a_r = ar_ref[:, h * n_re:(h + 1) * n_re]
        a_i = ai_ref[:, h * n_re:(h + 1) * n_re]
        sr = st_scr[:, c0:c0 + n_re]
        si = st_scr[:, c0 + n_re:c0 + half_w]
        for t in range(TC_SCAN):
            rows = slice(t * SUBLANES, (t + 1) * SUBLANES)
            br = bu_scr[rows, c0:c0 + n_re]
            bi = bu_scr[rows, c0 + n_re:c0 + half_w]
            sr, si = a_r * sr - a_i * si + br, a_r * si + a_i * sr + bi
            bu_scr[rows, c0:c0 + n_re] = sr
            bu_scr[rows, c0 + n_re:c0 + half_w] = si
        st_scr[:, c0:c0 + n_re] = sr
        st_scr[:, c0 + n_re:c0 + half_w] = si

    bu_scr[:, 0:half_w] = _dot(ub[:, 0:256], bplo_ref[...])
    bu_scr[:, half_w:2 * half_w] = _dot(ub[:, 256:512], bphi_ref[...])
    scan_half(0)
    y_lo = _dot(bu_scr[:, 0:half_w].astype(BF16), cplo_ref[...])
    scan_half(1)
    y_hi = _dot(bu_scr[:, half_w:2 * half_w].astype(BF16), cphi_ref[...])
    y = jnp.concatenate([y_lo, y_hi], axis=1)
    y = jax.nn.gelu(y + d_ref[...] * u)
    z = _dot(y.astype(BF16), wglu_ref[...]) + bglu_ref[...]
    y = y * jax.nn.sigmoid(z)
    ps = _dot(y.astype(BF16), wps_ref[...])
    o_ref[...] = jnp.transpose(ps.reshape(TC_SCAN, SUBLANES, D_MODEL), (1, 0, 2)).astype(BF16)


def _key_to_f32(key):
    bits = jnp.where(key < 0, key & jnp.int32(0x7FFFFFFF), ~key)
    return pltpu.bitcast(bits, F32)


def _attn_kernel(qs_ref, sg_ref, q_ref, ki_ref, k_ref, vt_ref, o_ref,
                 s_scr, m_scr, l_scr, acc_scr, sp_scr, qmod_scr, kmax_scr):
    i = pl.program_id(1)
    n_tiles = (i * QB + QB + TK - 1) // TK
    nv = TK // SUBLANES
    shape3 = (nv, SUBLANES, QB)
    k_in_tile = (lax.broadcasted_iota(jnp.int32, shape3, 0) * SUBLANES
                 + lax.broadcasted_iota(jnp.int32, shape3, 1))
    q_pos = i * QB + lax.broadcasted_iota(jnp.int32, shape3, 2)
    ksel = float(INDEX_TOPK)

    def all_sublanes(a, op):
        for sh in (4, 2, 1):
            a = op(a, pltpu.roll(a, sh, 0))
        return a

    sg = sg_ref[0]

    def idx_body(j, carry):
        kt = ki_ref[0, pl.ds(pl.multiple_of(j * TK, TK), TK), :]
        acc = jnp.zeros((TK, QB), F32)
        for pr in range(IDX_HEADS // 2):
            x = _dot(kt, jnp.concatenate([qs_ref[0, 2 * pr], qs_ref[0, 2 * pr + 1]], axis=1))
            for e in range(2):
                hd = 2 * pr + e
                acc = acc + sg[hd:hd + 1, :] * jnp.maximum(x[:, e * QB:(e + 1) * QB], 0.0)
        vis = j * TK + k_in_tile <= q_pos
        s_scr[j] = jnp.where(vis, acc.reshape(shape3), -jnp.inf).reshape(TK, QB)
        return carry

    def for_each_tile(body):
        def pair(jj, carry):
            body(2 * jj, carry)
            return body(2 * jj + 1, carry)
        lax.fori_loop(0, n_tiles // 2, pair, 0)

        @pl.when(n_tiles % 2 == 1)
        def _():
            body(n_tiles - 1, 0)

    for_each_tile(idx_body)

    def count(pred):
        def body(j, acc):
            hit = jnp.where(pred(s_scr[j].reshape(shape3), j), 1.0, 0.0)
            return acc + jnp.sum(hit.reshape(nv // 4, 4, SUBLANES, QB), axis=0)
        acc = lax.fori_loop(0, n_tiles, body, jnp.zeros((4, SUBLANES, QB), F32))
        return all_sublanes(jnp.sum(acc, axis=0), jnp.add)

    def bit_body(it, key):
        cand = key | lax.shift_left(jnp.int32(1), 31 - it)
        tc = _key_to_f32(cand)
        cnt = count(lambda s, j: s >= tc)
        below_ninf = lax.shift_right_logical(cand, 23) == 0
        return jnp.where(below_ninf | (cnt >= ksel), cand, key)

    key = lax.fori_loop(0, 32, bit_body, jnp.zeros((SUBLANES, QB), jnp.int32))
    thr = _key_to_f32(key)
    cnt_ge = count(lambda s, j: s >= thr)
    has_excess = jnp.max(cnt_ge) > ksel

    def write_bias(select):
        def body(j, carry):
            s = s_scr[j].reshape(shape3)
            vis = j * TK + k_in_tile <= q_pos
            bias = jnp.where(vis, jnp.where(select(s, j), 0.0, -jnp.inf), -jnp.inf)
            s_scr[j] = bias.reshape(TK, QB)
            return carry
        lax.fori_loop(0, n_tiles, body, 0)

    @pl.when(jnp.logical_not(has_excess))
    def _():
        write_bias(lambda s, j: s >= thr)

    @pl.when(has_excess)
    def _():
        cnt_gt = count(lambda s, j: s > thr)
        need = ksel - cnt_gt

        def jbit_body(it, jkey):
            cand = jkey | lax.shift_left(jnp.int32(1), 10 - it)
            cnt = count(lambda s, j: (s == thr) & (j * TK + k_in_tile < cand))
            return jnp.where(cnt < need, cand, jkey)

        jkey = lax.fori_loop(0, 11, jbit_body, jnp.zeros((SUBLANES, QB), jnp.int32))
        write_bias(lambda s, j: (s > thr) | ((s == thr) & (j * TK + k_in_tile <= jkey)))

    lane_k = lax.broadcasted_iota(jnp.int32, (TK, K_EXT), 1)

    def attend(get_q, k_fill):
        l_scr[...] = jnp.zeros(l_scr.shape, F32)
        acc_scr[...] = jnp.zeros(acc_scr.shape, F32)

        def body(j, carry):
            kt = k_ref[0, pl.ds(pl.multiple_of(j * TK, TK), TK), :]
            kt = jnp.where(lane_k == Q_SHIFT_ROW, jnp.full((TK, K_EXT), k_fill, F32).astype(BF16), kt)
            vt = vt_ref[0, j]
            bias = s_scr[j].reshape(shape3)
            for pr in range(N_HEADS // 2):
                sp_scr[pr] = _dot(kt, jnp.concatenate([get_q(2 * pr), get_q(2 * pr + 1)], axis=1))
            for pr in range(N_HEADS // 2):
                for e in range(2):
                    hd = 2 * pr + e
                    p = jnp.exp2(sp_scr[pr, :, e * QB:(e + 1) * QB].reshape(shape3) + bias)
                    l_scr[hd] = l_scr[hd] + jnp.sum(p, axis=0)
                    acc_scr[hd] = acc_scr[hd] + _dot(vt, p.reshape(TK, QB).astype(BF16))
            return carry

        for_each_tile(body)

    @pl.when(i == 0)
    def _():
        kf = k_ref[0].astype(F32)
        kmax_scr[0] = jnp.max(jnp.sqrt(jnp.sum(kf * kf, axis=1, keepdims=True))) * NORM_UP

    attend(lambda hd: q_ref[0, hd], kmax_scr[0])

    l_min = all_sublanes(l_scr[0], jnp.add)
    for hd in range(1, N_HEADS):
        l_min = jnp.minimum(l_min, all_sublanes(l_scr[hd], jnp.add))
    underflow = jnp.logical_not(jnp.min(l_min) >= L_FLOOR)

    @pl.when(underflow)
    def _():
        row_q = lax.broadcasted_iota(jnp.int32, (K_EXT, QB), 0)
        m_scr[...] = jnp.full(m_scr.shape, NEG_BIG, F32)

        def max_body(j, carry):
            kt = k_ref[0, pl.ds(pl.multiple_of(j * TK, TK), TK), :]
            kt = jnp.where(lane_k == Q_SHIFT_ROW, jnp.zeros((TK, K_EXT), BF16), kt)
            bias = s_scr[j].reshape(shape3)
            for hd in range(N_HEADS):
                s = _dot(kt, q_ref[0, hd]).reshape(shape3) + bias
                m_scr[hd] = jnp.maximum(m_scr[hd], all_sublanes(jnp.max(s, axis=0), jnp.maximum))
            return carry

        lax.fori_loop(0, n_tiles, max_body, 0)
        for hd in range(N_HEADS):
            shift = jnp.broadcast_to(-m_scr[hd][0:1, :], (K_EXT, QB)).astype(BF16)
            qmod_scr[hd] = jnp.where(row_q == Q_SHIFT_ROW, shift, q_ref[0, hd])
        attend(lambda hd: qmod_scr[hd], 1.0)

    outs = []
    for hd in range(N_HEADS):
        l = all_sublanes(l_scr[hd], jnp.add)
        o = acc_scr[hd].reshape(HEAD_DIM // SUBLANES, SUBLANES, QB) / l
        outs.append(o.reshape(HEAD_DIM, QB))
    o_ref[0] = jnp.concatenate(outs, axis=0).T.astype(BF16)


def _ffn_kernel(x_ref, ps_ref, ya_ref, g1_ref, wgate_ref, wpa_ref, wo_ref, g2_ref, wfg_ref, wfu_ref, wfd_ref,
                o_ref):
    x = x_ref[0]
    h = _rms(x, g1_ref[...]).astype(BF16)
    gates = _dot(h, wgate_ref[...])
    pa = _dot(ya_ref[0], wpa_ref[...])
    merged = (jax.nn.sigmoid(gates[:, 0:D_MODEL]) * ps_ref[0].astype(F32)
              + jax.nn.sigmoid(gates[:, D_MODEL:2 * D_MODEL]) * pa)
    x1 = x + _dot(merged.astype(BF16), wo_ref[...])
    h2 = _rms(x1, g2_ref[...]).astype(BF16)
    hid = jax.nn.silu(_dot(h2, wfg_ref[...])) * _dot(h2, wfu_ref[...])
    o_ref[0] = x1 + _dot(hid.astype(BF16), wfd_ref[...])


def _gate_repack_kernel(a_ref, b_ref, c_ref, o_ref):
    off = GATE_COL0 % GATE_RB
    rows = jnp.concatenate([a_ref[off:], b_ref[...], c_ref[:off]], axis=0)
    o_ref[...] = rows.T.astype(BF16)


def _const_spec(shape):
    nd = len(shape)
    return pl.BlockSpec(shape, lambda *_: (0,) * nd, pipeline_mode=pl.Buffered(1))


def _block_diag(blocks):
    g, r, c = blocks.shape
    eye = jnp.eye(g, dtype=blocks.dtype)
    return jnp.einsum('grc,gh->grhc', blocks, eye).reshape(g * r, g * c)


def kernel(x, norm1_g, w_in, A_re, A_im, log_dt, B_re, B_im, C_re, C_im, D_skip, w_glu, b_glu, q_norm_g, k_norm_g,
           idx_k_norm_g, w_proj_ssm, w_proj_attn, w_out, norm2_g, w_ffn_gate, w_ffn_up, w_ffn_down):
    bsz, seq, _ = x.shape
    assert x.shape == (8, 2048, D_MODEL) and w_in.shape[0] == 1
    G, N, P = SSM_GROUPS, SSM_STATE, SSM_GROUP
    cparams = functools.partial(pltpu.CompilerParams, vmem_limit_bytes=VMEM_LIMIT)

    w_in_t = w_in[0].T
    gblk = GATE_COL0 // GATE_RB
    spec = lambda k: pl.BlockSpec((GATE_RB, D_MODEL), lambda j: (gblk + 2 * j + k, 0))
    w_gates = pl.pallas_call(
        _gate_repack_kernel,
        grid=(2 * D_MODEL // LANES,),
        in_specs=[spec(0), spec(1), spec(2)],
        out_specs=pl.BlockSpec((D_MODEL, LANES), lambda j: (0, j)),
        out_shape=jax.ShapeDtypeStruct((D_MODEL, 2 * D_MODEL), BF16),
        name="gate_weight_repack",
    )(w_in_t, w_in_t, w_in_t)
    g1 = norm1_g[0].reshape(1, D_MODEL)
    ones64 = jnp.ones((HEAD_DIM,), F32)

    rep = lambda a: jnp.repeat(a, P, axis=0)
    gpn = jax.ShapeDtypeStruct((G * P, N), F32)
    abar_r, abar_i, bp_r, bp_i = pl.pallas_call(
        _disc_kernel, out_shape=(gpn, gpn, gpn, gpn), name="s5_discretise",
    )(rep(A_re[0]), rep(A_im[0]), rep(jnp.broadcast_to(log_dt[0].reshape(G, 1), (G, N))),
      B_re[0].transpose(0, 2, 1).reshape(G * P, N), B_im[0].transpose(0, 2, 1).reshape(G * P, N))
    abar_r, abar_i = abar_r[::P], abar_i[::P]
    bp_r, bp_i = bp_r.reshape(G, P, N), bp_i.reshape(G, P, N)
    hg = G // 2
    bp = [jnp.concatenate([_block_diag(bp_r[sl]), _block_diag(bp_i[sl])], axis=1).astype(BF16)
          for sl in (slice(0, hg), slice(hg, G))]
    c_r = C_re[0].transpose(0, 2, 1)
    c_i = C_im[0].transpose(0, 2, 1)
    cp = [jnp.concatenate([_block_diag(c_r[sl]), -_block_diag(c_i[sl])], axis=0).astype(BF16)
          for sl in (slice(0, hg), slice(hg, G))]
    ar8 = jnp.broadcast_to(abar_r.reshape(1, G * N), (SUBLANES, G * N))
    ai8 = jnp.broadcast_to(abar_i.reshape(1, G * N), (SUBLANES, G * N))

    n_in = seq // TL_IN
    u_t, q_h, qs_h, sg_t, k_n, ki_n, v_t = pl.pallas_call(
        _inproj_kernel,
        grid=(bsz, n_in),
        in_specs=[
            pl.BlockSpec((1, TL_IN, D_MODEL), lambda b, i: (b, i, 0)),
            _const_spec((1, D_MODEL)),
            _const_spec((W1_COLS, D_MODEL)),
            _const_spec((1, ATTN_WIDTH)), _const_spec((1, LANES)), _const_spec((1, LANES)),
            _const_spec((2 * LANES, 2 * LANES)),
        ],
        out_specs=[
            pl.BlockSpec((1, TL_IN, SSM_WIDTH), lambda b, i: (b, i, 0)),
            pl.BlockSpec((1, N_HEADS, K_EXT, TL_IN), lambda b, i: (b, 0, 0, i)),
            pl.BlockSpec((1, IDX_HEADS, IDX_DIM, TL_IN), lambda b, i: (b, 0, 0, i)),
            pl.BlockSpec((1, SUBLANES, TL_IN), lambda b, i: (b, 0, i)),
            pl.BlockSpec((1, TL_IN, K_EXT), lambda b, i: (b, i, 0)),
            pl.BlockSpec((1, TL_IN, IDX_DIM), lambda b, i: (b, i, 0)),
            pl.BlockSpec((1, TL_IN // TK, HEAD_DIM, TK), lambda b, i: (b, i, 0, 0)),
        ],
        out_shape=(
            jax.ShapeDtypeStruct((bsz, seq, SSM_WIDTH), F32),
            jax.ShapeDtypeStruct((bsz, N_HEADS, K_EXT, seq), BF16),
            jax.ShapeDtypeStruct((bsz, IDX_HEADS, IDX_DIM, seq), BF16),
            jax.ShapeDtypeStruct((bsz, SUBLANES, seq), F32),
            jax.ShapeDtypeStruct((bsz, seq, K_EXT), BF16),
            jax.ShapeDtypeStruct((bsz, seq, IDX_DIM), BF16),
            jax.ShapeDtypeStruct((bsz, seq // TK, HEAD_DIM, TK), BF16),
        ),
        scratch_shapes=[pltpu.VMEM((D_MODEL, W1_COLS), BF16)],
        compiler_params=cparams(dimension_semantics=("arbitrary", "arbitrary")),
        name="in_projection",
    )(x, g1, w_in_t, jnp.tile(q_norm_g[0], N_HEADS).reshape(1, ATTN_WIDTH),
      jnp.concatenate([k_norm_g[0], ones64]).reshape(1, LANES),
      jnp.concatenate([idx_k_norm_g[0], ones64]).reshape(1, LANES),
      jnp.kron(jnp.eye(2 * LANES // HEAD_DIM, dtype=BF16), jnp.ones((HEAD_DIM, HEAD_DIM), BF16)))

    rows = SUBLANES * TC_SCAN
    ps_t = pl.pallas_call(
        _s5_kernel,
        grid=(seq // TC_SCAN,),
        in_specs=[
            pl.BlockSpec((bsz, TC_SCAN, SSM_WIDTH), lambda c: (0, c, 0)),
            _const_spec((256, 2048)), _const_spec((256, 2048)),
            _const_spec((2048, 256)), _const_spec((2048, 256)),
            _const_spec((SUBLANES, G * N)), _const_spec((SUBLANES, G * N)),
            _const_spec((1, SSM_WIDTH)), _const_spec((SSM_WIDTH, SSM_WIDTH)), _const_spec((1, SSM_WIDTH)),
            _const_spec((SSM_WIDTH, D_MODEL)),
        ],
        out_specs=pl.BlockSpec((bsz, TC_SCAN, D_MODEL), lambda c: (0, c, 0)),
        out_shape=jax.ShapeDtypeStruct((bsz, seq, D_MODEL), BF16),
        scratch_shapes=[pltpu.VMEM((rows, 2 * G * N), F32), pltpu.VMEM((SUBLANES, 2 * G * N), F32)],
        compiler_params=cparams(dimension_semantics=("arbitrary",)),
        name="s5_branch",
    )(u_t, bp[0], bp[1], cp[0], cp[1], ar8, ai8,
      D_skip[0].reshape(1, SSM_WIDTH), w_glu[0].astype(BF16), b_glu[0].reshape(1, SSM_WIDTH),
      w_proj_ssm[0].astype(BF16))

    y_att = pl.pallas_call(
        _attn_kernel,
        grid=(bsz, seq // QB),
        in_specs=[
            pl.BlockSpec((1, IDX_HEADS, IDX_DIM, QB), lambda b, i: (b, 0, 0, i)),
            pl.BlockSpec((1, SUBLANES, QB), lambda b, i: (b, 0, i)),
            pl.BlockSpec((1, N_HEADS, K_EXT, QB), lambda b, i: (b, 0, 0, i)),
            pl.BlockSpec((1, seq, IDX_DIM), lambda b, i: (b, 0, 0)),
            pl.BlockSpec((1, seq, K_EXT), lambda b, i: (b, 0, 0)),
            pl.BlockSpec((1, seq // TK, HEAD_DIM, TK), lambda b, i: (b, 0, 0, 0)),
        ],
        out_specs=pl.BlockSpec((1, QB, ATTN_WIDTH), lambda b, i: (b, i, 0)),
        out_shape=jax.ShapeDtypeStruct((bsz, seq, ATTN_WIDTH), BF16),
        scratch_shapes=[pltpu.VMEM((seq // TK, TK, QB), F32),
                        pltpu.VMEM((N_HEADS, SUBLANES, QB), F32),
                        pltpu.VMEM((N_HEADS, SUBLANES, QB), F32),
                        pltpu.VMEM((N_HEADS, HEAD_DIM, QB), F32),
                        pltpu.VMEM((N_HEADS // 2, TK, 2 * QB), F32),
                        pltpu.VMEM((N_HEADS, K_EXT, QB), BF16),
                        pltpu.SMEM((1,), F32)],
        compiler_params=cparams(dimension_semantics=("parallel", "arbitrary")),
        name="sparse_attention",
    )(qs_h, sg_t, q_h, ki_n, k_n, v_t)

    out = pl.pallas_call(
        _ffn_kernel,
        grid=(bsz, seq // TL_FFN),
        in_specs=[
            pl.BlockSpec((1, TL_FFN, D_MODEL), lambda b, i: (b, i, 0)),
            pl.BlockSpec((1, TL_FFN, D_MODEL), lambda b, i: (b, i, 0)),
            pl.BlockSpec((1, TL_FFN, ATTN_WIDTH), lambda b, i: (b, i, 0)),
            _const_spec((1, D_MODEL)),
            _const_spec((D_MODEL, 2 * D_MODEL)),
            _const_spec((ATTN_WIDTH, D_MODEL)),
            _const_spec((D_MODEL, D_MODEL)),
            _const_spec((1, D_MODEL)),
            _const_spec((D_MODEL, D_FF)), _const_spec((D_MODEL, D_FF)), _const_spec((D_FF, D_MODEL)),
        ],
        out_specs=pl.BlockSpec((1, TL_FFN, D_MODEL), lambda b, i: (b, i, 0)),
        out_shape=jax.ShapeDtypeStruct((bsz, seq, D_MODEL), F32),
        compiler_params=cparams(dimension_semantics=("parallel", "parallel")),
        name="merge_ffn",
    )(x, ps_t, y_att, g1, w_gates, w_proj_attn[0].astype(BF16),
      w_out[0].astype(BF16), norm2_g[0].reshape(1, D_MODEL), w_ffn_gate[0].astype(BF16),
      w_ffn_up[0].astype(BF16), w_ffn_down[0].astype(BF16))
    return out
```

```python
import functools
import math

import jax
import jax.numpy as jnp
from jax import lax
from jax.experimental import pallas as pl
from jax.experimental.pallas import tpu as pltpu

F32 = jnp.float32
BF16 = jnp.bfloat16

D_MODEL = 1024
SSM_WIDTH = 512
SSM_GROUP = 16
SSM_GROUPS = 32
SSM_STATE = 64
N_HEADS = 8
HEAD_DIM = 64
ATTN_WIDTH = 512
IDX_HEADS = 4
IDX_DIM = 64
INDEX_TOPK = 256
D_FF = 2816
RMS_EPS = 1e-6

LANES = 128
SUBLANES = 8
VMEM_LIMIT = 56 * 1024 * 1024

TL_IN = 512
TC_SCAN = 64
QB = 256
TK = 256
TL_FFN = 256

W1_COLS = 1536
GATE_COL0 = 1476
GATE_RB = 64
NEG_BIG = -1e30
K_EXT = 80
Q_SHIFT_ROW = 64
NORM_UP = 1.0 + 2.0 ** -7
L_FLOOR = 2.0 ** -100
assert QB % LANES == 0


def _dot(a, b):
    return jnp.dot(a, b, preferred_element_type=F32)


def _rms(x, g):
    return x * lax.rsqrt(jnp.mean(x * x, axis=-1, keepdims=True) + RMS_EPS) * g


def _disc_kernel(are_ref, aim_ref, ldt_ref, bre_ref, bim_ref, abr_ref, abi_ref, bpr_ref, bpi_ref):
    ar = are_ref[...]
    ai = aim_ref[...]
    dt = jnp.exp(ldt_ref[...])
    mag = jnp.exp(ar * dt)
    abar_r = mag * jnp.cos(ai * dt)
    abar_i = mag * jnp.sin(ai * dt)
    den = ar * ar + ai * ai
    nr = abar_r - 1.0
    coef_r = (nr * ar + abar_i * ai) / den
    coef_i = (abar_i * ar - nr * ai) / den
    abr_ref[...] = abar_r
    abi_ref[...] = abar_i
    br = bre_ref[...]
    bi = bim_ref[...]
    bpr_ref[...] = coef_r * br - coef_i * bi
    bpi_ref[...] = coef_r * bi + coef_i * br


def _inproj_kernel(x_ref, g1_ref, w_ref, qg_ref, kg_ref, ikg_ref, seg_ref,
                   u_ref, q_ref, qs_ref, sg_ref, k_ref, ki_ref, v_ref, w_scr):
    @pl.when(jnp.logical_and(pl.program_id(0) == 0, pl.program_id(1) == 0))
    def _():
        w_scr[...] = w_ref[...].T.astype(BF16)

    x = x_ref[0]
    h = _rms(x, g1_ref[...]).astype(BF16)
    proj = _dot(h, w_scr[...])
    u_ref[0] = proj[:, 0:512]
    seg = seg_ref[...]

    def head_rms(v, g):
        sq = (v * v).astype(BF16)
        n = v.shape[1]
        w = min(n, seg.shape[0])
        ss = jnp.concatenate([_dot(sq[:, c:c + w], seg[0:w, 0:w]) for c in range(0, n, w)], axis=1)
        return v * lax.rsqrt(ss * (1.0 / HEAD_DIM) + RMS_EPS) * g

    def store_heads_transposed(ref, v, with_shift_rows=False):
        for g in range(v.shape[1] // LANES):
            tg = v[:, g * LANES:(g + 1) * LANES].T
            for e in range(2):
                th = tg[e * HEAD_DIM:(e + 1) * HEAD_DIM].astype(BF16)
                ref[0, 2 * g + e, 0:HEAD_DIM] = th
                if with_shift_rows:
                    tf = th.astype(F32)
                    nrm = jnp.sqrt(jnp.sum(tf * tf, axis=0, keepdims=True)) * NORM_UP
                    pad = (K_EXT - HEAD_DIM, th.shape[1])
                    first = lax.broadcasted_iota(jnp.int32, pad, 0) == 0
                    ref[0, 2 * g + e, HEAD_DIM:K_EXT] = jnp.where(first, -nrm, 0.0).astype(BF16)

    qscale = (HEAD_DIM ** -0.5) * math.log2(math.e)
    store_heads_transposed(q_ref, head_rms(proj[:, 512:1024], qg_ref[...]) * qscale, with_shift_rows=True)

    kv = proj[:, 1024:1152]
    kiw = proj[:, 1408:1536]
    w_scale = (IDX_HEADS ** -0.5) * (IDX_DIM ** -0.5)
    wabs = jnp.abs(kiw) * w_scale
    sshape = (LANES, IDX_HEADS * IDX_DIM)
    row = lax.broadcasted_iota(jnp.int32, sshape, 0)
    col = lax.broadcasted_iota(jnp.int32, sshape, 1)
    spread = jnp.where(row == IDX_DIM + col // IDX_DIM, 1.0, 0.0).astype(BF16)
    whi = wabs.astype(BF16)
    wlo = (wabs - whi.astype(F32)).astype(BF16)
    wrep = _dot(whi, spread) + _dot(wlo, spread)
    store_heads_transposed(qs_ref, proj[:, 1152:1408] * wrep)

    lane = lax.broadcasted_iota(jnp.int32, kv.shape, 1)
    k_ref[0] = jnp.where(lane < HEAD_DIM, head_rms(kv, kg_ref[...]), 0.0)[:, 0:K_EXT].astype(BF16)
    ki_ref[0] = head_rms(kiw, ikg_ref[...])[:, 0:IDX_DIM].astype(BF16)
    t_kv = kv.T
    t_kiw = kiw.T
    sg_ref[0] = jnp.where(t_kiw[IDX_DIM:IDX_DIM + SUBLANES] >= 0, 1.0, -1.0)
    for c in range(TL_IN // TK):
        v_ref[0, c] = t_kv[HEAD_DIM:2 * HEAD_DIM, c * TK:(c + 1) * TK].astype(BF16)


def _s5_kernel(u_ref, bplo_ref, bphi_ref, cplo_ref, cphi_ref, ar_ref, ai_ref, d_ref, wglu_ref, bglu_ref,
               wps_ref, o_ref, bu_scr, st_scr):
    @pl.when(pl.program_id(0) == 0)
    def _():
        st_scr[...] = jnp.zeros_like(st_scr)

    u = jnp.transpose(u_ref[...], (1, 0, 2)).reshape(SUBLANES * TC_SCAN, SSM_WIDTH)
    ub = u.astype(BF16)
    half_w = 2 * (SSM_GROUPS // 2) * SSM_STATE
    n_re = half_w // 2

    def scan_half(h):
        c0 = h * half_w
        a_r = ar_ref[:, h * n_re:(h + 1) * n_re]
        a_i = ai_ref[:, h * n_re:(h + 1) * n_re]
        sr = st_scr[:, c0:c0 + n_re]
        si = st_scr[:, c0 + n_re:c0 + half_w]
        for t in range(TC_SCAN):
            rows = slice(t * SUBLANES, (t + 1) * SUBLANES)
            br = bu_scr[rows, c0:c0 + n_re]
            bi = bu_scr[rows, c0 + n_re:c0 + half_w]
            sr, si = a_r * sr - a_i * si + br, a_r * si + a_i * sr + bi
            bu_scr[rows, c0:c0 + n_re] = sr
            bu_scr[rows, c0 + n_re:c0 + half_w] = si
        st_scr[:, c0:c0 + n_re] = sr
        st_scr[:, c0 + n_re:c0 + half_w] = si

    bu_scr[:, 0:half_w] = _dot(ub[:, 0:256], bplo_ref[...])
    bu_scr[:, half_w:2 * half_w] = _dot(ub[:, 256:512], bphi_ref[...])
    scan_half(0)
    y_lo = _dot(bu_scr[:, 0:half_w].astype(BF16), cplo_ref[...])
    scan_half(1)
    y_hi = _dot(bu_scr[:, half_w:2 * half_w].astype(BF16), cphi_ref[...])
    y = jnp.concatenate([y_lo, y_hi], axis=1)
    y = jax.nn.gelu(y + d_ref[...] * u)
    z = _dot(y.astype(BF16), wglu_ref[...]) + bglu_ref[...]
    y = y * jax.nn.sigmoid(z)
    ps = _dot(y.astype(BF16), wps_ref[...])
    o_ref[...] = jnp.transpose(ps.reshape(TC_SCAN, SUBLANES, D_MODEL), (1, 0, 2)).astype(BF16)


def _key_to_f32(key):
    bits = jnp.where(key < 0, key & jnp.int32(0x7FFFFFFF), ~key)
    return pltpu.bitcast(bits, F32)


def _attn_kernel(qs_ref, sg_ref, q_ref, ki_ref, k_ref, vt_ref, o_ref,
                 s_scr, m_scr, l_scr, acc_scr, sp_scr, qmod_scr, kmax_scr):
    i = pl.program_id(1)
    n_tiles = (i * QB + QB + TK - 1) // TK
    nv = TK // SUBLANES
    shape3 = (nv, SUBLANES, QB)
    k_in_tile = (lax.broadcasted_iota(jnp.int32, shape3, 0) * SUBLANES
                 + lax.broadcasted_iota(jnp.int32, shape3, 1))
    q_pos = i * QB + lax.broadcasted_iota(jnp.int32, shape3, 2)
    ksel = float(INDEX_TOPK)

    def all_sublanes(a, op):
        for sh in (4, 2, 1):
            a = op(a, pltpu.roll(a, sh, 0))
        return a

    sg = sg_ref[0]

    def idx_body(j, carry):
        kt = ki_ref[0, pl.ds(pl.multiple_of(j * TK, TK), TK), :]
        acc = jnp.zeros((TK, QB), F32)
        for pr in range(IDX_HEADS // 2):
            x = _dot(kt, jnp.concatenate([qs_ref[0, 2 * pr], qs_ref[0, 2 * pr + 1]], axis=1))
            for e in range(2):
                hd = 2 * pr + e
                acc = acc + sg[hd:hd + 1, :] * jnp.maximum(x[:, e * QB:(e + 1) * QB], 0.0)
        vis = j * TK + k_in_tile <= q_pos
        s_scr[j] = jnp.where(vis, acc.reshape(shape3), -jnp.inf).reshape(TK, QB)
        return carry

    def for_each_tile(body):
        def pair(jj, carry):
            body(2 * jj, carry)
            return body(2 * jj + 1, carry)
        lax.fori_loop(0, n_tiles // 2, pair, 0)

        @pl.when(n_tiles % 2 == 1)
        def _():
            body(n_tiles - 1, 0)

    for_each_tile(idx_body)

    def count(pred):
        def tile(j, acc):
            hit = jnp.where(pred(s_scr[j].reshape(shape3), j), 1.0, 0.0)
            return acc + jnp.sum(hit.reshape(nv // 4, 4, SUBLANES, QB), axis=0)
        acc = lax.fori_loop(0, n_tiles // 2, lambda jj, a: tile(2 * jj + 1, tile(2 * jj, a)),
                            jnp.zeros((4, SUBLANES, QB), F32))
        acc = lax.cond(n_tiles % 2 == 1, lambda a: tile(n_tiles - 1, a), lambda a: a, acc)
        return all_sublanes(jnp.sum(acc, axis=0), jnp.add)

    every = float(2 * INDEX_TOPK * 1024)

    def bit_body(it, carry):
        key, cnt_key = carry
        cand = key | lax.shift_left(jnp.int32(1), 31 - it)
        tc = _key_to_f32(cand)
        cnt = jnp.where(lax.shift_right_logical(cand, 23) == 0, every, count(lambda s, j: s >= tc))
        ok = cnt >= ksel
        return jnp.where(ok, cand, key), jnp.where(ok, cnt, cnt_key)

    key, cnt_ge = lax.fori_loop(0, 32, bit_body, (jnp.zeros((SUBLANES, QB), jnp.int32),
                                                  jnp.full((SUBLANES, QB), every, F32)))
    thr = _key_to_f32(key)
    has_excess = jnp.max(cnt_ge) > ksel

    def write_bias(select):
        def body(j, carry):
            s = s_scr[j].reshape(shape3)
            vis = j * TK + k_in_tile <= q_pos
            bias = jnp.where(vis, jnp.where(select(s, j), 0.0, -jnp.inf), -jnp.inf)
            s_scr[j] = bias.reshape(TK, QB)
            return carry
        lax.fori_loop(0, n_tiles, body, 0)

    @pl.when(jnp.logical_not(has_excess))
    def _():
        write_bias(lambda s, j: s >= thr)

    @pl.when(has_excess)
    def _():
        cnt_gt = count(lambda s, j: s > thr)
        need = ksel - cnt_gt

        def jbit_body(it, jkey):
            cand = jkey | lax.shift_left(jnp.int32(1), 10 - it)
            cnt = count(lambda s, j: (s == thr) & (j * TK + k_in_tile < cand))
            return jnp.where(cnt < need, cand, jkey)

        jkey = lax.fori_loop(0, 11, jbit_body, jnp.zeros((SUBLANES, QB), jnp.int32))
        write_bias(lambda s, j: (s > thr) | ((s == thr) & (j * TK + k_in_tile <= jkey)))

    lane_k = lax.broadcasted_iota(jnp.int32, (TK, K_EXT), 1)

    def attend(get_q, k_fill):
        l_scr[...] = jnp.zeros(l_scr.shape, F32)
        acc_scr[...] = jnp.zeros(acc_scr.shape, F32)

        def body(j, carry):
            kt = k_ref[0, pl.ds(pl.multiple_of(j * TK, TK), TK), :]
            kt = jnp.where(lane_k == Q_SHIFT_ROW, jnp.full((TK, K_EXT), k_fill, F32).astype(BF16), kt)
            vt = vt_ref[0, j]
            bias = s_scr[j].reshape(shape3)
            for pr in range(N_HEADS // 2):
                sp_scr[pr] = _dot(kt, jnp.concatenate([get_q(2 * pr), get_q(2 * pr + 1)], axis=1))
            for pr in range(N_HEADS // 2):
                for e in range(2):
                    hd = 2 * pr + e
                    p = jnp.exp2(sp_scr[pr, :, e * QB:(e + 1) * QB].reshape(shape3) + bias)
                    l_scr[hd] = l_scr[hd] + jnp.sum(p, axis=0)
                    acc_scr[hd] = acc_scr[hd] + _dot(vt, p.reshape(TK, QB).astype(BF16))
            return carry

        for_each_tile(body)

    @pl.when(i == 0)
    def _():
        kf = k_ref[0].astype(F32)
        kmax_scr[0] = jnp.max(jnp.sqrt(jnp.sum(kf * kf, axis=1, keepdims=True))) * NORM_UP

    attend(lambda hd: q_ref[0, hd], kmax_scr[0])

    l_min = all_sublanes(l_scr[0], jnp.add)
    for hd in range(1, N_HEADS):
        l_min = jnp.minimum(l_min, all_sublanes(l_scr[hd], jnp.add))
    underflow = jnp.logical_not(jnp.min(l_min) >= L_FLOOR)

    @pl.when(underflow)
    def _():
        row_q = lax.broadcasted_iota(jnp.int32, (K_EXT, QB), 0)
        m_scr[...] = jnp.full(m_scr.shape, NEG_BIG, F32)

        def max_body(j, carry):
            kt = k_ref[0, pl.ds(pl.multiple_of(j * TK, TK), TK), :]
            kt = jnp.where(lane_k == Q_SHIFT_ROW, jnp.zeros((TK, K_EXT), BF16), kt)
            bias = s_scr[j].reshape(shape3)
            for hd in range(N_HEADS):
                s = _dot(kt, q_ref[0, hd]).reshape(shape3) + bias
                m_scr[hd] = jnp.maximum(m_scr[hd], all_sublanes(jnp.max(s, axis=0), jnp.maximum))
            return carry

        lax.fori_loop(0, n_tiles, max_body, 0)
        for hd in range(N_HEADS):
            shift = jnp.broadcast_to(-m_scr[hd][0:1, :], (K_EXT, QB)).astype(BF16)
            qmod_scr[hd] = jnp.where(row_q == Q_SHIFT_ROW, shift, q_ref[0, hd])
        attend(lambda hd: qmod_scr[hd], 1.0)

    outs = []
    for hd in range(N_HEADS):
        l = all_sublanes(l_scr[hd], jnp.add)
        o = acc_scr[hd].reshape(HEAD_DIM // SUBLANES, SUBLANES, QB) / l
        outs.append(o.reshape(HEAD_DIM, QB))
    o_ref[0] = jnp.concatenate(outs, axis=0).T.astype(BF16)


def _ffn_kernel(x_ref, ps_ref, ya_ref, g1_ref, wgate_ref, wpa_ref, wo_ref, g2_ref, wfg_ref, wfu_ref, wfd_ref,
                o_ref):
    x = x_ref[0]
    h = _rms(x, g1_ref[...]).astype(BF16)
    gates = _dot(h, wgate_ref[...])
    pa = _dot(ya_ref[0], wpa_ref[...])
    merged = (jax.nn.sigmoid(gates[:, 0:D_MODEL]) * ps_ref[0].astype(F32)
              + jax.nn.sigmoid(gates[:, D_MODEL:2 * D_MODEL]) * pa)
    x1 = x + _dot(merged.astype(BF16), wo_ref[...])
    h2 = _rms(x1, g2_ref[...]).astype(BF16)
    hid = jax.nn.silu(_dot(h2, wfg_ref[...])) * _dot(h2, wfu_ref[...])
    o_ref[0] = x1 + _dot(hid.astype(BF16), wfd_ref[...])


def _gate_repack_kernel(a_ref, b_ref, c_ref, o_ref):
    off = GATE_COL0 % GATE_RB
    rows = jnp.concatenate([a_ref[off:], b_ref[...], c_ref[:off]], axis=0)
    o_ref[...] = rows.T.astype(BF16)


def _const_spec(shape):
    nd = len(shape)
    return pl.BlockSpec(shape, lambda *_: (0,) * nd, pipeline_mode=pl.Buffered(1))


def _block_diag(blocks):
    g, r, c = blocks.shape
    eye = jnp.eye(g, dtype=blocks.dtype)
    return jnp.einsum('grc,gh->grhc', blocks, eye).reshape(g * r, g * c)


def kernel(x, norm1_g, w_in, A_re, A_im, log_dt, B_re, B_im, C_re, C_im, D_skip, w_glu, b_glu, q_norm_g, k_norm_g,
           idx_k_norm_g, w_proj_ssm, w_proj_attn, w_out, norm2_g, w_ffn_gate, w_ffn_up, w_ffn_down):
    bsz, seq, _ = x.shape
    assert x.shape == (8, 2048, D_MODEL) and w_in.shape[0] == 1
    G, N, P = SSM_GROUPS, SSM_STATE, SSM_GROUP
    cparams = functools.partial(pltpu.CompilerParams, vmem_limit_bytes=VMEM_LIMIT)

    w_in_t = w_in[0].T
    gblk = GATE_COL0 // GATE_RB
    spec = lambda k: pl.BlockSpec((GATE_RB, D_MODEL), lambda j: (gblk + 2 * j + k, 0))
    w_gates = pl.pallas_call(
        _gate_repack_kernel,
        grid=(2 * D_MODEL // LANES,),
        in_specs=[spec(0), spec(1), spec(2)],
        out_specs=pl.BlockSpec((D_MODEL, LANES), lambda j: (0, j)),
        out_shape=jax.ShapeDtypeStruct((D_MODEL, 2 * D_MODEL), BF16),
        name="gate_weight_repack",
    )(w_in_t, w_in_t, w_in_t)
    g1 = norm1_g[0].reshape(1, D_MODEL)
    ones64 = jnp.ones((HEAD_DIM,), F32)

    rep = lambda a: jnp.repeat(a, P, axis=0)
    gpn = jax.ShapeDtypeStruct((G * P, N), F32)
    abar_r, abar_i, bp_r, bp_i = pl.pallas_call(
        _disc_kernel, out_shape=(gpn, gpn, gpn, gpn), name="s5_discretise",
    )(rep(A_re[0]), rep(A_im[0]), rep(jnp.broadcast_to(log_dt[0].reshape(G, 1), (G, N))),
      B_re[0].transpose(0, 2, 1).reshape(G * P, N), B_im[0].transpose(0, 2, 1).reshape(G * P, N))
    abar_r, abar_i = abar_r[::P], abar_i[::P]
    bp_r, bp_i = bp_r.reshape(G, P, N), bp_i.reshape(G, P, N)
    hg = G // 2
    bp = [jnp.concatenate([_block_diag(bp_r[sl]), _block_diag(bp_i[sl])], axis=1).astype(BF16)
          for sl in (slice(0, hg), slice(hg, G))]
    c_r = C_re[0].transpose(0, 2, 1)
    c_i = C_im[0].transpose(0, 2, 1)
    cp = [jnp.concatenate([_block_diag(c_r[sl]), -_block_diag(c_i[sl])], axis=0).astype(BF16)
          for sl in (slice(0, hg), slice(hg, G))]
    ar8 = jnp.broadcast_to(abar_r.reshape(1, G * N), (SUBLANES, G * N))
    ai8 = jnp.broadcast_to(abar_i.reshape(1, G * N), (SUBLANES, G * N))

    n_in = seq // TL_IN
    u_t, q_h, qs_h, sg_t, k_n, ki_n, v_t = pl.pallas_call(
        _inproj_kernel,
        grid=(bsz, n_in),
        in_specs=[
            pl.BlockSpec((1, TL_IN, D_MODEL), lambda b, i: (b, i, 0)),
            _const_spec((1, D_MODEL)),
            _const_spec((W1_COLS, D_MODEL)),
            _const_spec((1, ATTN_WIDTH)), _const_spec((1, LANES)), _const_spec((1, LANES)),
            _const_spec((2 * LANES, 2 * LANES)),
        ],
        out_specs=[
            pl.BlockSpec((1, TL_IN, SSM_WIDTH), lambda b, i: (b, i, 0)),
            pl.BlockSpec((1, N_HEADS, K_EXT, TL_IN), lambda b, i: (b, 0, 0, i)),
            pl.BlockSpec((1, IDX_HEADS, IDX_DIM, TL_IN), lambda b, i: (b, 0, 0, i)),
            pl.BlockSpec((1, SUBLANES, TL_IN), lambda b, i: (b, 0, i)),
            pl.BlockSpec((1, TL_IN, K_EXT), lambda b, i: (b, i, 0)),
            pl.BlockSpec((1, TL_IN, IDX_DIM), lambda b, i: (b, i, 0)),
            pl.BlockSpec((1, TL_IN // TK, HEAD_DIM, TK), lambda b, i: (b, i, 0, 0)),
        ],
        out_shape=(
            jax.ShapeDtypeStruct((bsz, seq, SSM_WIDTH), F32),
            jax.ShapeDtypeStruct((bsz, N_HEADS, K_EXT, seq), BF16),
            jax.ShapeDtypeStruct((bsz, IDX_HEADS, IDX_DIM, seq), BF16),
            jax.ShapeDtypeStruct((bsz, SUBLANES, seq), F32),
            jax.ShapeDtypeStruct((bsz, seq, K_EXT), BF16),
            jax.ShapeDtypeStruct((bsz, seq, IDX_DIM), BF16),
            jax.ShapeDtypeStruct((bsz, seq // TK, HEAD_DIM, TK), BF16),
        ),
        scratch_shapes=[pltpu.VMEM((D_MODEL, W1_COLS), BF16)],
        compiler_params=cparams(dimension_semantics=("arbitrary", "arbitrary")),
        name="in_projection",
    )(x, g1, w_in_t, jnp.tile(q_norm_g[0], N_HEADS).reshape(1, ATTN_WIDTH),
      jnp.concatenate([k_norm_g[0], ones64]).reshape(1, LANES),
      jnp.concatenate([idx_k_norm_g[0], ones64]).reshape(1, LANES),
      jnp.kron(jnp.eye(2 * LANES // HEAD_DIM, dtype=BF16), jnp.ones((HEAD_DIM, HEAD_DIM), BF16)))

    rows = SUBLANES * TC_SCAN
    ps_t = pl.pallas_call(
        _s5_kernel,
        grid=(seq // TC_SCAN,),
        in_specs=[
            pl.BlockSpec((bsz, TC_SCAN, SSM_WIDTH), lambda c: (0, c, 0)),
            _const_spec((256, 2048)), _const_spec((256, 2048)),
            _const_spec((2048, 256)), _const_spec((2048, 256)),
            _const_spec((SUBLANES, G * N)), _const_spec((SUBLANES, G * N)),
            _const_spec((1, SSM_WIDTH)), _const_spec((SSM_WIDTH, SSM_WIDTH)), _const_spec((1, SSM_WIDTH)),
            _const_spec((SSM_WIDTH, D_MODEL)),
        ],
        out_specs=pl.BlockSpec((bsz, TC_SCAN, D_MODEL), lambda c: (0, c, 0)),
        out_shape=jax.ShapeDtypeStruct((bsz, seq, D_MODEL), BF16),
        scratch_shapes=[pltpu.VMEM((rows, 2 * G * N), F32), pltpu.VMEM((SUBLANES, 2 * G * N), F32)],
        compiler_params=cparams(dimension_semantics=("arbitrary",)),
        name="s5_branch",
    )(u_t, bp[0], bp[1], cp[0], cp[1], ar8, ai8,
      D_skip[0].reshape(1, SSM_WIDTH), w_glu[0].astype(BF16), b_glu[0].reshape(1, SSM_WIDTH),
      w_proj_ssm[0].astype(BF16))

    y_att = pl.pallas_call(
        _attn_kernel,
        grid=(bsz, seq // QB),
        in_specs=[
            pl.BlockSpec((1, IDX_HEADS, IDX_DIM, QB), lambda b, i: (b, 0, 0, i)),
            pl.BlockSpec((1, SUBLANES, QB), lambda b, i: (b, 0, i)),
            pl.BlockSpec((1, N_HEADS, K_EXT, QB), lambda b, i: (b, 0, 0, i)),
            pl.BlockSpec((1, seq, IDX_DIM), lambda b, i: (b, 0, 0)),
            pl.BlockSpec((1, seq, K_EXT), lambda b, i: (b, 0, 0)),
            pl.BlockSpec((1, seq // TK, HEAD_DIM, TK), lambda b, i: (b, 0, 0, 0)),
        ],
        out_specs=pl.BlockSpec((1, QB, ATTN_WIDTH), lambda b, i: (b, i, 0)),
        out_shape=jax.ShapeDtypeStruct((bsz, seq, ATTN_WIDTH), BF16),
        scratch_shapes=[pltpu.VMEM((seq // TK, TK, QB), F32),
                        pltpu.VMEM((N_HEADS, SUBLANES, QB), F32),
                        pltpu.VMEM((N_HEADS, SUBLANES, QB), F32),
                        pltpu.VMEM((N_HEADS, HEAD_DIM, QB), F32),
                        pltpu.VMEM((N_HEADS // 2, TK, 2 * QB), F32),
                        pltpu.VMEM((N_HEADS, K_EXT, QB), BF16),
                        pltpu.SMEM((1,), F32)],
        compiler_params=cparams(dimension_semantics=("parallel", "arbitrary")),
        name="sparse_attention",
    )(qs_h, sg_t, q_h, ki_n, k_n, v_t)

    out = pl.pallas_call(
        _ffn_kernel,
        grid=(bsz, seq // TL_FFN),
        in_specs=[
            pl.BlockSpec((1, TL_FFN, D_MODEL), lambda b, i: (b, i, 0)),
            pl.BlockSpec((1, TL_FFN, D_MODEL), lambda b, i: (b, i, 0)),
            pl.BlockSpec((1, TL_FFN, ATTN_WIDTH), lambda b, i: (b, i, 0)),
            _const_spec((1, D_MODEL)),
            _const_spec((D_MODEL, 2 * D_MODEL)),
            _const_spec((ATTN_WIDTH, D_MODEL)),
            _const_spec((D_MODEL, D_MODEL)),
            _const_spec((1, D_MODEL)),
            _const_spec((D_MODEL, D_FF)), _const_spec((D_MODEL, D_FF)), _const_spec((D_FF, D_MODEL)),
        ],
        out_specs=pl.BlockSpec((1, TL_FFN, D_MODEL), lambda b, i: (b, i, 0)),
        out_shape=jax.ShapeDtypeStruct((bsz, seq, D_MODEL), F32),
        compiler_params=cparams(dimension_semantics=("parallel", "parallel")),
        name="merge_ffn",
    )(x, ps_t, y_att, g1, w_gates, w_proj_attn[0].astype(BF16),
      w_out[0].astype(BF16), norm2_g[0].reshape(1, D_MODEL), w_ffn_gate[0].astype(BF16),
      w_ffn_up[0].astype(BF16), w_ffn_down[0].astype(BF16))
    return out
```

```python
import functools
import math

import jax
import jax.numpy as jnp
from jax import lax
from jax.experimental import pallas as pl
from jax.experimental.pallas import tpu as pltpu

F32 = jnp.float32
BF16 = jnp.bfloat16

D_MODEL = 1024
SSM_WIDTH = 512
SSM_GROUP = 16
SSM_GROUPS = 32
SSM_STATE = 64
N_HEADS = 8
HEAD_DIM = 64
ATTN_WIDTH = 512
IDX_HEADS = 4
IDX_DIM = 64
INDEX_TOPK = 256
D_FF = 2816
RMS_EPS = 1e-6

LANES = 128
SUBLANES = 8
VMEM_LIMIT = 56 * 1024 * 1024

TL_IN = 512
TC_SCAN = 64
QB = 256
TK = 256
TL_FFN = 256

W1_COLS = 1536
GATE_COL0 = 1476
GATE_RB = 64
NEG_BIG = -1e30
K_EXT = 80
Q_SHIFT_ROW = 64
NORM_UP = 1.0 + 2.0 ** -7
L_FLOOR = 2.0 ** -100
assert QB % LANES == 0


def _dot(a, b):
    return jnp.dot(a, b, preferred_element_type=F32)


def _rms(x, g):
    return x * lax.rsqrt(jnp.mean(x * x, axis=-1, keepdims=True) + RMS_EPS) * g


def _disc_kernel(are_ref, aim_ref, ldt_ref, bre_ref, bim_ref, abr_ref, abi_ref, bpr_ref, bpi_ref):
    ar = are_ref[...]
    ai = aim_ref[...]
    dt = jnp.exp(ldt_ref[...])
    mag = jnp.exp(ar * dt)
    abar_r = mag * jnp.cos(ai * dt)
    abar_i = mag * jnp.sin(ai * dt)
    den = ar * ar + ai * ai
    nr = abar_r - 1.0
    coef_r = (nr * ar + abar_i * ai) / den
    coef_i = (abar_i * ar - nr * ai) / den
    abr_ref[...] = abar_r
    abi_ref[...] = abar_i
    br = bre_ref[...]
    bi = bim_ref[...]
    bpr_ref[...] = coef_r * br - coef_i * bi
    bpi_ref[...] = coef_r * bi + coef_i * br


def _inproj_kernel(x_ref, g1_ref, w_ref, qg_ref, kg_ref, ikg_ref, seg_ref,
                   u_ref, q_ref, qs_ref, sg_ref, k_ref, ki_ref, v_ref, w_scr):
    @pl.when(jnp.logical_and(pl.program_id(0) == 0, pl.program_id(1) == 0))
    def _():
        w_scr[...] = w_ref[...].T.astype(BF16)

    x = x_ref[0]
    h = _rms(x, g1_ref[...]).astype(BF16)
    proj = _dot(h, w_scr[...])
    u_ref[0] = proj[:, 0:512]
    seg = seg_ref[...]

    def head_rms(v, g):
        sq = (v * v).astype(BF16)
        n = v.shape[1]
        w = min(n, seg.shape[0])
        ss = jnp.concatenate([_dot(sq[:, c:c + w], seg[0:w, 0:w]) for c in range(0, n, w)], axis=1)
        return v * lax.rsqrt(ss * (1.0 / HEAD_DIM) + RMS_EPS) * g

    def store_heads_transposed(ref, v, with_shift_rows=False):
        for g in range(v.shape[1] // LANES):
            tg = v[:, g * LANES:(g + 1) * LANES].T
            for e in range(2):
                th = tg[e * HEAD_DIM:(e + 1) * HEAD_DIM].astype(BF16)
                ref[0, 2 * g + e, 0:HEAD_DIM] = th
                if with_shift_rows:
                    tf = th.astype(F32)
                    nrm = jnp.sqrt(jnp.sum(tf * tf, axis=0, keepdims=True)) * NORM_UP
                    pad = (K_EXT - HEAD_DIM, th.shape[1])
                    first = lax.broadcasted_iota(jnp.int32, pad, 0) == 0
                    ref[0, 2 * g + e, HEAD_DIM:K_EXT] = jnp.where(first, -nrm, 0.0).astype(BF16)

    qscale = (HEAD_DIM ** -0.5) * math.log2(math.e)
    store_heads_transposed(q_ref, head_rms(proj[:, 512:1024], qg_ref[...]) * qscale, with_shift_rows=True)

    kv = proj[:, 1024:1152]
    kiw = proj[:, 1408:1536]
    w_scale = (IDX_HEADS ** -0.5) * (IDX_DIM ** -0.5)
    wabs = jnp.abs(kiw) * w_scale
    sshape = (LANES, IDX_HEADS * IDX_DIM)
    row = lax.broadcasted_iota(jnp.int32, sshape, 0)
    col = lax.broadcasted_iota(jnp.int32, sshape, 1)
    spread = jnp.where(row == IDX_DIM + col // IDX_DIM, 1.0, 0.0).astype(BF16)
    whi = wabs.astype(BF16)
    wlo = (wabs - whi.astype(F32)).astype(BF16)
    wrep = _dot(whi, spread) + _dot(wlo, spread)
    store_heads_transposed(qs_ref, proj[:, 1152:1408] * wrep)

    lane = lax.broadcasted_iota(jnp.int32, kv.shape, 1)
    k_ref[0] = jnp.where(lane < HEAD_DIM, head_rms(kv, kg_ref[...]), 0.0)[:, 0:K_EXT].astype(BF16)
    ki_ref[0] = head_rms(kiw, ikg_ref[...])[:, 0:IDX_DIM].astype(BF16)
    t_kv = kv.T
    t_kiw = kiw.T
    sg_ref[0] = jnp.where(t_kiw[IDX_DIM:IDX_DIM + SUBLANES] >= 0, 1.0, -1.0)
    for c in range(TL_IN // TK):
        v_ref[0, c] = t_kv[HEAD_DIM:2 * HEAD_DIM, c * TK:(c + 1) * TK].astype(BF16)


def _s5_kernel(u_ref, bplo_ref, bphi_ref, cplo_ref, cphi_ref, ar_ref, ai_ref, d_ref, wglu_ref, bglu_ref,
               wps_ref, o_ref, bu_scr, st_scr):
    @pl.when(pl.program_id(0) == 0)
    def _():
        st_scr[...] = jnp.zeros_like(st_scr)

    u = jnp.transpose(u_ref[...], (1, 0, 2)).reshape(SUBLANES * TC_SCAN, SSM_WIDTH)
    ub = u.astype(BF16)
    half_w = 2 * (SSM_GROUPS // 2) * SSM_STATE
    n_re = half_w // 2

    def scan_half(h):
        c0 = h * half_w
        a_r = ar_ref[:, h * n_re:(h + 1) * n_re]
        a_i = ai_ref[:, h * n_re:(h + 1) * n_re]
        sr = st_scr[:, c0:c0 + n_re]
        si = st_scr[:, c0 + n_re:c0 + half_w]
        for t in range(TC_SCAN):
            rows = slice(t * SUBLANES, (t + 1) * SUBLANES)
            br = bu_scr[rows, c0:c0 + n_re]
            bi = bu_scr[rows, c0 + n_re:c0 + half_w]
            sr, si = a_r * sr - a_i * si + br, a_r * si + a_i * sr + bi
            bu_scr[rows, c0:c0 + n_re] = sr
            bu_scr[rows, c0 + n_re:c0 + half_w] = si
        st_scr[:, c0:c0 + n_re] = sr
        st_scr[:, c0 + n_re:c0 + half_w] = si

    bu_scr[:, 0:half_w] = _dot(ub[:, 0:256], bplo_ref[...])
    bu_scr[:, half_w:2 * half_w] = _dot(ub[:, 256:512], bphi_ref[...])
    scan_half(0)
    y_lo = _dot(bu_scr[:, 0:half_w].astype(BF16), cplo_ref[...])
    scan_half(1)
    y_hi = _dot(bu_scr[:, half_w:2 * half_w].astype(BF16), cphi_ref[...])
    y = jnp.concatenate([y_lo, y_hi], axis=1)
    y = jax.nn.gelu(y + d_ref[...] * u)
    z = _dot(y.astype(BF16), wglu_ref[...]) + bglu_ref[...]
    y = y * jax.nn.sigmoid(z)
    ps = _dot(y.astype(BF16), wps_ref[...])
    o_ref[...] = jnp.transpose(ps.reshape(TC_SCAN, SUBLANES, D_MODEL), (1, 0, 2)).astype(BF16)


def _key_to_f32(key):
    bits = jnp.where(key < 0, key & jnp.int32(0x7FFFFFFF), ~key)
    return pltpu.bitcast(bits, F32)


def _attn_kernel(qs_ref, sg_ref, q_ref, ki_ref, k_ref, vt_ref, o_ref,
                 s_scr, m_scr, l_scr, acc_scr, sp_scr, qmod_scr, kmax_scr, sb_scr):
    i = pl.program_id(1)
    n_tiles = (i * QB + QB + TK - 1) // TK
    nv = TK // SUBLANES
    shape3 = (nv, SUBLANES, QB)
    k_in_tile = (lax.broadcasted_iota(jnp.int32, shape3, 0) * SUBLANES
                 + lax.broadcasted_iota(jnp.int32, shape3, 1))
    q_pos = i * QB + lax.broadcasted_iota(jnp.int32, shape3, 2)
    ksel = float(INDEX_TOPK)

    def all_sublanes(a, op):
        for sh in (4, 2, 1):
            a = op(a, pltpu.roll(a, sh, 0))
        return a

    sg = sg_ref[0]

    def idx_body(j, carry):
        kt = ki_ref[0, pl.ds(pl.multiple_of(j * TK, TK), TK), :]
        acc = jnp.zeros((TK, QB), F32)
        for pr in range(IDX_HEADS // 2):
            x = _dot(kt, jnp.concatenate([qs_ref[0, 2 * pr], qs_ref[0, 2 * pr + 1]], axis=1))
            for e in range(2):
                hd = 2 * pr + e
                acc = acc + sg[hd:hd + 1, :] * jnp.maximum(x[:, e * QB:(e + 1) * QB], 0.0)
        vis = j * TK + k_in_tile <= q_pos
        sc = jnp.where(vis, acc.reshape(shape3), -jnp.inf).reshape(TK, QB)
        s_scr[j] = sc
        sb_scr[j] = sc.astype(BF16)
        return carry

    def for_each_tile(body):
        def pair(jj, carry):
            body(2 * jj, carry)
            return body(2 * jj + 1, carry)
        lax.fori_loop(0, n_tiles // 2, pair, 0)

        @pl.when(n_tiles % 2 == 1)
        def _():
            body(n_tiles - 1, 0)

    for_each_tile(idx_body)

    def count(pred):
        def tile(j, acc):
            hit = jnp.where(pred(s_scr[j].reshape(shape3), j), 1.0, 0.0)
            return acc + jnp.sum(hit.reshape(nv // 4, 4, SUBLANES, QB), axis=0)
        acc = lax.fori_loop(0, n_tiles // 2, lambda jj, a: tile(2 * jj + 1, tile(2 * jj, a)),
                            jnp.zeros((4, SUBLANES, QB), F32))
        acc = lax.cond(n_tiles % 2 == 1, lambda a: tile(n_tiles - 1, a), lambda a: a, acc)
        return all_sublanes(jnp.sum(acc, axis=0), jnp.add)

    every = float(2 * INDEX_TOPK * 1024)
    nv16 = TK // (2 * SUBLANES)
    one_b, zero_b = jnp.ones((), BF16), jnp.zeros((), BF16)

    def count_hi(gc):
        def tile(j, acc):
            hit = jnp.where(sb_scr[j].reshape(nv16, 2 * SUBLANES, QB) >= gc, one_b, zero_b)
            for k in range(nv16):
                acc[k % 4] = acc[k % 4] + hit[k]
            return acc
        acc = lax.fori_loop(0, n_tiles // 2, lambda jj, a: tuple(tile(2 * jj + 1, tile(2 * jj, list(a)))),
                            tuple(jnp.zeros((2 * SUBLANES, QB), BF16) for _ in range(4)))
        acc = lax.cond(n_tiles % 2 == 1, lambda a: tuple(tile(n_tiles - 1, list(a))), lambda a: a, acc)
        tot = (acc[0].astype(F32) + acc[1].astype(F32)) + (acc[2].astype(F32) + acc[3].astype(F32))
        return all_sublanes(tot[0:SUBLANES] + tot[SUBLANES:2 * SUBLANES], jnp.add)

    def grid_key(k16):
        return lax.shift_left(k16, 16) | jnp.where(k16 < 0x8000, 0xFFFF, 0)

    def hi_body(it, k16):
        cand = k16 | lax.shift_left(jnp.int32(1), 15 - it)
        g = _key_to_f32(grid_key(cand))
        n = count_hi(jnp.concatenate([g, g], axis=0).astype(BF16))
        cnt = jnp.where(lax.shift_right_logical(cand, 7) == 0, every, n)
        return jnp.where(cnt >= ksel, cand, k16)

    k16 = lax.fori_loop(0, 16, hi_body, jnp.zeros((SUBLANES, QB), jnp.int32))

    def lo_body(_, carry):
        lo, hi, cnt_lo = carry
        mid = lo + lax.shift_right_logical(hi - lo, 1)
        tc = _key_to_f32(mid)
        cnt = jnp.where(lax.shift_right_logical(mid, 23) == 0, every, count(lambda s, j: s >= tc))
        ok = cnt >= ksel
        return jnp.where(ok, mid, lo), jnp.where(ok, hi, mid), jnp.where(ok, cnt, cnt_lo)

    key, _, cnt_ge = lax.fori_loop(
        0, 18, lo_body,
        (grid_key(jnp.maximum(k16 - 1, 0)), grid_key(jnp.minimum(k16 + 1, 0xFFFF)),
         jnp.full((SUBLANES, QB), every, F32)))
    thr = _key_to_f32(key)
    has_excess = jnp.max(cnt_ge) > ksel

    def write_bias(select):
        def body(j, carry):
            s = s_scr[j].reshape(shape3)
            vis = j * TK + k_in_tile <= q_pos
            bias = jnp.where(vis, jnp.where(select(s, j), 0.0, -jnp.inf), -jnp.inf)
            s_scr[j] = bias.reshape(TK, QB)
            return carry
        lax.fori_loop(0, n_tiles, body, 0)

    @pl.when(jnp.logical_not(has_excess))
    def _():
        write_bias(lambda s, j: s >= thr)

    @pl.when(has_excess)
    def _():
        cnt_gt = count(lambda s, j: s > thr)
        need = ksel - cnt_gt

        def jbit_body(it, jkey):
            cand = jkey | lax.shift_left(jnp.int32(1), 10 - it)
            cnt = count(lambda s, j: (s == thr) & (j * TK + k_in_tile < cand))
            return jnp.where(cnt < need, cand, jkey)

        jkey = lax.fori_loop(0, 11, jbit_body, jnp.zeros((SUBLANES, QB), jnp.int32))
        write_bias(lambda s, j: (s > thr) | ((s == thr) & (j * TK + k_in_tile <= jkey)))

    lane_k = lax.broadcasted_iota(jnp.int32, (TK, K_EXT), 1)

    def attend(get_q, k_fill):
        l_scr[...] = jnp.zeros(l_scr.shape, F32)
        acc_scr[...] = jnp.zeros(acc_scr.shape, F32)

        def body(j, carry):
            kt = k_ref[0, pl.ds(pl.multiple_of(j * TK, TK), TK), :]
            kt = jnp.where(lane_k == Q_SHIFT_ROW, jnp.full((TK, K_EXT), k_fill, F32).astype(BF16), kt)
            vt = vt_ref[0, j]
            bias = s_scr[j].reshape(shape3)
            for pr in range(N_HEADS // 2):
                sp_scr[pr] = _dot(kt, jnp.concatenate([get_q(2 * pr), get_q(2 * pr + 1)], axis=1))
            for pr in range(N_HEADS // 2):
                for e in range(2):
                    hd = 2 * pr + e
                    p = jnp.exp2(sp_scr[pr, :, e * QB:(e + 1) * QB].reshape(shape3) + bias)
                    l_scr[hd] = l_scr[hd] + jnp.sum(p, axis=0)
                    acc_scr[hd] = acc_scr[hd] + _dot(vt, p.reshape(TK, QB).astype(BF16))
            return carry

        for_each_tile(body)

    @pl.when(i == 0)
    def _():
        kf = k_ref[0].astype(F32)
        kmax_scr[0] = jnp.max(jnp.sqrt(jnp.sum(kf * kf, axis=1, keepdims=True))) * NORM_UP

    attend(lambda hd: q_ref[0, hd], kmax_scr[0])

    l_min = all_sublanes(l_scr[0], jnp.add)
    for hd in range(1, N_HEADS):
        l_min = jnp.minimum(l_min, all_sublanes(l_scr[hd], jnp.add))
    underflow = jnp.logical_not(jnp.min(l_min) >= L_FLOOR)

    @pl.when(underflow)
    def _():
        row_q = lax.broadcasted_iota(jnp.int32, (K_EXT, QB), 0)
        m_scr[...] = jnp.full(m_scr.shape, NEG_BIG, F32)

        def max_body(j, carry):
            kt = k_ref[0, pl.ds(pl.multiple_of(j * TK, TK), TK), :]
            kt = jnp.where(lane_k == Q_SHIFT_ROW, jnp.zeros((TK, K_EXT), BF16), kt)
            bias = s_scr[j].reshape(shape3)
            for hd in range(N_HEADS):
                s = _dot(kt, q_ref[0, hd]).reshape(shape3) + bias
                m_scr[hd] = jnp.maximum(m_scr[hd], all_sublanes(jnp.max(s, axis=0), jnp.maximum))
            return carry

        lax.fori_loop(0, n_tiles, max_body, 0)
        for hd in range(N_HEADS):
            shift = jnp.broadcast_to(-m_scr[hd][0:1, :], (K_EXT, QB)).astype(BF16)
            qmod_scr[hd] = jnp.where(row_q == Q_SHIFT_ROW, shift, q_ref[0, hd])
        attend(lambda hd: qmod_scr[hd], 1.0)

    outs = []
    for hd in range(N_HEADS):
        l = all_sublanes(l_scr[hd], jnp.add)
        o = acc_scr[hd].reshape(HEAD_DIM // SUBLANES, SUBLANES, QB) / l
        outs.append(o.reshape(HEAD_DIM, QB))
    o_ref[0] = jnp.concatenate(outs, axis=0).T.astype(BF16)


def _ffn_kernel(x_ref, ps_ref, ya_ref, g1_ref, wgate_ref, wpa_ref, wo_ref, g2_ref, wfg_ref, wfu_ref, wfd_ref,
                o_ref):
    x = x_ref[0]
    h = _rms(x, g1_ref[...]).astype(BF16)
    gates = _dot(h, wgate_ref[...])
    pa = _dot(ya_ref[0], wpa_ref[...])
    merged = (jax.nn.sigmoid(gates[:, 0:D_MODEL]) * ps_ref[0].astype(F32)
              + jax.nn.sigmoid(gates[:, D_MODEL:2 * D_MODEL]) * pa)
    x1 = x + _dot(merged.astype(BF16), wo_ref[...])
    h2 = _rms(x1, g2_ref[...]).astype(BF16)
    hid = jax.nn.silu(_dot(h2, wfg_ref[...])) * _dot(h2, wfu_ref[...])
    o_ref[0] = x1 + _dot(hid.astype(BF16), wfd_ref[...])


def _gate_repack_kernel(a_ref, b_ref, c_ref, o_ref):
    off = GATE_COL0 % GATE_RB
    rows = jnp.concatenate([a_ref[off:], b_ref[...], c_ref[:off]], axis=0)
    o_ref[...] = rows.T.astype(BF16)


def _const_spec(shape):
    nd = len(shape)
    return pl.BlockSpec(shape, lambda *_: (0,) * nd, pipeline_mode=pl.Buffered(1))


def _block_diag(blocks):
    g, r, c = blocks.shape
    eye = jnp.eye(g, dtype=blocks.dtype)
    return jnp.einsum('grc,gh->grhc', blocks, eye).reshape(g * r, g * c)


def kernel(x, norm1_g, w_in, A_re, A_im, log_dt, B_re, B_im, C_re, C_im, D_skip, w_glu, b_glu, q_norm_g, k_norm_g,
           idx_k_norm_g, w_proj_ssm, w_proj_attn, w_out, norm2_g, w_ffn_gate, w_ffn_up, w_ffn_down):
    bsz, seq, _ = x.shape
    assert x.shape == (8, 2048, D_MODEL) and w_in.shape[0] == 1
    G, N, P = SSM_GROUPS, SSM_STATE, SSM_GROUP
    cparams = functools.partial(pltpu.CompilerParams, vmem_limit_bytes=VMEM_LIMIT)

    w_in_t = w_in[0].T
    gblk = GATE_COL0 // GATE_RB
    spec = lambda k: pl.BlockSpec((GATE_RB, D_MODEL), lambda j: (gblk + 2 * j + k, 0))
    w_gates = pl.pallas_call(
        _gate_repack_kernel,
        grid=(2 * D_MODEL // LANES,),
        in_specs=[spec(0), spec(1), spec(2)],
        out_specs=pl.BlockSpec((D_MODEL, LANES), lambda j: (0, j)),
        out_shape=jax.ShapeDtypeStruct((D_MODEL, 2 * D_MODEL), BF16),
        name="gate_weight_repack",
    )(w_in_t, w_in_t, w_in_t)
    g1 = norm1_g[0].reshape(1, D_MODEL)
    ones64 = jnp.ones((HEAD_DIM,), F32)

    rep = lambda a: jnp.repeat(a, P, axis=0)
    gpn = jax.ShapeDtypeStruct((G * P, N), F32)
    abar_r, abar_i, bp_r, bp_i = pl.pallas_call(
        _disc_kernel, out_shape=(gpn, gpn, gpn, gpn), name="s5_discretise",
    )(rep(A_re[0]), rep(A_im[0]), rep(jnp.broadcast_to(log_dt[0].reshape(G, 1), (G, N))),
      B_re[0].transpose(0, 2, 1).reshape(G * P, N), B_im[0].transpose(0, 2, 1).reshape(G * P, N))
    abar_r, abar_i = abar_r[::P], abar_i[::P]
    bp_r, bp_i = bp_r.reshape(G, P, N), bp_i.reshape(G, P, N)
    hg = G // 2
    bp = [jnp.concatenate([_block_diag(bp_r[sl]), _block_diag(bp_i[sl])], axis=1).astype(BF16)
          for sl in (slice(0, hg), slice(hg, G))]
    c_r = C_re[0].transpose(0, 2, 1)
    c_i = C_im[0].transpose(0, 2, 1)
    cp = [jnp.concatenate([_block_diag(c_r[sl]), -_block_diag(c_i[sl])], axis=0).astype(BF16)
          for sl in (slice(0, hg), slice(hg, G))]
    ar8 = jnp.broadcast_to(abar_r.reshape(1, G * N), (SUBLANES, G * N))
    ai8 = jnp.broadcast_to(abar_i.reshape(1, G * N), (SUBLANES, G * N))

    n_in = seq // TL_IN
    u_t, q_h, qs_h, sg_t, k_n, ki_n, v_t = pl.pallas_call(
        _inproj_kernel,
        grid=(bsz, n_in),
        in_specs=[
            pl.BlockSpec((1, TL_IN, D_MODEL), lambda b, i: (b, i, 0)),
            _const_spec((1, D_MODEL)),
            _const_spec((W1_COLS, D_MODEL)),
            _const_spec((1, ATTN_WIDTH)), _const_spec((1, LANES)), _const_spec((1, LANES)),
            _const_spec((2 * LANES, 2 * LANES)),
        ],
        out_specs=[
            pl.BlockSpec((1, TL_IN, SSM_WIDTH), lambda b, i: (b, i, 0)),
            pl.BlockSpec((1, N_HEADS, K_EXT, TL_IN), lambda b, i: (b, 0, 0, i)),
            pl.BlockSpec((1, IDX_HEADS, IDX_DIM, TL_IN), lambda b, i: (b, 0, 0, i)),
            pl.BlockSpec((1, SUBLANES, TL_IN), lambda b, i: (b, 0, i)),
            pl.BlockSpec((1, TL_IN, K_EXT), lambda b, i: (b, i, 0)),
            pl.BlockSpec((1, TL_IN, IDX_DIM), lambda b, i: (b, i, 0)),
            pl.BlockSpec((1, TL_IN // TK, HEAD_DIM, TK), lambda b, i: (b, i, 0, 0)),
        ],
        out_shape=(
            jax.ShapeDtypeStruct((bsz, seq, SSM_WIDTH), F32),
            jax.ShapeDtypeStruct((bsz, N_HEADS, K_EXT, seq), BF16),
            jax.ShapeDtypeStruct((bsz, IDX_HEADS, IDX_DIM, seq), BF16),
            jax.ShapeDtypeStruct((bsz, SUBLANES, seq), F32),
            jax.ShapeDtypeStruct((bsz, seq, K_EXT), BF16),
            jax.ShapeDtypeStruct((bsz, seq, IDX_DIM), BF16),
            jax.ShapeDtypeStruct((bsz, seq // TK, HEAD_DIM, TK), BF16),
        ),
        scratch_shapes=[pltpu.VMEM((D_MODEL, W1_COLS), BF16)],
        compiler_params=cparams(dimension_semantics=("arbitrary", "arbitrary")),
        name="in_projection",
    )(x, g1, w_in_t, jnp.tile(q_norm_g[0], N_HEADS).reshape(1, ATTN_WIDTH),
      jnp.concatenate([k_norm_g[0], ones64]).reshape(1, LANES),
      jnp.concatenate([idx_k_norm_g[0], ones64]).reshape(1, LANES),
      jnp.kron(jnp.eye(2 * LANES // HEAD_DIM, dtype=BF16), jnp.ones((HEAD_DIM, HEAD_DIM), BF16)))

    rows = SUBLANES * TC_SCAN
    ps_t = pl.pallas_call(
        _s5_kernel,
        grid=(seq // TC_SCAN,),
        in_specs=[
            pl.BlockSpec((bsz, TC_SCAN, SSM_WIDTH), lambda c: (0, c, 0)),
            _const_spec((256, 2048)), _const_spec((256, 2048)),
            _const_spec((2048, 256)), _const_spec((2048, 256)),
            _const_spec((SUBLANES, G * N)), _const_spec((SUBLANES, G * N)),
            _const_spec((1, SSM_WIDTH)), _const_spec((SSM_WIDTH, SSM_WIDTH)), _const_spec((1, SSM_WIDTH)),
            _const_spec((SSM_WIDTH, D_MODEL)),
        ],
        out_specs=pl.BlockSpec((bsz, TC_SCAN, D_MODEL), lambda c: (0, c, 0)),
        out_shape=jax.ShapeDtypeStruct((bsz, seq, D_MODEL), BF16),
        scratch_shapes=[pltpu.VMEM((rows, 2 * G * N), F32), pltpu.VMEM((SUBLANES, 2 * G * N), F32)],
        compiler_params=cparams(dimension_semantics=("arbitrary",)),
        name="s5_branch",
    )(u_t, bp[0], bp[1], cp[0], cp[1], ar8, ai8,
      D_skip[0].reshape(1, SSM_WIDTH), w_glu[0].astype(BF16), b_glu[0].reshape(1, SSM_WIDTH),
      w_proj_ssm[0].astype(BF16))

    y_att = pl.pallas_call(
        _attn_kernel,
        grid=(bsz, seq // QB),
        in_specs=[
            pl.BlockSpec((1, IDX_HEADS, IDX_DIM, QB), lambda b, i: (b, 0, 0, i)),
            pl.BlockSpec((1, SUBLANES, QB), lambda b, i: (b, 0, i)),
            pl.BlockSpec((1, N_HEADS, K_EXT, QB), lambda b, i: (b, 0, 0, i)),
            pl.BlockSpec((1, seq, IDX_DIM), lambda b, i: (b, 0, 0)),
            pl.BlockSpec((1, seq, K_EXT), lambda b, i: (b, 0, 0)),
            pl.BlockSpec((1, seq // TK, HEAD_DIM, TK), lambda b, i: (b, 0, 0, 0)),
        ],
        out_specs=pl.BlockSpec((1, QB, ATTN_WIDTH), lambda b, i: (b, i, 0)),
        out_shape=jax.ShapeDtypeStruct((bsz, seq, ATTN_WIDTH), BF16),
        scratch_shapes=[pltpu.VMEM((seq // TK, TK, QB), F32),
                        pltpu.VMEM((N_HEADS, SUBLANES, QB), F32),
                        pltpu.VMEM((N_HEADS, SUBLANES, QB), F32),
                        pltpu.VMEM((N_HEADS, HEAD_DIM, QB), F32),
                        pltpu.VMEM((N_HEADS // 2, TK, 2 * QB), F32),
                        pltpu.VMEM((N_HEADS, K_EXT, QB), BF16),
                        pltpu.SMEM((1,), F32),
                        pltpu.VMEM((seq // TK, TK, QB), BF16)],
        compiler_params=cparams(dimension_semantics=("parallel", "arbitrary")),
        name="sparse_attention",
    )(qs_h, sg_t, q_h, ki_n, k_n, v_t)

    out = pl.pallas_call(
        _ffn_kernel,
        grid=(bsz, seq // TL_FFN),
        in_specs=[
            pl.BlockSpec((1, TL_FFN, D_MODEL), lambda b, i: (b, i, 0)),
            pl.BlockSpec((1, TL_FFN, D_MODEL), lambda b, i: (b, i, 0)),
            pl.BlockSpec((1, TL_FFN, ATTN_WIDTH), lambda b, i: (b, i, 0)),
            _const_spec((1, D_MODEL)),
            _const_spec((D_MODEL, 2 * D_MODEL)),
            _const_spec((ATTN_WIDTH, D_MODEL)),
            _const_spec((D_MODEL, D_MODEL)),
            _const_spec((1, D_MODEL)),
            _const_spec((D_MODEL, D_FF)), _const_spec((D_MODEL, D_FF)), _const_spec((D_FF, D_MODEL)),
        ],
        out_specs=pl.BlockSpec((1, TL_FFN, D_MODEL), lambda b, i: (b, i, 0)),
        out_shape=jax.ShapeDtypeStruct((bsz, seq, D_MODEL), F32),
        compiler_params=cparams(dimension_semantics=("parallel", "parallel")),
        name="merge_ffn",
    )(x, ps_t, y_att, g1, w_gates, w_proj_attn[0].astype(BF16),
      w_out[0].astype(BF16), norm2_g[0].reshape(1, D_MODEL), w_ffn_gate[0].astype(BF16),
      w_ffn_up[0].astype(BF16), w_ffn_down[0].astype(BF16))
    return out
```

```python
import functools
import math

import jax
import jax.numpy as jnp
from jax import lax
from jax.experimental import pallas as pl
from jax.experimental.pallas import tpu as pltpu

F32 = jnp.float32
BF16 = jnp.bfloat16

D_MODEL = 1024
SSM_WIDTH = 512
SSM_GROUP = 16
SSM_GROUPS = 32
SSM_STATE = 64
N_HEADS = 8
HEAD_DIM = 64
ATTN_WIDTH = 512
IDX_HEADS = 4
IDX_DIM = 64
INDEX_TOPK = 256
D_FF = 2816
RMS_EPS = 1e-6

LANES = 128
SUBLANES = 8
VMEM_LIMIT = 56 * 1024 * 1024

TL_IN = 512
TC_SCAN = 128
QB = 256
TK = 256
TL_FFN = 256

W1_COLS = 1536
GATE_COL0 = 1476
GATE_RB = 64
NEG_BIG = -1e30
K_EXT = 80
Q_SHIFT_ROW = 64
NORM_UP = 1.0 + 2.0 ** -7
L_FLOOR = 2.0 ** -100
assert QB % LANES == 0


def _dot(a, b):
    return jnp.dot(a, b, preferred_element_type=F32)


def _rms(x, g):
    return x * lax.rsqrt(jnp.mean(x * x, axis=-1, keepdims=True) + RMS_EPS) * g


def _disc_kernel(are_ref, aim_ref, ldt_ref, bre_ref, bim_ref, abr_ref, abi_ref, bpr_ref, bpi_ref):
    ar = are_ref[...]
    ai = aim_ref[...]
    dt = jnp.exp(ldt_ref[...])
    mag = jnp.exp(ar * dt)
    abar_r = mag * jnp.cos(ai * dt)
    abar_i = mag * jnp.sin(ai * dt)
    den = ar * ar + ai * ai
    nr = abar_r - 1.0
    coef_r = (nr * ar + abar_i * ai) / den
    coef_i = (abar_i * ar - nr * ai) / den
    abr_ref[...] = abar_r
    abi_ref[...] = abar_i
    br = bre_ref[...]
    bi = bim_ref[...]
    bpr_ref[...] = coef_r * br - coef_i * bi
    bpi_ref[...] = coef_r * bi + coef_i * br


def _inproj_kernel(x_ref, g1_ref, w_ref, qg_ref, kg_ref, ikg_ref, seg_ref,
                   u_ref, q_ref, qs_ref, sg_ref, k_ref, ki_ref, v_ref, w_scr):
    @pl.when(jnp.logical_and(pl.program_id(0) == 0, pl.program_id(1) == 0))
    def _():
        w_scr[...] = w_ref[...].T.astype(BF16)

    x = x_ref[0]
    h = _rms(x, g1_ref[...]).astype(BF16)
    proj = _dot(h, w_scr[...])
    u_ref[0] = proj[:, 0:512]
    seg = seg_ref[...]

    def head_rms(v, g):
        sq = (v * v).astype(BF16)
        n = v.shape[1]
        w = min(n, seg.shape[0])
        ss = jnp.concatenate([_dot(sq[:, c:c + w], seg[0:w, 0:w]) for c in range(0, n, w)], axis=1)
        return v * lax.rsqrt(ss * (1.0 / HEAD_DIM) + RMS_EPS) * g

    def store_heads_transposed(ref, v, with_shift_rows=False):
        for g in range(v.shape[1] // LANES):
            tg = v[:, g * LANES:(g + 1) * LANES].T
            for e in range(2):
                th = tg[e * HEAD_DIM:(e + 1) * HEAD_DIM].astype(BF16)
                ref[0, 2 * g + e, 0:HEAD_DIM] = th
                if with_shift_rows:
                    tf = th.astype(F32)
                    nrm = jnp.sqrt(jnp.sum(tf * tf, axis=0, keepdims=True)) * NORM_UP
                    pad = (K_EXT - HEAD_DIM, th.shape[1])
                    first = lax.broadcasted_iota(jnp.int32, pad, 0) == 0
                    ref[0, 2 * g + e, HEAD_DIM:K_EXT] = jnp.where(first, -nrm, 0.0).astype(BF16)

    qscale = (HEAD_DIM ** -0.5) * math.log2(math.e)
    store_heads_transposed(q_ref, head_rms(proj[:, 512:1024], qg_ref[...]) * qscale, with_shift_rows=True)

    kv = proj[:, 1024:1152]
    kiw = proj[:, 1408:1536]
    w_scale = (IDX_HEADS ** -0.5) * (IDX_DIM ** -0.5)
    wabs = jnp.abs(kiw) * w_scale
    sshape = (LANES, IDX_HEADS * IDX_DIM)
    row = lax.broadcasted_iota(jnp.int32, sshape, 0)
    col = lax.broadcasted_iota(jnp.int32, sshape, 1)
    spread = jnp.where(row == IDX_DIM + col // IDX_DIM, 1.0, 0.0).astype(BF16)
    whi = wabs.astype(BF16)
    wlo = (wabs - whi.astype(F32)).astype(BF16)
    wrep = _dot(whi, spread) + _dot(wlo, spread)
    store_heads_transposed(qs_ref, proj[:, 1152:1408] * wrep)

    lane = lax.broadcasted_iota(jnp.int32, kv.shape, 1)
    k_ref[0] = jnp.where(lane < HEAD_DIM, head_rms(kv, kg_ref[...]), 0.0)[:, 0:K_EXT].astype(BF16)
    ki_ref[0] = head_rms(kiw, ikg_ref[...])[:, 0:IDX_DIM].astype(BF16)
    t_kv = kv.T
    t_kiw = kiw.T
    sg_ref[0] = jnp.where(t_kiw[IDX_DIM:IDX_DIM + SUBLANES] >= 0, 1.0, -1.0)
    for c in range(TL_IN // TK):
        v_ref[0, c] = t_kv[HEAD_DIM:2 * HEAD_DIM, c * TK:(c + 1) * TK].astype(BF16)


def _s5_kernel(u_ref, bplo_ref, bphi_ref, cplo_ref, cphi_ref, ar_ref, ai_ref, d_ref, wglu_ref, bglu_ref,
               wps_ref, o_ref, bu_scr, st_scr):
    @pl.when(pl.program_id(0) == 0)
    def _():
        st_scr[...] = jnp.zeros_like(st_scr)

    u = jnp.transpose(u_ref[...], (1, 0, 2)).reshape(SUBLANES * TC_SCAN, SSM_WIDTH)
    ub = u.astype(BF16)
    half_w = 2 * (SSM_GROUPS // 2) * SSM_STATE
    n_re = half_w // 2

    def scan_half(h):
        c0 = h * half_w
        a_r = ar_ref[:, h * n_re:(h + 1) * n_re]
        a_i = ai_ref[:, h * n_re:(h + 1) * n_re]
        sr = st_scr[:, c0:c0 + n_re]
        si = st_scr[:, c0 + n_re:c0 + half_w]
        for t in range(TC_SCAN):
            rows = slice(t * SUBLANES, (t + 1) * SUBLANES)
            br = bu_scr[rows, c0:c0 + n_re]
            bi = bu_scr[rows, c0 + n_re:c0 + half_w]
            sr, si = a_r * sr - a_i * si + br, a_r * si + a_i * sr + bi
            bu_scr[rows, c0:c0 + n_re] = sr
            bu_scr[rows, c0 + n_re:c0 + half_w] = si
        st_scr[:, c0:c0 + n_re] = sr
        st_scr[:, c0 + n_re:c0 + half_w] = si

    bu_scr[:, 0:half_w] = _dot(ub[:, 0:256], bplo_ref[...])
    bu_scr[:, half_w:2 * half_w] = _dot(ub[:, 256:512], bphi_ref[...])
    scan_half(0)
    y_lo = _dot(bu_scr[:, 0:half_w].astype(BF16), cplo_ref[...])
    scan_half(1)
    y_hi = _dot(bu_scr[:, half_w:2 * half_w].astype(BF16), cphi_ref[...])
    y = jnp.concatenate([y_lo, y_hi], axis=1)
    y = jax.nn.gelu(y + d_ref[...] * u)
    z = _dot(y.astype(BF16), wglu_ref[...]) + bglu_ref[...]
    y = y * jax.nn.sigmoid(z)
    ps = _dot(y.astype(BF16), wps_ref[...])
    o_ref[...] = jnp.transpose(ps.reshape(TC_SCAN, SUBLANES, D_MODEL), (1, 0, 2)).astype(BF16)


def _key_to_f32(key):
    bits = jnp.where(key < 0, key & jnp.int32(0x7FFFFFFF), ~key)
    return pltpu.bitcast(bits, F32)


def _attn_kernel(qs_ref, sg_ref, q_ref, ki_ref, k_ref, vt_ref, o_ref,
                 s_scr, m_scr, l_scr, acc_scr, sp_scr, qmod_scr, kmax_scr, sb_scr):
    i = pl.program_id(1)
    n_tiles = (i * QB + QB + TK - 1) // TK
    nv = TK // SUBLANES
    shape3 = (nv, SUBLANES, QB)
    k_in_tile = (lax.broadcasted_iota(jnp.int32, shape3, 0) * SUBLANES
                 + lax.broadcasted_iota(jnp.int32, shape3, 1))
    q_pos = i * QB + lax.broadcasted_iota(jnp.int32, shape3, 2)
    ksel = float(INDEX_TOPK)

    def all_sublanes(a, op):
        for sh in (4, 2, 1):
            a = op(a, pltpu.roll(a, sh, 0))
        return a

    sg = sg_ref[0]

    def idx_body(j, carry):
        kt = ki_ref[0, pl.ds(pl.multiple_of(j * TK, TK), TK), :]
        acc = jnp.zeros((TK, QB), F32)
        for pr in range(IDX_HEADS // 2):
            x = _dot(kt, jnp.concatenate([qs_ref[0, 2 * pr], qs_ref[0, 2 * pr + 1]], axis=1))
            for e in range(2):
                hd = 2 * pr + e
                acc = acc + sg[hd:hd + 1, :] * jnp.maximum(x[:, e * QB:(e + 1) * QB], 0.0)
        vis = j * TK + k_in_tile <= q_pos
        sc = jnp.where(vis, acc.reshape(shape3), -jnp.inf).reshape(TK, QB)
        s_scr[j] = sc
        sb_scr[j] = sc.astype(BF16)
        return carry

    def for_each_tile(body):
        def pair(jj, carry):
            body(2 * jj, carry)
            return body(2 * jj + 1, carry)
        lax.fori_loop(0, n_tiles // 2, pair, 0)

        @pl.when(n_tiles % 2 == 1)
        def _():
            body(n_tiles - 1, 0)

    for_each_tile(idx_body)

    def count(pred):
        def tile(j, acc):
            hit = jnp.where(pred(s_scr[j].reshape(shape3), j), 1.0, 0.0)
            return acc + jnp.sum(hit.reshape(nv // 4, 4, SUBLANES, QB), axis=0)
        acc = lax.fori_loop(0, n_tiles // 2, lambda jj, a: tile(2 * jj + 1, tile(2 * jj, a)),
                            jnp.zeros((4, SUBLANES, QB), F32))
        acc = lax.cond(n_tiles % 2 == 1, lambda a: tile(n_tiles - 1, a), lambda a: a, acc)
        return all_sublanes(jnp.sum(acc, axis=0), jnp.add)

    every = float(2 * INDEX_TOPK * 1024)
    nv16 = TK // (2 * SUBLANES)
    one_b, zero_b = jnp.ones((), BF16), jnp.zeros((), BF16)

    def count_hi(gc):
        def tile(j, acc):
            hit = jnp.where(sb_scr[j].reshape(nv16, 2 * SUBLANES, QB) >= gc, one_b, zero_b)
            for k in range(nv16):
                acc[k % 4] = acc[k % 4] + hit[k]
            return acc
        acc = lax.fori_loop(0, n_tiles // 2, lambda jj, a: tuple(tile(2 * jj + 1, tile(2 * jj, list(a)))),
                            tuple(jnp.zeros((2 * SUBLANES, QB), BF16) for _ in range(4)))
        acc = lax.cond(n_tiles % 2 == 1, lambda a: tuple(tile(n_tiles - 1, list(a))), lambda a: a, acc)
        tot = (acc[0].astype(F32) + acc[1].astype(F32)) + (acc[2].astype(F32) + acc[3].astype(F32))
        return all_sublanes(tot[0:SUBLANES] + tot[SUBLANES:2 * SUBLANES], jnp.add)

    def grid_key(k16):
        return lax.shift_left(k16, 16) | jnp.where(k16 < 0x8000, 0xFFFF, 0)

    def hi_body(it, k16):
        cand = k16 | lax.shift_left(jnp.int32(1), 15 - it)
        g = _key_to_f32(grid_key(cand))
        n = count_hi(jnp.concatenate([g, g], axis=0).astype(BF16))
        cnt = jnp.where(lax.shift_right_logical(cand, 7) == 0, every, n)
        return jnp.where(cnt >= ksel, cand, k16)

    k16 = lax.fori_loop(0, 16, hi_body, jnp.zeros((SUBLANES, QB), jnp.int32))

    def lo_body(_, carry):
        lo, hi, cnt_lo = carry
        mid = lo + lax.shift_right_logical(hi - lo, 1)
        tc = _key_to_f32(mid)
        cnt = jnp.where(lax.shift_right_logical(mid, 23) == 0, every, count(lambda s, j: s >= tc))
        ok = cnt >= ksel
        return jnp.where(ok, mid, lo), jnp.where(ok, hi, mid), jnp.where(ok, cnt, cnt_lo)

    key, _, cnt_ge = lax.fori_loop(
        0, 18, lo_body,
        (grid_key(jnp.maximum(k16 - 1, 0)), grid_key(jnp.minimum(k16 + 1, 0xFFFF)),
         jnp.full((SUBLANES, QB), every, F32)))
    thr = _key_to_f32(key)
    has_excess = jnp.max(cnt_ge) > ksel

    def write_bias(select):
        def body(j, carry):
            s = s_scr[j].reshape(shape3)
            vis = j * TK + k_in_tile <= q_pos
            bias = jnp.where(vis, jnp.where(select(s, j), 0.0, -jnp.inf), -jnp.inf)
            s_scr[j] = bias.reshape(TK, QB)
            return carry
        lax.fori_loop(0, n_tiles, body, 0)

    @pl.when(jnp.logical_not(has_excess))
    def _():
        write_bias(lambda s, j: s >= thr)

    @pl.when(has_excess)
    def _():
        cnt_gt = count(lambda s, j: s > thr)
        need = ksel - cnt_gt

        def jbit_body(it, jkey):
            cand = jkey | lax.shift_left(jnp.int32(1), 10 - it)
            cnt = count(lambda s, j: (s == thr) & (j * TK + k_in_tile < cand))
            return jnp.where(cnt < need, cand, jkey)

        jkey = lax.fori_loop(0, 11, jbit_body, jnp.zeros((SUBLANES, QB), jnp.int32))
        write_bias(lambda s, j: (s > thr) | ((s == thr) & (j * TK + k_in_tile <= jkey)))

    lane_k = lax.broadcasted_iota(jnp.int32, (TK, K_EXT), 1)

    def attend(get_q, k_fill):
        l_scr[...] = jnp.zeros(l_scr.shape, F32)
        acc_scr[...] = jnp.zeros(acc_scr.shape, F32)

        def body(j, carry):
            kt = k_ref[0, pl.ds(pl.multiple_of(j * TK, TK), TK), :]
            kt = jnp.where(lane_k == Q_SHIFT_ROW, jnp.full((TK, K_EXT), k_fill, F32).astype(BF16), kt)
            vt = vt_ref[0, j]
            bias = s_scr[j].reshape(shape3)
            for pr in range(N_HEADS // 2):
                sp_scr[pr] = _dot(kt, jnp.concatenate([get_q(2 * pr), get_q(2 * pr + 1)], axis=1))
            for pr in range(N_HEADS // 2):
                for e in range(2):
                    hd = 2 * pr + e
                    p = jnp.exp2(sp_scr[pr, :, e * QB:(e + 1) * QB].reshape(shape3) + bias)
                    l_scr[hd] = l_scr[hd] + jnp.sum(p, axis=0)
                    acc_scr[hd] = acc_scr[hd] + _dot(vt, p.reshape(TK, QB).astype(BF16))
            return carry

        for_each_tile(body)

    @pl.when(i == 0)
    def _():
        kf = k_ref[0].astype(F32)
        kmax_scr[0] = jnp.max(jnp.sqrt(jnp.sum(kf * kf, axis=1, keepdims=True))) * NORM_UP

    attend(lambda hd: q_ref[0, hd], kmax_scr[0])

    l_min = all_sublanes(l_scr[0], jnp.add)
    for hd in range(1, N_HEADS):
        l_min = jnp.minimum(l_min, all_sublanes(l_scr[hd], jnp.add))
    underflow = jnp.logical_not(jnp.min(l_min) >= L_FLOOR)

    @pl.when(underflow)
    def _():
        row_q = lax.broadcasted_iota(jnp.int32, (K_EXT, QB), 0)
        m_scr[...] = jnp.full(m_scr.shape, NEG_BIG, F32)

        def max_body(j, carry):
            kt = k_ref[0, pl.ds(pl.multiple_of(j * TK, TK), TK), :]
            kt = jnp.where(lane_k == Q_SHIFT_ROW, jnp.zeros((TK, K_EXT), BF16), kt)
            bias = s_scr[j].reshape(shape3)
            for hd in range(N_HEADS):
                s = _dot(kt, q_ref[0, hd]).reshape(shape3) + bias
                m_scr[hd] = jnp.maximum(m_scr[hd], all_sublanes(jnp.max(s, axis=0), jnp.maximum))
            return carry

        lax.fori_loop(0, n_tiles, max_body, 0)
        for hd in range(N_HEADS):
            shift = jnp.broadcast_to(-m_scr[hd][0:1, :], (K_EXT, QB)).astype(BF16)
            qmod_scr[hd] = jnp.where(row_q == Q_SHIFT_ROW, shift, q_ref[0, hd])
        attend(lambda hd: qmod_scr[hd], 1.0)

    outs = []
    for hd in range(N_HEADS):
        l = all_sublanes(l_scr[hd], jnp.add)
        o = acc_scr[hd].reshape(HEAD_DIM // SUBLANES, SUBLANES, QB) / l
        outs.append(o.reshape(HEAD_DIM, QB))
    o_ref[0] = jnp.concatenate(outs, axis=0).T.astype(BF16)


def _ffn_kernel(x_ref, ps_ref, ya_ref, g1_ref, wgate_ref, wpa_ref, wo_ref, g2_ref, wfg_ref, wfu_ref, wfd_ref,
                o_ref):
    x = x_ref[0]
    h = _rms(x, g1_ref[...]).astype(BF16)
    gates = _dot(h, wgate_ref[...])
    pa = _dot(ya_ref[0], wpa_ref[...])
    merged = (jax.nn.sigmoid(gates[:, 0:D_MODEL]) * ps_ref[0].astype(F32)
              + jax.nn.sigmoid(gates[:, D_MODEL:2 * D_MODEL]) * pa)
    x1 = x + _dot(merged.astype(BF16), wo_ref[...])
    h2 = _rms(x1, g2_ref[...]).astype(BF16)
    hid = jax.nn.silu(_dot(h2, wfg_ref[...])) * _dot(h2, wfu_ref[...])
    o_ref[0] = x1 + _dot(hid.astype(BF16), wfd_ref[...])


def _gate_repack_kernel(a_ref, b_ref, c_ref, o_ref):
    off = GATE_COL0 % GATE_RB
    rows = jnp.concatenate([a_ref[off:], b_ref[...], c_ref[:off]], axis=0)
    o_ref[...] = rows.T.astype(BF16)


def _const_spec(shape):
    nd = len(shape)
    return pl.BlockSpec(shape, lambda *_: (0,) * nd, pipeline_mode=pl.Buffered(1))


def _block_diag(blocks):
    g, r, c = blocks.shape
    eye = jnp.eye(g, dtype=blocks.dtype)
    return jnp.einsum('grc,gh->grhc', blocks, eye).reshape(g * r, g * c)


def kernel(x, norm1_g, w_in, A_re, A_im, log_dt, B_re, B_im, C_re, C_im, D_skip, w_glu, b_glu, q_norm_g, k_norm_g,
           idx_k_norm_g, w_proj_ssm, w_proj_attn, w_out, norm2_g, w_ffn_gate, w_ffn_up, w_ffn_down):
    bsz, seq, _ = x.shape
    assert x.shape == (8, 2048, D_MODEL) and w_in.shape[0] == 1
    G, N, P = SSM_GROUPS, SSM_STATE, SSM_GROUP
    cparams = functools.partial(pltpu.CompilerParams, vmem_limit_bytes=VMEM_LIMIT)

    w_in_t = w_in[0].T
    gblk = GATE_COL0 // GATE_RB
    spec = lambda k: pl.BlockSpec((GATE_RB, D_MODEL), lambda j: (gblk + 2 * j + k, 0))
    w_gates = pl.pallas_call(
        _gate_repack_kernel,
        grid=(2 * D_MODEL // LANES,),
        in_specs=[spec(0), spec(1), spec(2)],
        out_specs=pl.BlockSpec((D_MODEL, LANES), lambda j: (0, j)),
        out_shape=jax.ShapeDtypeStruct((D_MODEL, 2 * D_MODEL), BF16),
        name="gate_weight_repack",
    )(w_in_t, w_in_t, w_in_t)
    g1 = norm1_g[0].reshape(1, D_MODEL)
    ones64 = jnp.ones((HEAD_DIM,), F32)

    rep = lambda a: jnp.repeat(a, P, axis=0)
    gpn = jax.ShapeDtypeStruct((G * P, N), F32)
    abar_r, abar_i, bp_r, bp_i = pl.pallas_call(
        _disc_kernel, out_shape=(gpn, gpn, gpn, gpn), name="s5_discretise",
    )(rep(A_re[0]), rep(A_im[0]), rep(jnp.broadcast_to(log_dt[0].reshape(G, 1), (G, N))),
      B_re[0].transpose(0, 2, 1).reshape(G * P, N), B_im[0].transpose(0, 2, 1).reshape(G * P, N))
    abar_r, abar_i = abar_r[::P], abar_i[::P]
    bp_r, bp_i = bp_r.reshape(G, P, N), bp_i.reshape(G, P, N)
    hg = G // 2
    bp = [jnp.concatenate([_block_diag(bp_r[sl]), _block_diag(bp_i[sl])], axis=1).astype(BF16)
          for sl in (slice(0, hg), slice(hg, G))]
    c_r = C_re[0].transpose(0, 2, 1)
    c_i = C_im[0].transpose(0, 2, 1)
    cp = [jnp.concatenate([_block_diag(c_r[sl]), -_block_diag(c_i[sl])], axis=0).astype(BF16)
          for sl in (slice(0, hg), slice(hg, G))]
    ar8 = jnp.broadcast_to(abar_r.reshape(1, G * N), (SUBLANES, G * N))
    ai8 = jnp.broadcast_to(abar_i.reshape(1, G * N), (SUBLANES, G * N))

    n_in = seq // TL_IN
    u_t, q_h, qs_h, sg_t, k_n, ki_n, v_t = pl.pallas_call(
        _inproj_kernel,
        grid=(bsz, n_in),
        in_specs=[
            pl.BlockSpec((1, TL_IN, D_MODEL), lambda b, i: (b, i, 0)),
            _const_spec((1, D_MODEL)),
            _const_spec((W1_COLS, D_MODEL)),
            _const_spec((1, ATTN_WIDTH)), _const_spec((1, LANES)), _const_spec((1, LANES)),
            _const_spec((2 * LANES, 2 * LANES)),
        ],
        out_specs=[
            pl.BlockSpec((1, TL_IN, SSM_WIDTH), lambda b, i: (b, i, 0)),
            pl.BlockSpec((1, N_HEADS, K_EXT, TL_IN), lambda b, i: (b, 0, 0, i)),
            pl.BlockSpec((1, IDX_HEADS, IDX_DIM, TL_IN), lambda b, i: (b, 0, 0, i)),
            pl.BlockSpec((1, SUBLANES, TL_IN), lambda b, i: (b, 0, i)),
            pl.BlockSpec((1, TL_IN, K_EXT), lambda b, i: (b, i, 0)),
            pl.BlockSpec((1, TL_IN, IDX_DIM), lambda b, i: (b, i, 0)),
            pl.BlockSpec((1, TL_IN // TK, HEAD_DIM, TK), lambda b, i: (b, i, 0, 0)),
        ],
        out_shape=(
            jax.ShapeDtypeStruct((bsz, seq, SSM_WIDTH), F32),
            jax.ShapeDtypeStruct((bsz, N_HEADS, K_EXT, seq), BF16),
            jax.ShapeDtypeStruct((bsz, IDX_HEADS, IDX_DIM, seq), BF16),
            jax.ShapeDtypeStruct((bsz, SUBLANES, seq), F32),
            jax.ShapeDtypeStruct((bsz, seq, K_EXT), BF16),
            jax.ShapeDtypeStruct((bsz, seq, IDX_DIM), BF16),
            jax.ShapeDtypeStruct((bsz, seq // TK, HEAD_DIM, TK), BF16),
        ),
        scratch_shapes=[pltpu.VMEM((D_MODEL, W1_COLS), BF16)],
        compiler_params=cparams(dimension_semantics=("arbitrary", "arbitrary")),
        name="in_projection",
    )(x, g1, w_in_t, jnp.tile(q_norm_g[0], N_HEADS).reshape(1, ATTN_WIDTH),
      jnp.concatenate([k_norm_g[0], ones64]).reshape(1, LANES),
      jnp.concatenate([idx_k_norm_g[0], ones64]).reshape(1, LANES),
      jnp.kron(jnp.eye(2 * LANES // HEAD_DIM, dtype=BF16), jnp.ones((HEAD_DIM, HEAD_DIM), BF16)))

    rows = SUBLANES * TC_SCAN
    ps_t = pl.pallas_call(
        _s5_kernel,
        grid=(seq // TC_SCAN,),
        in_specs=[
            pl.BlockSpec((bsz, TC_SCAN, SSM_WIDTH), lambda c: (0, c, 0)),
            _const_spec((256, 2048)), _const_spec((256, 2048)),
            _const_spec((2048, 256)), _const_spec((2048, 256)),
            _const_spec((SUBLANES, G * N)), _const_spec((SUBLANES, G * N)),
            _const_spec((1, SSM_WIDTH)), _const_spec((SSM_WIDTH, SSM_WIDTH)), _const_spec((1, SSM_WIDTH)),
            _const_spec((SSM_WIDTH, D_MODEL)),
        ],
        out_specs=pl.BlockSpec((bsz, TC_SCAN, D_MODEL), lambda c: (0, c, 0)),
        out_shape=jax.ShapeDtypeStruct((bsz, seq, D_MODEL), BF16),
        scratch_shapes=[pltpu.VMEM((rows, 2 * G * N), F32), pltpu.VMEM((SUBLANES, 2 * G * N), F32)],
        compiler_params=cparams(dimension_semantics=("arbitrary",)),
        name="s5_branch",
    )(u_t, bp[0], bp[1], cp[0], cp[1], ar8, ai8,
      D_skip[0].reshape(1, SSM_WIDTH), w_glu[0].astype(BF16), b_glu[0].reshape(1, SSM_WIDTH),
      w_proj_ssm[0].astype(BF16))

    y_att = pl.pallas_call(
        _attn_kernel,
        grid=(bsz, seq // QB),
        in_specs=[
            pl.BlockSpec((1, IDX_HEADS, IDX_DIM, QB), lambda b, i: (b, 0, 0, i)),
            pl.BlockSpec((1, SUBLANES, QB), lambda b, i: (b, 0, i)),
            pl.BlockSpec((1, N_HEADS, K_EXT, QB), lambda b, i: (b, 0, 0, i)),
            pl.BlockSpec((1, seq, IDX_DIM), lambda b, i: (b, 0, 0)),
            pl.BlockSpec((1, seq, K_EXT), lambda b, i: (b, 0, 0)),
            pl.BlockSpec((1, seq // TK, HEAD_DIM, TK), lambda b, i: (b, 0, 0, 0)),
        ],
        out_specs=pl.BlockSpec((1, QB, ATTN_WIDTH), lambda b, i: (b, i, 0)),
        out_shape=jax.ShapeDtypeStruct((bsz, seq, ATTN_WIDTH), BF16),
        scratch_shapes=[pltpu.VMEM((seq // TK, TK, QB), F32),
                        pltpu.VMEM((N_HEADS, SUBLANES, QB), F32),
                        pltpu.VMEM((N_HEADS, SUBLANES, QB), F32),
                        pltpu.VMEM((N_HEADS, HEAD_DIM, QB), F32),
                        pltpu.VMEM((N_HEADS // 2, TK, 2 * QB), F32),
                        pltpu.VMEM((N_HEADS, K_EXT, QB), BF16),
                        pltpu.SMEM((1,), F32),
                        pltpu.VMEM((seq // TK, TK, QB), BF16)],
        compiler_params=cparams(dimension_semantics=("parallel", "arbitrary")),
        name="sparse_attention",
    )(qs_h, sg_t, q_h, ki_n, k_n, v_t)

    out = pl.pallas_call(
        _ffn_kernel,
        grid=(bsz, seq // TL_FFN),
        in_specs=[
            pl.BlockSpec((1, TL_FFN, D_MODEL), lambda b, i: (b, i, 0)),
            pl.BlockSpec((1, TL_FFN, D_MODEL), lambda b, i: (b, i, 0)),
            pl.BlockSpec((1, TL_FFN, ATTN_WIDTH), lambda b, i: (b, i, 0)),
            _const_spec((1, D_MODEL)),
            _const_spec((D_MODEL, 2 * D_MODEL)),
            _const_spec((ATTN_WIDTH, D_MODEL)),
            _const_spec((D_MODEL, D_MODEL)),
            _const_spec((1, D_MODEL)),
            _const_spec((D_MODEL, D_FF)), _const_spec((D_MODEL, D_FF)), _const_spec((D_FF, D_MODEL)),
        ],
        out_specs=pl.BlockSpec((1, TL_FFN, D_MODEL), lambda b, i: (b, i, 0)),
        out_shape=jax.ShapeDtypeStruct((bsz, seq, D_MODEL), F32),
        compiler_params=cparams(dimension_semantics=("parallel", "parallel")),
        name="merge_ffn",
    )(x, ps_t, y_att, g1, w_gates, w_proj_attn[0].astype(BF16),
      w_out[0].astype(BF16), norm2_g[0].reshape(1, D_MODEL), w_ffn_gate[0].astype(BF16),
      w_ffn_up[0].astype(BF16), w_ffn_down[0].astype(BF16))
    return out
```

```python
import functools
import math

import jax
import jax.numpy as jnp
from jax import lax
from jax.experimental import pallas as pl
from jax.experimental.pallas import tpu as pltpu

F32 = jnp.float32
BF16 = jnp.bfloat16

D_MODEL = 1024
SSM_WIDTH = 512
SSM_GROUP = 16
SSM_GROUPS = 32
SSM_STATE = 64
N_HEADS = 8
HEAD_DIM = 64
ATTN_WIDTH = 512
IDX_HEADS = 4
IDX_DIM = 64
INDEX_TOPK = 256
D_FF = 2816
RMS_EPS = 1e-6

LANES = 128
SUBLANES = 8
VMEM_LIMIT = 56 * 1024 * 1024

TL_IN = 512
TC_SCAN = 128
QB = 256
TK = 256
TL_FFN = 256

W1_COLS = 1536
GATE_COL0 = 1476
GATE_RB = 64
NEG_BIG = -1e30
K_EXT = 80
Q_SHIFT_ROW = 64
NORM_UP = 1.0 + 2.0 ** -7
L_FLOOR = 2.0 ** -100
assert QB % LANES == 0


def _dot(a, b):
    return jnp.dot(a, b, preferred_element_type=F32)


def _rms(x, g):
    return x * lax.rsqrt(jnp.mean(x * x, axis=-1, keepdims=True) + RMS_EPS) * g


def _disc_kernel(are_ref, aim_ref, ldt_ref, bre_ref, bim_ref, abr_ref, abi_ref, bpr_ref, bpi_ref):
    ar = are_ref[...]
    ai = aim_ref[...]
    dt = jnp.exp(ldt_ref[...])
    mag = jnp.exp(ar * dt)
    abar_r = mag * jnp.cos(ai * dt)
    abar_i = mag * jnp.sin(ai * dt)
    den = ar * ar + ai * ai
    nr = abar_r - 1.0
    coef_r = (nr * ar + abar_i * ai) / den
    coef_i = (abar_i * ar - nr * ai) / den
    abr_ref[...] = abar_r
    abi_ref[...] = abar_i
    br = bre_ref[...]
    bi = bim_ref[...]
    bpr_ref[...] = coef_r * br - coef_i * bi
    bpi_ref[...] = coef_r * bi + coef_i * br


def _inproj_kernel(x_ref, g1_ref, w_ref, qg_ref, kg_ref, ikg_ref, seg_ref,
                   u_ref, q_ref, qs_ref, sg_ref, k_ref, ki_ref, v_ref, w_scr):
    @pl.when(jnp.logical_and(pl.program_id(0) == 0, pl.program_id(1) == 0))
    def _():
        w_scr[...] = w_ref[...].T.astype(BF16)

    x = x_ref[0]
    h = _rms(x, g1_ref[...]).astype(BF16)
    proj = _dot(h, w_scr[...])
    u_ref[0] = proj[:, 0:512]
    seg = seg_ref[...]

    def head_rms(v, g):
        sq = (v * v).astype(BF16)
        n = v.shape[1]
        w = min(n, seg.shape[0])
        ss = jnp.concatenate([_dot(sq[:, c:c + w], seg[0:w, 0:w]) for c in range(0, n, w)], axis=1)
        return v * lax.rsqrt(ss * (1.0 / HEAD_DIM) + RMS_EPS) * g

    def store_heads_transposed(ref, v, with_shift_rows=False):
        for g in range(v.shape[1] // LANES):
            tg = v[:, g * LANES:(g + 1) * LANES].T
            for e in range(2):
                th = tg[e * HEAD_DIM:(e + 1) * HEAD_DIM].astype(BF16)
                ref[0, 2 * g + e, 0:HEAD_DIM] = th
                if with_shift_rows:
                    tf = th.astype(F32)
                    nrm = jnp.sqrt(jnp.sum(tf * tf, axis=0, keepdims=True)) * NORM_UP
                    pad = (K_EXT - HEAD_DIM, th.shape[1])
                    first = lax.broadcasted_iota(jnp.int32, pad, 0) == 0
                    ref[0, 2 * g + e, HEAD_DIM:K_EXT] = jnp.where(first, -nrm, 0.0).astype(BF16)

    qscale = (HEAD_DIM ** -0.5) * math.log2(math.e)
    store_heads_transposed(q_ref, head_rms(proj[:, 512:1024], qg_ref[...]) * qscale, with_shift_rows=True)

    kv = proj[:, 1024:1152]
    kiw = proj[:, 1408:1536]
    w_scale = (IDX_HEADS ** -0.5) * (IDX_DIM ** -0.5)
    wabs = jnp.abs(kiw) * w_scale
    sshape = (LANES, IDX_HEADS * IDX_DIM)
    row = lax.broadcasted_iota(jnp.int32, sshape, 0)
    col = lax.broadcasted_iota(jnp.int32, sshape, 1)
    spread = jnp.where(row == IDX_DIM + col // IDX_DIM, 1.0, 0.0).astype(BF16)
    whi = wabs.astype(BF16)
    wlo = (wabs - whi.astype(F32)).astype(BF16)
    wrep = _dot(whi, spread) + _dot(wlo, spread)
    store_heads_transposed(qs_ref, proj[:, 1152:1408] * wrep)

    lane = lax.broadcasted_iota(jnp.int32, kv.shape, 1)
    k_ref[0] = jnp.where(lane < HEAD_DIM, head_rms(kv, kg_ref[...]), 0.0)[:, 0:K_EXT].astype(BF16)
    ki_ref[0] = head_rms(kiw, ikg_ref[...])[:, 0:IDX_DIM].astype(BF16)
    t_kv = kv.T
    t_kiw = kiw.T
    sg_ref[0] = jnp.where(t_kiw[IDX_DIM:IDX_DIM + SUBLANES] >= 0, 1.0, -1.0)
    for c in range(TL_IN // TK):
        v_ref[0, c] = t_kv[HEAD_DIM:2 * HEAD_DIM, c * TK:(c + 1) * TK].astype(BF16)


def _s5_kernel(u_ref, bplo_ref, bphi_ref, cplo_ref, cphi_ref, ar_ref, ai_ref, d_ref, wglu_ref, bglu_ref,
               wps_ref, o_ref, bu_scr, st_scr):
    @pl.when(pl.program_id(0) == 0)
    def _():
        st_scr[...] = jnp.zeros_like(st_scr)

    u = jnp.transpose(u_ref[...], (1, 0, 2)).reshape(SUBLANES * TC_SCAN, SSM_WIDTH)
    ub = u.astype(BF16)
    half_w = 2 * (SSM_GROUPS // 2) * SSM_STATE
    n_re = half_w // 2

    def scan_half(h):
        c0 = h * half_w
        a_r = ar_ref[:, h * n_re:(h + 1) * n_re]
        a_i = ai_ref[:, h * n_re:(h + 1) * n_re]
        sr = st_scr[:, c0:c0 + n_re]
        si = st_scr[:, c0 + n_re:c0 + half_w]
        for t in range(TC_SCAN):
            rows = slice(t * SUBLANES, (t + 1) * SUBLANES)
            br = bu_scr[rows, c0:c0 + n_re]
            bi = bu_scr[rows, c0 + n_re:c0 + half_w]
            sr, si = a_r * sr - a_i * si + br, a_r * si + a_i * sr + bi
            bu_scr[rows, c0:c0 + n_re] = sr
            bu_scr[rows, c0 + n_re:c0 + half_w] = si
        st_scr[:, c0:c0 + n_re] = sr
        st_scr[:, c0 + n_re:c0 + half_w] = si

    bu_scr[:, 0:half_w] = _dot(ub[:, 0:256], bplo_ref[...])
    bu_scr[:, half_w:2 * half_w] = _dot(ub[:, 256:512], bphi_ref[...])
    scan_half(0)
    y_lo = _dot(bu_scr[:, 0:half_w].astype(BF16), cplo_ref[...])
    scan_half(1)
    y_hi = _dot(bu_scr[:, half_w:2 * half_w].astype(BF16), cphi_ref[...])
    y = jnp.concatenate([y_lo, y_hi], axis=1)
    y = jax.nn.gelu(y + d_ref[...] * u)
    z = _dot(y.astype(BF16), wglu_ref[...]) + bglu_ref[...]
    y = y * jax.nn.sigmoid(z)
    ps = _dot(y.astype(BF16), wps_ref[...])
    o_ref[...] = jnp.transpose(ps.reshape(TC_SCAN, SUBLANES, D_MODEL), (1, 0, 2)).astype(BF16)


def _key_to_f32(key):
    bits = jnp.where(key < 0, key & jnp.int32(0x7FFFFFFF), ~key)
    return pltpu.bitcast(bits, F32)


def _attn_kernel(qs_ref, sg_ref, q_ref, ki_ref, k_ref, vt_ref, o_ref,
                 s_scr, m_scr, l_scr, acc_scr, sp_scr, qmod_scr, kmax_scr, sb_scr):
    i = pl.program_id(1)
    n_tiles = (i * QB + QB + TK - 1) // TK
    nv = TK // SUBLANES
    shape3 = (nv, SUBLANES, QB)
    k_in_tile = (lax.broadcasted_iota(jnp.int32, shape3, 0) * SUBLANES
                 + lax.broadcasted_iota(jnp.int32, shape3, 1))
    q_pos = i * QB + lax.broadcasted_iota(jnp.int32, shape3, 2)
    ksel = float(INDEX_TOPK)

    def all_sublanes(a, op):
        for sh in (4, 2, 1):
            a = op(a, pltpu.roll(a, sh, 0))
        return a

    sg = sg_ref[0]

    def idx_body(j, carry):
        kt = ki_ref[0, pl.ds(pl.multiple_of(j * TK, TK), TK), :]
        acc = jnp.zeros((TK, QB), F32)
        for pr in range(IDX_HEADS // 2):
            x = _dot(kt, jnp.concatenate([qs_ref[0, 2 * pr], qs_ref[0, 2 * pr + 1]], axis=1))
            for e in range(2):
                hd = 2 * pr + e
                acc = acc + sg[hd:hd + 1, :] * jnp.maximum(x[:, e * QB:(e + 1) * QB], 0.0)
        vis = j * TK + k_in_tile <= q_pos
        sc = jnp.where(vis, acc.reshape(shape3), -jnp.inf).reshape(TK, QB)
        s_scr[j] = sc
        sb_scr[j] = sc.astype(BF16)
        return carry

    def for_each_tile(body):
        def pair(jj, carry):
            body(2 * jj, carry)
            return body(2 * jj + 1, carry)
        lax.fori_loop(0, n_tiles // 2, pair, 0)

        @pl.when(n_tiles % 2 == 1)
        def _():
            body(n_tiles - 1, 0)

    for_each_tile(idx_body)

    def count(pred):
        def tile(j, acc):
            hit = jnp.where(pred(s_scr[j].reshape(shape3), j), 1.0, 0.0)
            return acc + jnp.sum(hit.reshape(nv // 4, 4, SUBLANES, QB), axis=0)
        acc = lax.fori_loop(0, n_tiles // 2, lambda jj, a: tile(2 * jj + 1, tile(2 * jj, a)),
                            jnp.zeros((4, SUBLANES, QB), F32))
        acc = lax.cond(n_tiles % 2 == 1, lambda a: tile(n_tiles - 1, a), lambda a: a, acc)
        return all_sublanes(jnp.sum(acc, axis=0), jnp.add)

    every = float(2 * INDEX_TOPK * 1024)
    nv16 = TK // (2 * SUBLANES)
    one_b, zero_b = jnp.ones((), BF16), jnp.zeros((), BF16)

    def count_hi(gc):
        def tile(j, acc):
            hit = jnp.where(sb_scr[j].reshape(nv16, 2 * SUBLANES, QB) >= gc, one_b, zero_b)
            for k in range(nv16):
                acc[k % 4] = acc[k % 4] + hit[k]
            return acc
        acc = lax.fori_loop(0, n_tiles // 2, lambda jj, a: tuple(tile(2 * jj + 1, tile(2 * jj, list(a)))),
                            tuple(jnp.zeros((2 * SUBLANES, QB), BF16) for _ in range(4)))
        acc = lax.cond(n_tiles % 2 == 1, lambda a: tuple(tile(n_tiles - 1, list(a))), lambda a: a, acc)
        tot = (acc[0].astype(F32) + acc[1].astype(F32)) + (acc[2].astype(F32) + acc[3].astype(F32))
        return all_sublanes(tot[0:SUBLANES] + tot[SUBLANES:2 * SUBLANES], jnp.add)

    def grid_key(k16):
        return lax.shift_left(k16, 16) | jnp.where(k16 < 0x8000, 0xFFFF, 0)

    def hi_body(it, k16):
        cand = k16 | lax.shift_left(jnp.int32(1), 15 - it)
        g = _key_to_f32(grid_key(cand))
        n = count_hi(jnp.concatenate([g, g], axis=0).astype(BF16))
        cnt = jnp.where(lax.shift_right_logical(cand, 7) == 0, every, n)
        return jnp.where(cnt >= ksel, cand, k16)

    k16 = lax.fori_loop(0, 16, hi_body, jnp.zeros((SUBLANES, QB), jnp.int32))

    def lo_body(_, carry):
        lo, hi, cnt_lo = carry
        mid = lo + lax.shift_right_logical(hi - lo, 1)
        tc = _key_to_f32(mid)
        cnt = jnp.where(lax.shift_right_logical(mid, 23) == 0, every, count(lambda s, j: s >= tc))
        ok = cnt >= ksel
        return jnp.where(ok, mid, lo), jnp.where(ok, hi, mid), jnp.where(ok, cnt, cnt_lo)

    key, _, cnt_ge = lax.fori_loop(
        0, 17, lo_body,
        (grid_key(k16) - 0x8001, grid_key(jnp.minimum(k16 + 1, 0xFFFF)), jnp.full((SUBLANES, QB), every, F32)))
    thr = _key_to_f32(key)
    has_excess = jnp.max(cnt_ge) > ksel

    def write_bias(select):
        def body(j, carry):
            s = s_scr[j].reshape(shape3)
            vis = j * TK + k_in_tile <= q_pos
            bias = jnp.where(vis, jnp.where(select(s, j), 0.0, -jnp.inf), -jnp.inf)
            s_scr[j] = bias.reshape(TK, QB)
            return carry
        lax.fori_loop(0, n_tiles, body, 0)

    @pl.when(jnp.logical_not(has_excess))
    def _():
        write_bias(lambda s, j: s >= thr)

    @pl.when(has_excess)
    def _():
        cnt_gt = count(lambda s, j: s > thr)
        need = ksel - cnt_gt

        def jbit_body(it, jkey):
            cand = jkey | lax.shift_left(jnp.int32(1), 10 - it)
            cnt = count(lambda s, j: (s == thr) & (j * TK + k_in_tile < cand))
            return jnp.where(cnt < need, cand, jkey)

        jkey = lax.fori_loop(0, 11, jbit_body, jnp.zeros((SUBLANES, QB), jnp.int32))
        write_bias(lambda s, j: (s > thr) | ((s == thr) & (j * TK + k_in_tile <= jkey)))

    lane_k = lax.broadcasted_iota(jnp.int32, (TK, K_EXT), 1)

    def attend(get_q, k_fill):
        l_scr[...] = jnp.zeros(l_scr.shape, F32)
        acc_scr[...] = jnp.zeros(acc_scr.shape, F32)

        def body(j, carry):
            kt = k_ref[0, pl.ds(pl.multiple_of(j * TK, TK), TK), :]
            kt = jnp.where(lane_k == Q_SHIFT_ROW, jnp.full((TK, K_EXT), k_fill, F32).astype(BF16), kt)
            vt = vt_ref[0, j]
            bias = s_scr[j].reshape(shape3)
            for pr in range(N_HEADS // 2):
                sp_scr[pr] = _dot(kt, jnp.concatenate([get_q(2 * pr), get_q(2 * pr + 1)], axis=1))
            for pr in range(N_HEADS // 2):
                for e in range(2):
                    hd = 2 * pr + e
                    p = jnp.exp2(sp_scr[pr, :, e * QB:(e + 1) * QB].reshape(shape3) + bias)
                    l_scr[hd] = l_scr[hd] + jnp.sum(p, axis=0)
                    acc_scr[hd] = acc_scr[hd] + _dot(vt, p.reshape(TK, QB).astype(BF16))
            return carry

        for_each_tile(body)

    @pl.when(i == 0)
    def _():
        kf = k_ref[0].astype(F32)
        kmax_scr[0] = jnp.max(jnp.sqrt(jnp.sum(kf * kf, axis=1, keepdims=True))) * NORM_UP

    attend(lambda hd: q_ref[0, hd], kmax_scr[0])

    l_min = all_sublanes(l_scr[0], jnp.add)
    for hd in range(1, N_HEADS):
        l_min = jnp.minimum(l_min, all_sublanes(l_scr[hd], jnp.add))
    underflow = jnp.logical_not(jnp.min(l_min) >= L_FLOOR)

    @pl.when(underflow)
    def _():
        row_q = lax.broadcasted_iota(jnp.int32, (K_EXT, QB), 0)
        m_scr[...] = jnp.full(m_scr.shape, NEG_BIG, F32)

        def max_body(j, carry):
            kt = k_ref[0, pl.ds(pl.multiple_of(j * TK, TK), TK), :]
            kt = jnp.where(lane_k == Q_SHIFT_ROW, jnp.zeros((TK, K_EXT), BF16), kt)
            bias = s_scr[j].reshape(shape3)
            for hd in range(N_HEADS):
                s = _dot(kt, q_ref[0, hd]).reshape(shape3) + bias
                m_scr[hd] = jnp.maximum(m_scr[hd], all_sublanes(jnp.max(s, axis=0), jnp.maximum))
            return carry

        lax.fori_loop(0, n_tiles, max_body, 0)
        for hd in range(N_HEADS):
            shift = jnp.broadcast_to(-m_scr[hd][0:1, :], (K_EXT, QB)).astype(BF16)
            qmod_scr[hd] = jnp.where(row_q == Q_SHIFT_ROW, shift, q_ref[0, hd])
        attend(lambda hd: qmod_scr[hd], 1.0)

    outs = []
    for hd in range(N_HEADS):
        l = all_sublanes(l_scr[hd], jnp.add)
        o = acc_scr[hd].reshape(HEAD_DIM // SUBLANES, SUBLANES, QB) / l
        outs.append(o.reshape(HEAD_DIM, QB))
    o_ref[0] = jnp.concatenate(outs, axis=0).T.astype(BF16)


def _ffn_kernel(x_ref, ps_ref, ya_ref, g1_ref, wgate_ref, wpa_ref, wo_ref, g2_ref, wfg_ref, wfu_ref, wfd_ref,
                o_ref):
    x = x_ref[0]
    h = _rms(x, g1_ref[...]).astype(BF16)
    gates = _dot(h, wgate_ref[...])
    pa = _dot(ya_ref[0], wpa_ref[...])
    merged = (jax.nn.sigmoid(gates[:, 0:D_MODEL]) * ps_ref[0].astype(F32)
              + jax.nn.sigmoid(gates[:, D_MODEL:2 * D_MODEL]) * pa)
    x1 = x + _dot(merged.astype(BF16), wo_ref[...])
    h2 = _rms(x1, g2_ref[...]).astype(BF16)
    hid = jax.nn.silu(_dot(h2, wfg_ref[...])) * _dot(h2, wfu_ref[...])
    o_ref[0] = x1 + _dot(hid.astype(BF16), wfd_ref[...])


def _gate_repack_kernel(a_ref, b_ref, c_ref, o_ref):
    off = GATE_COL0 % GATE_RB
    rows = jnp.concatenate([a_ref[off:], b_ref[...], c_ref[:off]], axis=0)
    o_ref[...] = rows.T.astype(BF16)


def _const_spec(shape):
    nd = len(shape)
    return pl.BlockSpec(shape, lambda *_: (0,) * nd, pipeline_mode=pl.Buffered(1))


def _block_diag(blocks):
    g, r, c = blocks.shape
    eye = jnp.eye(g, dtype=blocks.dtype)
    return jnp.einsum('grc,gh->grhc', blocks, eye).reshape(g * r, g * c)


def kernel(x, norm1_g, w_in, A_re, A_im, log_dt, B_re, B_im, C_re, C_im, D_skip, w_glu, b_glu, q_norm_g, k_norm_g,
           idx_k_norm_g, w_proj_ssm, w_proj_attn, w_out, norm2_g, w_ffn_gate, w_ffn_up, w_ffn_down):
    bsz, seq, _ = x.shape
    assert x.shape == (8, 2048, D_MODEL) and w_in.shape[0] == 1
    G, N, P = SSM_GROUPS, SSM_STATE, SSM_GROUP
    cparams = functools.partial(pltpu.CompilerParams, vmem_limit_bytes=VMEM_LIMIT)

    w_in_t = w_in[0].T
    gblk = GATE_COL0 // GATE_RB
    spec = lambda k: pl.BlockSpec((GATE_RB, D_MODEL), lambda j: (gblk + 2 * j + k, 0))
    w_gates = pl.pallas_call(
        _gate_repack_kernel,
        grid=(2 * D_MODEL // LANES,),
        in_specs=[spec(0), spec(1), spec(2)],
        out_specs=pl.BlockSpec((D_MODEL, LANES), lambda j: (0, j)),
        out_shape=jax.ShapeDtypeStruct((D_MODEL, 2 * D_MODEL), BF16),
        name="gate_weight_repack",
    )(w_in_t, w_in_t, w_in_t)
    g1 = norm1_g[0].reshape(1, D_MODEL)
    ones64 = jnp.ones((HEAD_DIM,), F32)

    rep = lambda a: jnp.repeat(a, P, axis=0)
    gpn = jax.ShapeDtypeStruct((G * P, N), F32)
    abar_r, abar_i, bp_r, bp_i = pl.pallas_call(
        _disc_kernel, out_shape=(gpn, gpn, gpn, gpn), name="s5_discretise",
    )(rep(A_re[0]), rep(A_im[0]), rep(jnp.broadcast_to(log_dt[0].reshape(G, 1), (G, N))),
      B_re[0].transpose(0, 2, 1).reshape(G * P, N), B_im[0].transpose(0, 2, 1).reshape(G * P, N))
    abar_r, abar_i = abar_r[::P], abar_i[::P]
    bp_r, bp_i = bp_r.reshape(G, P, N), bp_i.reshape(G, P, N)
    hg = G // 2
    bp = [jnp.concatenate([_block_diag(bp_r[sl]), _block_diag(bp_i[sl])], axis=1).astype(BF16)
          for sl in (slice(0, hg), slice(hg, G))]
    c_r = C_re[0].transpose(0, 2, 1)
    c_i = C_im[0].transpose(0, 2, 1)
    cp = [jnp.concatenate([_block_diag(c_r[sl]), -_block_diag(c_i[sl])], axis=0).astype(BF16)
          for sl in (slice(0, hg), slice(hg, G))]
    ar8 = jnp.broadcast_to(abar_r.reshape(1, G * N), (SUBLANES, G * N))
    ai8 = jnp.broadcast_to(abar_i.reshape(1, G * N), (SUBLANES, G * N))

    n_in = seq // TL_IN
    u_t, q_h, qs_h, sg_t, k_n, ki_n, v_t = pl.pallas_call(
        _inproj_kernel,
        grid=(bsz, n_in),
        in_specs=[
            pl.BlockSpec((1, TL_IN, D_MODEL), lambda b, i: (b, i, 0)),
            _const_spec((1, D_MODEL)),
            _const_spec((W1_COLS, D_MODEL)),
            _const_spec((1, ATTN_WIDTH)), _const_spec((1, LANES)), _const_spec((1, LANES)),
            _const_spec((2 * LANES, 2 * LANES)),
        ],
        out_specs=[
            pl.BlockSpec((1, TL_IN, SSM_WIDTH), lambda b, i: (b, i, 0)),
            pl.BlockSpec((1, N_HEADS, K_EXT, TL_IN), lambda b, i: (b, 0, 0, i)),
            pl.BlockSpec((1, IDX_HEADS, IDX_DIM, TL_IN), lambda b, i: (b, 0, 0, i)),
            pl.BlockSpec((1, SUBLANES, TL_IN), lambda b, i: (b, 0, i)),
            pl.BlockSpec((1, TL_IN, K_EXT), lambda b, i: (b, i, 0)),
            pl.BlockSpec((1, TL_IN, IDX_DIM), lambda b, i: (b, i, 0)),
            pl.BlockSpec((1, TL_IN // TK, HEAD_DIM, TK), lambda b, i: (b, i, 0, 0)),
        ],
        out_shape=(
            jax.ShapeDtypeStruct((bsz, seq, SSM_WIDTH), F32),
            jax.ShapeDtypeStruct((bsz, N_HEADS, K_EXT, seq), BF16),
            jax.ShapeDtypeStruct((bsz, IDX_HEADS, IDX_DIM, seq), BF16),
            jax.ShapeDtypeStruct((bsz, SUBLANES, seq), F32),
            jax.ShapeDtypeStruct((bsz, seq, K_EXT), BF16),
            jax.ShapeDtypeStruct((bsz, seq, IDX_DIM), BF16),
            jax.ShapeDtypeStruct((bsz, seq // TK, HEAD_DIM, TK), BF16),
        ),
        scratch_shapes=[pltpu.VMEM((D_MODEL, W1_COLS), BF16)],
        compiler_params=cparams(dimension_semantics=("arbitrary", "arbitrary")),
        name="in_projection",
    )(x, g1, w_in_t, jnp.tile(q_norm_g[0], N_HEADS).reshape(1, ATTN_WIDTH),
      jnp.concatenate([k_norm_g[0], ones64]).reshape(1, LANES),
      jnp.concatenate([idx_k_norm_g[0], ones64]).reshape(1, LANES),
      jnp.kron(jnp.eye(2 * LANES // HEAD_DIM, dtype=BF16), jnp.ones((HEAD_DIM, HEAD_DIM), BF16)))

    rows = SUBLANES * TC_SCAN
    ps_t = pl.pallas_call(
        _s5_kernel,
        grid=(seq // TC_SCAN,),
        in_specs=[
            pl.BlockSpec((bsz, TC_SCAN, SSM_WIDTH), lambda c: (0, c, 0)),
            _const_spec((256, 2048)), _const_spec((256, 2048)),
            _const_spec((2048, 256)), _const_spec((2048, 256)),
            _const_spec((SUBLANES, G * N)), _const_spec((SUBLANES, G * N)),
            _const_spec((1, SSM_WIDTH)), _const_spec((SSM_WIDTH, SSM_WIDTH)), _const_spec((1, SSM_WIDTH)),
            _const_spec((SSM_WIDTH, D_MODEL)),
        ],
        out_specs=pl.BlockSpec((bsz, TC_SCAN, D_MODEL), lambda c: (0, c, 0)),
        out_shape=jax.ShapeDtypeStruct((bsz, seq, D_MODEL), BF16),
        scratch_shapes=[pltpu.VMEM((rows, 2 * G * N), F32), pltpu.VMEM((SUBLANES, 2 * G * N), F32)],
        compiler_params=cparams(dimension_semantics=("arbitrary",)),
        name="s5_branch",
    )(u_t, bp[0], bp[1], cp[0], cp[1], ar8, ai8,
      D_skip[0].reshape(1, SSM_WIDTH), w_glu[0].astype(BF16), b_glu[0].reshape(1, SSM_WIDTH),
      w_proj_ssm[0].astype(BF16))

    y_att = pl.pallas_call(
        _attn_kernel,
        grid=(bsz, seq // QB),
        in_specs=[
            pl.BlockSpec((1, IDX_HEADS, IDX_DIM, QB), lambda b, i: (b, 0, 0, i)),
            pl.BlockSpec((1, SUBLANES, QB), lambda b, i: (b, 0, i)),
            pl.BlockSpec((1, N_HEADS, K_EXT, QB), lambda b, i: (b, 0, 0, i)),
            pl.BlockSpec((1, seq, IDX_DIM), lambda b, i: (b, 0, 0)),
            pl.BlockSpec((1, seq, K_EXT), lambda b, i: (b, 0, 0)),
            pl.BlockSpec((1, seq // TK, HEAD_DIM, TK), lambda b, i: (b, 0, 0, 0)),
        ],
        out_specs=pl.BlockSpec((1, QB, ATTN_WIDTH), lambda b, i: (b, i, 0)),
        out_shape=jax.ShapeDtypeStruct((bsz, seq, ATTN_WIDTH), BF16),
        scratch_shapes=[pltpu.VMEM((seq // TK, TK, QB), F32),
                        pltpu.VMEM((N_HEADS, SUBLANES, QB), F32),
                        pltpu.VMEM((N_HEADS, SUBLANES, QB), F32),
                        pltpu.VMEM((N_HEADS, HEAD_DIM, QB), F32),
                        pltpu.VMEM((N_HEADS // 2, TK, 2 * QB), F32),
                        pltpu.VMEM((N_HEADS, K_EXT, QB), BF16),
                        pltpu.SMEM((1,), F32),
                        pltpu.VMEM((seq // TK, TK, QB), BF16)],
        compiler_params=cparams(dimension_semantics=("parallel", "arbitrary")),
        name="sparse_attention",
    )(qs_h, sg_t, q_h, ki_n, k_n, v_t)

    out = pl.pallas_call(
        _ffn_kernel,
        grid=(bsz, seq // TL_FFN),
        in_specs=[
            pl.BlockSpec((1, TL_FFN, D_MODEL), lambda b, i: (b, i, 0)),
            pl.BlockSpec((1, TL_FFN, D_MODEL), lambda b, i: (b, i, 0)),
            pl.BlockSpec((1, TL_FFN, ATTN_WIDTH), lambda b, i: (b, i, 0)),
            _const_spec((1, D_MODEL)),
            _const_spec((D_MODEL, 2 * D_MODEL)),
            _const_spec((ATTN_WIDTH, D_MODEL)),
            _const_spec((D_MODEL, D_MODEL)),
            _const_spec((1, D_MODEL)),
            _const_spec((D_MODEL, D_FF)), _const_spec((D_MODEL, D_FF)), _const_spec((D_FF, D_MODEL)),
        ],
        out_specs=pl.BlockSpec((1, TL_FFN, D_MODEL), lambda b, i: (b, i, 0)),
        out_shape=jax.ShapeDtypeStruct((bsz, seq, D_MODEL), F32),
        compiler_params=cparams(dimension_semantics=("parallel", "parallel")),
        name="merge_ffn",
    )(x, ps_t, y_att, g1, w_gates, w_proj_attn[0].astype(BF16),
      w_out[0].astype(BF16), norm2_g[0].reshape(1, D_MODEL), w_ffn_gate[0].astype(BF16),
      w_ffn_up[0].astype(BF16), w_ffn_down[0].astype(BF16))
    return out
```

```python
import functools
import math

import jax
import jax.numpy as jnp
from jax import lax
from jax.experimental import pallas as pl
from jax.experimental.pallas import tpu as pltpu

F32 = jnp.float32
BF16 = jnp.bfloat16

D_MODEL = 1024
SSM_WIDTH = 512
SSM_GROUP = 16
SSM_GROUPS = 32
SSM_STATE = 64
N_HEADS = 8
HEAD_DIM = 64
ATTN_WIDTH = 512
IDX_HEADS = 4
IDX_DIM = 64
INDEX_TOPK = 256
D_FF = 2816
RMS_EPS = 1e-6

LANES = 128
SUBLANES = 8
VMEM_LIMIT = 56 * 1024 * 1024

TL_IN = 512
TC_SCAN = 128
QB = 256
TK = 256
TL_FFN = 256

W1_COLS = 1536
GATE_COL0 = 1476
GATE_RB = 64
NEG_BIG = -1e30
K_EXT = 80
Q_SHIFT_ROW = 64
NORM_UP = 1.0 + 2.0 ** -7
L_FLOOR = 2.0 ** -100
HALVINGS_PER_CHECK = 3
assert QB % LANES == 0


def _dot(a, b):
    return jnp.dot(a, b, preferred_element_type=F32)


def _rms(x, g):
    return x * lax.rsqrt(jnp.mean(x * x, axis=-1, keepdims=True) + RMS_EPS) * g


def _disc_kernel(are_ref, aim_ref, ldt_ref, bre_ref, bim_ref, abr_ref, abi_ref, bpr_ref, bpi_ref):
    ar = are_ref[...]
    ai = aim_ref[...]
    dt = jnp.exp(ldt_ref[...])
    mag = jnp.exp(ar * dt)
    abar_r = mag * jnp.cos(ai * dt)
    abar_i = mag * jnp.sin(ai * dt)
    den = ar * ar + ai * ai
    nr = abar_r - 1.0
    coef_r = (nr * ar + abar_i * ai) / den
    coef_i = (abar_i * ar - nr * ai) / den
    abr_ref[...] = abar_r
    abi_ref[...] = abar_i
    br = bre_ref[...]
    bi = bim_ref[...]
    bpr_ref[...] = coef_r * br - coef_i * bi
    bpi_ref[...] = coef_r * bi + coef_i * br


def _inproj_kernel(x_ref, g1_ref, w_ref, qg_ref, kg_ref, ikg_ref, seg_ref,
                   u_ref, q_ref, qs_ref, sg_ref, k_ref, ki_ref, v_ref, w_scr):
    @pl.when(jnp.logical_and(pl.program_id(0) == 0, pl.program_id(1) == 0))
    def _():
        w_scr[...] = w_ref[...].T.astype(BF16)

    x = x_ref[0]
    h = _rms(x, g1_ref[...]).astype(BF16)
    proj = _dot(h, w_scr[...])
    u_ref[0] = proj[:, 0:512]
    seg = seg_ref[...]

    def head_rms(v, g):
        sq = (v * v).astype(BF16)
        n = v.shape[1]
        w = min(n, seg.shape[0])
        ss = jnp.concatenate([_dot(sq[:, c:c + w], seg[0:w, 0:w]) for c in range(0, n, w)], axis=1)
        return v * lax.rsqrt(ss * (1.0 / HEAD_DIM) + RMS_EPS) * g

    def store_heads_transposed(ref, v, with_shift_rows=False):
        for g in range(v.shape[1] // LANES):
            tg = v[:, g * LANES:(g + 1) * LANES].T
            for e in range(2):
                th = tg[e * HEAD_DIM:(e + 1) * HEAD_DIM].astype(BF16)
                ref[0, 2 * g + e, 0:HEAD_DIM] = th
                if with_shift_rows:
                    tf = th.astype(F32)
                    nrm = jnp.sqrt(jnp.sum(tf * tf, axis=0, keepdims=True)) * NORM_UP
                    pad = (K_EXT - HEAD_DIM, th.shape[1])
                    first = lax.broadcasted_iota(jnp.int32, pad, 0) == 0
                    ref[0, 2 * g + e, HEAD_DIM:K_EXT] = jnp.where(first, -nrm, 0.0).astype(BF16)

    qscale = (HEAD_DIM ** -0.5) * math.log2(math.e)
    store_heads_transposed(q_ref, head_rms(proj[:, 512:1024], qg_ref[...]) * qscale, with_shift_rows=True)

    kv = proj[:, 1024:1152]
    kiw = proj[:, 1408:1536]
    w_scale = (IDX_HEADS ** -0.5) * (IDX_DIM ** -0.5)
    wabs = jnp.abs(kiw) * w_scale
    sshape = (LANES, IDX_HEADS * IDX_DIM)
    row = lax.broadcasted_iota(jnp.int32, sshape, 0)
    col = lax.broadcasted_iota(jnp.int32, sshape, 1)
    spread = jnp.where(row == IDX_DIM + col // IDX_DIM, 1.0, 0.0).astype(BF16)
    whi = wabs.astype(BF16)
    wlo = (wabs - whi.astype(F32)).astype(BF16)
    wrep = _dot(whi, spread) + _dot(wlo, spread)
    store_heads_transposed(qs_ref, proj[:, 1152:1408] * wrep)

    lane = lax.broadcasted_iota(jnp.int32, kv.shape, 1)
    k_ref[0] = jnp.where(lane < HEAD_DIM, head_rms(kv, kg_ref[...]), 0.0)[:, 0:K_EXT].astype(BF16)
    ki_ref[0] = head_rms(kiw, ikg_ref[...])[:, 0:IDX_DIM].astype(BF16)
    t_kv = kv.T
    t_kiw = kiw.T
    sg_ref[0] = jnp.where(t_kiw[IDX_DIM:IDX_DIM + SUBLANES] >= 0, 1.0, -1.0)
    for c in range(TL_IN // TK):
        v_ref[0, c] = t_kv[HEAD_DIM:2 * HEAD_DIM, c * TK:(c + 1) * TK].astype(BF16)


def _s5_kernel(u_ref, bplo_ref, bphi_ref, cplo_ref, cphi_ref, ar_ref, ai_ref, d_ref, wglu_ref, bglu_ref,
               wps_ref, o_ref, bu_scr, st_scr):
    @pl.when(pl.program_id(0) == 0)
    def _():
        st_scr[...] = jnp.zeros_like(st_scr)

    u = jnp.transpose(u_ref[...], (1, 0, 2)).reshape(SUBLANES * TC_SCAN, SSM_WIDTH)
    ub = u.astype(BF16)
    half_w = 2 * (SSM_GROUPS // 2) * SSM_STATE
    n_re = half_w // 2

    def scan_half(h):
        c0 = h * half_w
        a_r = ar_ref[:, h * n_re:(h + 1) * n_re]
        a_i = ai_ref[:, h * n_re:(h + 1) * n_re]
        sr = st_scr[:, c0:c0 + n_re]
        si = st_scr[:, c0 + n_re:c0 + half_w]
        for t in range(TC_SCAN):
            rows = slice(t * SUBLANES, (t + 1) * SUBLANES)
            br = bu_scr[rows, c0:c0 + n_re]
            bi = bu_scr[rows, c0 + n_re:c0 + half_w]
            sr, si = a_r * sr - a_i * si + br, a_r * si + a_i * sr + bi
            bu_scr[rows, c0:c0 + n_re] = sr
            bu_scr[rows, c0 + n_re:c0 + half_w] = si
        st_scr[:, c0:c0 + n_re] = sr
        st_scr[:, c0 + n_re:c0 + half_w] = si

    bu_scr[:, 0:half_w] = _dot(ub[:, 0:256], bplo_ref[...])
    bu_scr[:, half_w:2 * half_w] = _dot(ub[:, 256:512], bphi_ref[...])
    scan_half(0)
    y_lo = _dot(bu_scr[:, 0:half_w].astype(BF16), cplo_ref[...])
    scan_half(1)
    y_hi = _dot(bu_scr[:, half_w:2 * half_w].astype(BF16), cphi_ref[...])
    y = jnp.concatenate([y_lo, y_hi], axis=1)
    y = jax.nn.gelu(y + d_ref[...] * u)
    z = _dot(y.astype(BF16), wglu_ref[...]) + bglu_ref[...]
    y = y * jax.nn.sigmoid(z)
    ps = _dot(y.astype(BF16), wps_ref[...])
    o_ref[...] = jnp.transpose(ps.reshape(TC_SCAN, SUBLANES, D_MODEL), (1, 0, 2)).astype(BF16)


def _key_to_f32(key):
    bits = jnp.where(key < 0, key & jnp.int32(0x7FFFFFFF), ~key)
    return pltpu.bitcast(bits, F32)


def _attn_kernel(qs_ref, sg_ref, q_ref, ki_ref, k_ref, vt_ref, o_ref,
                 s_scr, m_scr, l_scr, acc_scr, sp_scr, qmod_scr, kmax_scr, sb_scr):
    i = pl.program_id(1)
    n_tiles = (i * QB + QB + TK - 1) // TK
    nv = TK // SUBLANES
    shape3 = (nv, SUBLANES, QB)
    k_in_tile = (lax.broadcasted_iota(jnp.int32, shape3, 0) * SUBLANES
                 + lax.broadcasted_iota(jnp.int32, shape3, 1))
    q_pos = i * QB + lax.broadcasted_iota(jnp.int32, shape3, 2)
    ksel = float(INDEX_TOPK)

    def all_sublanes(a, op):
        for sh in (4, 2, 1):
            a = op(a, pltpu.roll(a, sh, 0))
        return a

    sg = sg_ref[0]

    def idx_body(j, carry):
        kt = ki_ref[0, pl.ds(pl.multiple_of(j * TK, TK), TK), :]
        acc = jnp.zeros((TK, QB), F32)
        for pr in range(IDX_HEADS // 2):
            x = _dot(kt, jnp.concatenate([qs_ref[0, 2 * pr], qs_ref[0, 2 * pr + 1]], axis=1))
            for e in range(2):
                hd = 2 * pr + e
                acc = acc + sg[hd:hd + 1, :] * jnp.maximum(x[:, e * QB:(e + 1) * QB], 0.0)
        vis = j * TK + k_in_tile <= q_pos
        sc = jnp.where(vis, acc.reshape(shape3), -jnp.inf).reshape(TK, QB)
        s_scr[j] = sc
        sb_scr[j] = sc.astype(BF16)
        return carry

    def for_each_tile(body):
        def pair(jj, carry):
            body(2 * jj, carry)
            return body(2 * jj + 1, carry)
        lax.fori_loop(0, n_tiles // 2, pair, 0)

        @pl.when(n_tiles % 2 == 1)
        def _():
            body(n_tiles - 1, 0)

    for_each_tile(idx_body)

    def count(pred):
        def tile(j, acc):
            hit = jnp.where(pred(s_scr[j].reshape(shape3), j), 1.0, 0.0)
            return acc + jnp.sum(hit.reshape(nv // 4, 4, SUBLANES, QB), axis=0)
        acc = lax.fori_loop(0, n_tiles // 2, lambda jj, a: tile(2 * jj + 1, tile(2 * jj, a)),
                            jnp.zeros((4, SUBLANES, QB), F32))
        acc = lax.cond(n_tiles % 2 == 1, lambda a: tile(n_tiles - 1, a), lambda a: a, acc)
        return all_sublanes(jnp.sum(acc, axis=0), jnp.add)

    every = float(2 * INDEX_TOPK * 1024)
    nv16 = TK // (2 * SUBLANES)
    one_b, zero_b = jnp.ones((), BF16), jnp.zeros((), BF16)

    def count_hi(gc):
        def tile(j, acc):
            hit = jnp.where(sb_scr[j].reshape(nv16, 2 * SUBLANES, QB) >= gc, one_b, zero_b)
            for k in range(nv16):
                acc[k % 4] = acc[k % 4] + hit[k]
            return acc
        acc = lax.fori_loop(0, n_tiles // 2, lambda jj, a: tuple(tile(2 * jj + 1, tile(2 * jj, list(a)))),
                            tuple(jnp.zeros((2 * SUBLANES, QB), BF16) for _ in range(4)))
        acc = lax.cond(n_tiles % 2 == 1, lambda a: tuple(tile(n_tiles - 1, list(a))), lambda a: a, acc)
        tot = (acc[0].astype(F32) + acc[1].astype(F32)) + (acc[2].astype(F32) + acc[3].astype(F32))
        return all_sublanes(tot[0:SUBLANES] + tot[SUBLANES:2 * SUBLANES], jnp.add)

    def grid_key(k16):
        return lax.shift_left(k16, 16) | jnp.where(k16 < 0x8000, 0xFFFF, 0)

    def hi_body(it, k16):
        cand = k16 | lax.shift_left(jnp.int32(1), 15 - it)
        g = _key_to_f32(grid_key(cand))
        n = count_hi(jnp.concatenate([g, g], axis=0).astype(BF16))
        cnt = jnp.where(lax.shift_right_logical(cand, 7) == 0, every, n)
        return jnp.where(cnt >= ksel, cand, k16)

    k16 = lax.fori_loop(0, 16, hi_body, jnp.zeros((SUBLANES, QB), jnp.int32))

    def lo_body(carry):
        it, _, lo, hi, cnt_lo = carry
        for _ in range(HALVINGS_PER_CHECK):
            mid = lo + lax.shift_right_logical(hi - lo, 1)
            tc = _key_to_f32(mid)
            cnt = jnp.where(lax.shift_right_logical(mid, 23) == 0, every, count(lambda s, j: s >= tc))
            ok = cnt >= ksel
            lo, hi, cnt_lo = jnp.where(ok, mid, lo), jnp.where(ok, hi, mid), jnp.where(ok, cnt, cnt_lo)
        settled = jnp.where(cnt_lo == ksel, 1.0, jnp.where(hi - lo == 1, 1.0, 0.0))
        return it + 1, jnp.min(settled) > 0.5, lo, hi, cnt_lo

    n_checks = -(-17 // HALVINGS_PER_CHECK)
    _, _, key, _, cnt_ge = lax.while_loop(
        lambda c: jnp.logical_and(c[0] < n_checks, jnp.logical_not(c[1])), lo_body,
        (jnp.int32(0), jnp.bool_(False), grid_key(k16) - 0x8001, grid_key(jnp.minimum(k16 + 1, 0xFFFF)),
         jnp.full((SUBLANES, QB), every, F32)))
    thr = _key_to_f32(key)
    has_excess = jnp.max(cnt_ge) > ksel

    def write_bias(select):
        def body(j, carry):
            s = s_scr[j].reshape(shape3)
            vis = j * TK + k_in_tile <= q_pos
            bias = jnp.where(vis, jnp.where(select(s, j), 0.0, -jnp.inf), -jnp.inf)
            s_scr[j] = bias.reshape(TK, QB)
            return carry
        lax.fori_loop(0, n_tiles, body, 0)

    @pl.when(jnp.logical_not(has_excess))
    def _():
        write_bias(lambda s, j: s >= thr)

    @pl.when(has_excess)
    def _():
        cnt_gt = count(lambda s, j: s > thr)
        need = ksel - cnt_gt

        def jbit_body(it, jkey):
            cand = jkey | lax.shift_left(jnp.int32(1), 10 - it)
            cnt = count(lambda s, j: (s == thr) & (j * TK + k_in_tile < cand))
            return jnp.where(cnt < need, cand, jkey)

        jkey = lax.fori_loop(0, 11, jbit_body, jnp.zeros((SUBLANES, QB), jnp.int32))
        write_bias(lambda s, j: (s > thr) | ((s == thr) & (j * TK + k_in_tile <= jkey)))

    lane_k = lax.broadcasted_iota(jnp.int32, (TK, K_EXT), 1)

    def attend(get_q, k_fill):
        l_scr[...] = jnp.zeros(l_scr.shape, F32)
        acc_scr[...] = jnp.zeros(acc_scr.shape, F32)

        def body(j, carry):
            kt = k_ref[0, pl.ds(pl.multiple_of(j * TK, TK), TK), :]
            kt = jnp.where(lane_k == Q_SHIFT_ROW, jnp.full((TK, K_EXT), k_fill, F32).astype(BF16), kt)
            vt = vt_ref[0, j]
            bias = s_scr[j].reshape(shape3)
            for pr in range(N_HEADS // 2):
                sp_scr[pr] = _dot(kt, jnp.concatenate([get_q(2 * pr), get_q(2 * pr + 1)], axis=1))
            for pr in range(N_HEADS // 2):
                for e in range(2):
                    hd = 2 * pr + e
                    p = jnp.exp2(sp_scr[pr, :, e * QB:(e + 1) * QB].reshape(shape3) + bias)
                    l_scr[hd] = l_scr[hd] + jnp.sum(p, axis=0)
                    acc_scr[hd] = acc_scr[hd] + _dot(vt, p.reshape(TK, QB).astype(BF16))
            return carry

        for_each_tile(body)

    @pl.when(i == 0)
    def _():
        kf = k_ref[0].astype(F32)
        kmax_scr[0] = jnp.max(jnp.sqrt(jnp.sum(kf * kf, axis=1, keepdims=True))) * NORM_UP

    attend(lambda hd: q_ref[0, hd], kmax_scr[0])

    l_min = all_sublanes(l_scr[0], jnp.add)
    for hd in range(1, N_HEADS):
        l_min = jnp.minimum(l_min, all_sublanes(l_scr[hd], jnp.add))
    underflow = jnp.logical_not(jnp.min(l_min) >= L_FLOOR)

    @pl.when(underflow)
    def _():
        row_q = lax.broadcasted_iota(jnp.int32, (K_EXT, QB), 0)
        m_scr[...] = jnp.full(m_scr.shape, NEG_BIG, F32)

        def max_body(j, carry):
            kt = k_ref[0, pl.ds(pl.multiple_of(j * TK, TK), TK), :]
            kt = jnp.where(lane_k == Q_SHIFT_ROW, jnp.zeros((TK, K_EXT), BF16), kt)
            bias = s_scr[j].reshape(shape3)
            for hd in range(N_HEADS):
                s = _dot(kt, q_ref[0, hd]).reshape(shape3) + bias
                m_scr[hd] = jnp.maximum(m_scr[hd], all_sublanes(jnp.max(s, axis=0), jnp.maximum))
            return carry

        lax.fori_loop(0, n_tiles, max_body, 0)
        for hd in range(N_HEADS):
            shift = jnp.broadcast_to(-m_scr[hd][0:1, :], (K_EXT, QB)).astype(BF16)
            qmod_scr[hd] = jnp.where(row_q == Q_SHIFT_ROW, shift, q_ref[0, hd])
        attend(lambda hd: qmod_scr[hd], 1.0)

    outs = []
    for hd in range(N_HEADS):
        l = all_sublanes(l_scr[hd], jnp.add)
        o = acc_scr[hd].reshape(HEAD_DIM // SUBLANES, SUBLANES, QB) / l
        outs.append(o.reshape(HEAD_DIM, QB))
    o_ref[0] = jnp.concatenate(outs, axis=0).T.astype(BF16)


def _ffn_kernel(x_ref, ps_ref, ya_ref, g1_ref, wgate_ref, wpa_ref, wo_ref, g2_ref, wfg_ref, wfu_ref, wfd_ref,
                o_ref):
    x = x_ref[0]
    h = _rms(x, g1_ref[...]).astype(BF16)
    gates = _dot(h, wgate_ref[...])
    pa = _dot(ya_ref[0], wpa_ref[...])
    merged = (jax.nn.sigmoid(gates[:, 0:D_MODEL]) * ps_ref[0].astype(F32)
              + jax.nn.sigmoid(gates[:, D_MODEL:2 * D_MODEL]) * pa)
    x1 = x + _dot(merged.astype(BF16), wo_ref[...])
    h2 = _rms(x1, g2_ref[...]).astype(BF16)
    hid = jax.nn.silu(_dot(h2, wfg_ref[...])) * _dot(h2, wfu_ref[...])
    o_ref[0] = x1 + _dot(hid.astype(BF16), wfd_ref[...])


def _gate_repack_kernel(a_ref, b_ref, c_ref, o_ref):
    off = GATE_COL0 % GATE_RB
    rows = jnp.concatenate([a_ref[off:], b_ref[...], c_ref[:off]], axis=0)
    o_ref[...] = rows.T.astype(BF16)


def _const_spec(shape):
    nd = len(shape)
    return pl.BlockSpec(shape, lambda *_: (0,) * nd, pipeline_mode=pl.Buffered(1))


def _block_diag(blocks):
    g, r, c = blocks.shape
    eye = jnp.eye(g, dtype=blocks.dtype)
    return jnp.einsum('grc,gh->grhc', blocks, eye).reshape(g * r, g * c)


def kernel(x, norm1_g, w_in, A_re, A_im, log_dt, B_re, B_im, C_re, C_im, D_skip, w_glu, b_glu, q_norm_g, k_norm_g,
           idx_k_norm_g, w_proj_ssm, w_proj_attn, w_out, norm2_g, w_ffn_gate, w_ffn_up, w_ffn_down):
    bsz, seq, _ = x.shape
    assert x.shape == (8, 2048, D_MODEL) and w_in.shape[0] == 1
    G, N, P = SSM_GROUPS, SSM_STATE, SSM_GROUP
    cparams = functools.partial(pltpu.CompilerParams, vmem_limit_bytes=VMEM_LIMIT)

    w_in_t = w_in[0].T
    gblk = GATE_COL0 // GATE_RB
    spec = lambda k: pl.BlockSpec((GATE_RB, D_MODEL), lambda j: (gblk + 2 * j + k, 0))
    w_gates = pl.pallas_call(
        _gate_repack_kernel,
        grid=(2 * D_MODEL // LANES,),
        in_specs=[spec(0), spec(1), spec(2)],
        out_specs=pl.BlockSpec((D_MODEL, LANES), lambda j: (0, j)),
        out_shape=jax.ShapeDtypeStruct((D_MODEL, 2 * D_MODEL), BF16),
        name="gate_weight_repack",
    )(w_in_t, w_in_t, w_in_t)
    g1 = norm1_g[0].reshape(1, D_MODEL)
    ones64 = jnp.ones((HEAD_DIM,), F32)

    rep = lambda a: jnp.repeat(a, P, axis=0)
    gpn = jax.ShapeDtypeStruct((G * P, N), F32)
    abar_r, abar_i, bp_r, bp_i = pl.pallas_call(
        _disc_kernel, out_shape=(gpn, gpn, gpn, gpn), name="s5_discretise",
    )(rep(A_re[0]), rep(A_im[0]), rep(jnp.broadcast_to(log_dt[0].reshape(G, 1), (G, N))),
      B_re[0].transpose(0, 2, 1).reshape(G * P, N), B_im[0].transpose(0, 2, 1).reshape(G * P, N))
    abar_r, abar_i = abar_r[::P], abar_i[::P]
    bp_r, bp_i = bp_r.reshape(G, P, N), bp_i.reshape(G, P, N)
    hg = G // 2
    bp = [jnp.concatenate([_block_diag(bp_r[sl]), _block_diag(bp_i[sl])], axis=1).astype(BF16)
          for sl in (slice(0, hg), slice(hg, G))]
    c_r = C_re[0].transpose(0, 2, 1)
    c_i = C_im[0].transpose(0, 2, 1)
    cp = [jnp.concatenate([_block_diag(c_r[sl]), -_block_diag(c_i[sl])], axis=0).astype(BF16)
          for sl in (slice(0, hg), slice(hg, G))]
    ar8 = jnp.broadcast_to(abar_r.reshape(1, G * N), (SUBLANES, G * N))
    ai8 = jnp.broadcast_to(abar_i.reshape(1, G * N), (SUBLANES, G * N))

    n_in = seq // TL_IN
    u_t, q_h, qs_h, sg_t, k_n, ki_n, v_t = pl.pallas_call(
        _inproj_kernel,
        grid=(bsz, n_in),
        in_specs=[
            pl.BlockSpec((1, TL_IN, D_MODEL), lambda b, i: (b, i, 0)),
            _const_spec((1, D_MODEL)),
            _const_spec((W1_COLS, D_MODEL)),
            _const_spec((1, ATTN_WIDTH)), _const_spec((1, LANES)), _const_spec((1, LANES)),
            _const_spec((2 * LANES, 2 * LANES)),
        ],
        out_specs=[
            pl.BlockSpec((1, TL_IN, SSM_WIDTH), lambda b, i: (b, i, 0)),
            pl.BlockSpec((1, N_HEADS, K_EXT, TL_IN), lambda b, i: (b, 0, 0, i)),
            pl.BlockSpec((1, IDX_HEADS, IDX_DIM, TL_IN), lambda b, i: (b, 0, 0, i)),
            pl.BlockSpec((1, SUBLANES, TL_IN), lambda b, i: (b, 0, i)),
            pl.BlockSpec((1, TL_IN, K_EXT), lambda b, i: (b, i, 0)),
            pl.BlockSpec((1, TL_IN, IDX_DIM), lambda b, i: (b, i, 0)),
            pl.BlockSpec((1, TL_IN // TK, HEAD_DIM, TK), lambda b, i: (b, i, 0, 0)),
        ],
        out_shape=(
            jax.ShapeDtypeStruct((bsz, seq, SSM_WIDTH), F32),
            jax.ShapeDtypeStruct((bsz, N_HEADS, K_EXT, seq), BF16),
            jax.ShapeDtypeStruct((bsz, IDX_HEADS, IDX_DIM, seq), BF16),
            jax.ShapeDtypeStruct((bsz, SUBLANES, seq), F32),
            jax.ShapeDtypeStruct((bsz, seq, K_EXT), BF16),
            jax.ShapeDtypeStruct((bsz, seq, IDX_DIM), BF16),
            jax.ShapeDtypeStruct((bsz, seq // TK, HEAD_DIM, TK), BF16),
        ),
        scratch_shapes=[pltpu.VMEM((D_MODEL, W1_COLS), BF16)],
        compiler_params=cparams(dimension_semantics=("arbitrary", "arbitrary")),
        name="in_projection",
    )(x, g1, w_in_t, jnp.tile(q_norm_g[0], N_HEADS).reshape(1, ATTN_WIDTH),
      jnp.concatenate([k_norm_g[0], ones64]).reshape(1, LANES),
      jnp.concatenate([idx_k_norm_g[0], ones64]).reshape(1, LANES),
      jnp.kron(jnp.eye(2 * LANES // HEAD_DIM, dtype=BF16), jnp.ones((HEAD_DIM, HEAD_DIM), BF16)))

    rows = SUBLANES * TC_SCAN
    ps_t = pl.pallas_call(
        _s5_kernel,
        grid=(seq // TC_SCAN,),
        in_specs=[
            pl.BlockSpec((bsz, TC_SCAN, SSM_WIDTH), lambda c: (0, c, 0)),
            _const_spec((256, 2048)), _const_spec((256, 2048)),
            _const_spec((2048, 256)), _const_spec((2048, 256)),
            _const_spec((SUBLANES, G * N)), _const_spec((SUBLANES, G * N)),
            _const_spec((1, SSM_WIDTH)), _const_spec((SSM_WIDTH, SSM_WIDTH)), _const_spec((1, SSM_WIDTH)),
            _const_spec((SSM_WIDTH, D_MODEL)),
        ],
        out_specs=pl.BlockSpec((bsz, TC_SCAN, D_MODEL), lambda c: (0, c, 0)),
        out_shape=jax.ShapeDtypeStruct((bsz, seq, D_MODEL), BF16),
        scratch_shapes=[pltpu.VMEM((rows, 2 * G * N), F32), pltpu.VMEM((SUBLANES, 2 * G * N), F32)],
        compiler_params=cparams(dimension_semantics=("arbitrary",)),
        name="s5_branch",
    )(u_t, bp[0], bp[1], cp[0], cp[1], ar8, ai8,
      D_skip[0].reshape(1, SSM_WIDTH), w_glu[0].astype(BF16), b_glu[0].reshape(1, SSM_WIDTH),
      w_proj_ssm[0].astype(BF16))

    y_att = pl.pallas_call(
        _attn_kernel,
        grid=(bsz, seq // QB),
        in_specs=[
            pl.BlockSpec((1, IDX_HEADS, IDX_DIM, QB), lambda b, i: (b, 0, 0, i)),
            pl.BlockSpec((1, SUBLANES, QB), lambda b, i: (b, 0, i)),
            pl.BlockSpec((1, N_HEADS, K_EXT, QB), lambda b, i: (b, 0, 0, i)),
            pl.BlockSpec((1, seq, IDX_DIM), lambda b, i: (b, 0, 0)),
            pl.BlockSpec((1, seq, K_EXT), lambda b, i: (b, 0, 0)),
            pl.BlockSpec((1, seq // TK, HEAD_DIM, TK), lambda b, i: (b, 0, 0, 0)),
        ],
        out_specs=pl.BlockSpec((1, QB, ATTN_WIDTH), lambda b, i: (b, i, 0)),
        out_shape=jax.ShapeDtypeStruct((bsz, seq, ATTN_WIDTH), BF16),
        scratch_shapes=[pltpu.VMEM((seq // TK, TK, QB), F32),
                        pltpu.VMEM((N_HEADS, SUBLANES, QB), F32),
                        pltpu.VMEM((N_HEADS, SUBLANES, QB), F32),
                        pltpu.VMEM((N_HEADS, HEAD_DIM, QB), F32),
                        pltpu.VMEM((N_HEADS // 2, TK, 2 * QB), F32),
                        pltpu.VMEM((N_HEADS, K_EXT, QB), BF16),
                        pltpu.SMEM((1,), F32),
                        pltpu.VMEM((seq // TK, TK, QB), BF16)],
        compiler_params=cparams(dimension_semantics=("parallel", "arbitrary")),
        name="sparse_attention",
    )(qs_h, sg_t, q_h, ki_n, k_n, v_t)

    out = pl.pallas_call(
        _ffn_kernel,
        grid=(bsz, seq // TL_FFN),
        in_specs=[
            pl.BlockSpec((1, TL_FFN, D_MODEL), lambda b, i: (b, i, 0)),
            pl.BlockSpec((1, TL_FFN, D_MODEL), lambda b, i: (b, i, 0)),
            pl.BlockSpec((1, TL_FFN, ATTN_WIDTH), lambda b, i: (b, i, 0)),
            _const_spec((1, D_MODEL)),
            _const_spec((D_MODEL, 2 * D_MODEL)),
            _const_spec((ATTN_WIDTH, D_MODEL)),
            _const_spec((D_MODEL, D_MODEL)),
            _const_spec((1, D_MODEL)),
            _const_spec((D_MODEL, D_FF)), _const_spec((D_MODEL, D_FF)), _const_spec((D_FF, D_MODEL)),
        ],
        out_specs=pl.BlockSpec((1, TL_FFN, D_MODEL), lambda b, i: (b, i, 0)),
        out_shape=jax.ShapeDtypeStruct((bsz, seq, D_MODEL), F32),
        compiler_params=cparams(dimension_semantics=("parallel", "parallel")),
        name="merge_ffn",
    )(x, ps_t, y_att, g1, w_gates, w_proj_attn[0].astype(BF16),
      w_out[0].astype(BF16), norm2_g[0].reshape(1, D_MODEL), w_ffn_gate[0].astype(BF16),
      w_ffn_up[0].astype(BF16), w_ffn_down[0].astype(BF16))
    return out
```

```python
import functools
import math

import jax
import jax.numpy as jnp
from jax import lax
from jax.experimental import pallas as pl
from jax.experimental.pallas import tpu as pltpu

F32 = jnp.float32
BF16 = jnp.bfloat16

D_MODEL = 1024
SSM_WIDTH = 512
SSM_GROUP = 16
SSM_GROUPS = 32
SSM_STATE = 64
N_HEADS = 8
HEAD_DIM = 64
ATTN_WIDTH = 512
IDX_HEADS = 4
IDX_DIM = 64
INDEX_TOPK = 256
D_FF = 2816
RMS_EPS = 1e-6

LANES = 128
SUBLANES = 8
VMEM_LIMIT = 56 * 1024 * 1024

TL_IN = 512
TC_SCAN = 128
QB = 256
TK = 256
TL_FFN = 256

W1_COLS = 1536
GATE_COL0 = 1476
GATE_RB = 64
NEG_BIG = -1e30
K_EXT = 80
Q_SHIFT_ROW = 64
NORM_UP = 1.0 + 2.0 ** -7
L_FLOOR = 2.0 ** -100
assert QB % LANES == 0


def _dot(a, b):
    return jnp.dot(a, b, preferred_element_type=F32)


def _rms(x, g):
    return x * lax.rsqrt(jnp.mean(x * x, axis=-1, keepdims=True) + RMS_EPS) * g


def _disc_kernel(are_ref, aim_ref, ldt_ref, bre_ref, bim_ref, abr_ref, abi_ref, bpr_ref, bpi_ref):
    ar = are_ref[...]
    ai = aim_ref[...]
    dt = jnp.exp(ldt_ref[...])
    mag = jnp.exp(ar * dt)
    abar_r = mag * jnp.cos(ai * dt)
    abar_i = mag * jnp.sin(ai * dt)
    den = ar * ar + ai * ai
    nr = abar_r - 1.0
    coef_r = (nr * ar + abar_i * ai) / den
    coef_i = (abar_i * ar - nr * ai) / den
    abr_ref[...] = abar_r
    abi_ref[...] = abar_i
    br = bre_ref[...]
    bi = bim_ref[...]
    bpr_ref[...] = coef_r * br - coef_i * bi
    bpi_ref[...] = coef_r * bi + coef_i * br


def _inproj_kernel(x_ref, g1_ref, w_ref, qg_ref, kg_ref, ikg_ref, seg_ref,
                   u_ref, q_ref, qs_ref, sg_ref, k_ref, ki_ref, v_ref, w_scr):
    @pl.when(jnp.logical_and(pl.program_id(0) == 0, pl.program_id(1) == 0))
    def _():
        w_scr[...] = w_ref[...].T.astype(BF16)

    x = x_ref[0]
    h = _rms(x, g1_ref[...]).astype(BF16)
    proj = _dot(h, w_scr[...])
    u_ref[0] = proj[:, 0:512]
    seg = seg_ref[...]

    def head_rms(v, g):
        sq = (v * v).astype(BF16)
        n = v.shape[1]
        w = min(n, seg.shape[0])
        ss = jnp.concatenate([_dot(sq[:, c:c + w], seg[0:w, 0:w]) for c in range(0, n, w)], axis=1)
        return v * lax.rsqrt(ss * (1.0 / HEAD_DIM) + RMS_EPS) * g

    def store_heads_transposed(ref, v, with_shift_rows=False):
        for g in range(v.shape[1] // LANES):
            tg = v[:, g * LANES:(g + 1) * LANES].T
            for e in range(2):
                th = tg[e * HEAD_DIM:(e + 1) * HEAD_DIM].astype(BF16)
                ref[0, 2 * g + e, 0:HEAD_DIM] = th
                if with_shift_rows:
                    tf = th.astype(F32)
                    nrm = jnp.sqrt(jnp.sum(tf * tf, axis=0, keepdims=True)) * NORM_UP
                    pad = (K_EXT - HEAD_DIM, th.shape[1])
                    first = lax.broadcasted_iota(jnp.int32, pad, 0) == 0
                    ref[0, 2 * g + e, HEAD_DIM:K_EXT] = jnp.where(first, -nrm, 0.0).astype(BF16)

    qscale = (HEAD_DIM ** -0.5) * math.log2(math.e)
    store_heads_transposed(q_ref, head_rms(proj[:, 512:1024], qg_ref[...]) * qscale, with_shift_rows=True)

    kv = proj[:, 1024:1152]
    kiw = proj[:, 1408:1536]
    w_scale = (IDX_HEADS ** -0.5) * (IDX_DIM ** -0.5)
    wabs = jnp.abs(kiw) * w_scale
    sshape = (LANES, IDX_HEADS * IDX_DIM)
    row = lax.broadcasted_iota(jnp.int32, sshape, 0)
    col = lax.broadcasted_iota(jnp.int32, sshape, 1)
    spread = jnp.where(row == IDX_DIM + col // IDX_DIM, 1.0, 0.0).astype(BF16)
    whi = wabs.astype(BF16)
    wlo = (wabs - whi.astype(F32)).astype(BF16)
    wrep = _dot(whi, spread) + _dot(wlo, spread)
    store_heads_transposed(qs_ref, proj[:, 1152:1408] * wrep)

    lane = lax.broadcasted_iota(jnp.int32, kv.shape, 1)
    k_ref[0] = jnp.where(lane < HEAD_DIM, head_rms(kv, kg_ref[...]), 0.0)[:, 0:K_EXT].astype(BF16)
    ki_ref[0] = head_rms(kiw, ikg_ref[...])[:, 0:IDX_DIM].astype(BF16)
    t_kv = kv.T
    t_kiw = kiw.T
    sg_ref[0] = jnp.where(t_kiw[IDX_DIM:IDX_DIM + SUBLANES] >= 0, 1.0, -1.0)
    for c in range(TL_IN // TK):
        v_ref[0, c] = t_kv[HEAD_DIM:2 * HEAD_DIM, c * TK:(c + 1) * TK].astype(BF16)


def _s5_kernel(u_ref, bplo_ref, bphi_ref, cplo_ref, cphi_ref, ar_ref, ai_ref, d_ref, wglu_ref, bglu_ref,
               wps_ref, o_ref, bu_scr, st_scr):
    @pl.when(pl.program_id(0) == 0)
    def _():
        st_scr[...] = jnp.zeros_like(st_scr)

    u = jnp.transpose(u_ref[...], (1, 0, 2)).reshape(SUBLANES * TC_SCAN, SSM_WIDTH)
    ub = u.astype(BF16)
    half_w = 2 * (SSM_GROUPS // 2) * SSM_STATE
    n_re = half_w // 2

    def scan_half(h):
        c0 = h * half_w
        a_r = ar_ref[:, h * n_re:(h + 1) * n_re]
        a_i = ai_ref[:, h * n_re:(h + 1) * n_re]
        sr = st_scr[:, c0:c0 + n_re]
        si = st_scr[:, c0 + n_re:c0 + half_w]
        for t in range(TC_SCAN):
            rows = slice(t * SUBLANES, (t + 1) * SUBLANES)
            br = bu_scr[rows, c0:c0 + n_re]
            bi = bu_scr[rows, c0 + n_re:c0 + half_w]
            sr, si = a_r * sr - a_i * si + br, a_r * si + a_i * sr + bi
            bu_scr[rows, c0:c0 + n_re] = sr
            bu_scr[rows, c0 + n_re:c0 + half_w] = si
        st_scr[:, c0:c0 + n_re] = sr
        st_scr[:, c0 + n_re:c0 + half_w] = si

    bu_scr[:, 0:half_w] = _dot(ub[:, 0:256], bplo_ref[...])
    bu_scr[:, half_w:2 * half_w] = _dot(ub[:, 256:512], bphi_ref[...])
    scan_half(0)
    y_lo = _dot(bu_scr[:, 0:half_w].astype(BF16), cplo_ref[...])
    scan_half(1)
    y_hi = _dot(bu_scr[:, half_w:2 * half_w].astype(BF16), cphi_ref[...])
    y = jnp.concatenate([y_lo, y_hi], axis=1)
    y = jax.nn.gelu(y + d_ref[...] * u)
    z = _dot(y.astype(BF16), wglu_ref[...]) + bglu_ref[...]
    y = y * jax.nn.sigmoid(z)
    ps = _dot(y.astype(BF16), wps_ref[...])
    o_ref[...] = jnp.transpose(ps.reshape(TC_SCAN, SUBLANES, D_MODEL), (1, 0, 2)).astype(BF16)


def _key_to_f32(key):
    bits = jnp.where(key < 0, key & jnp.int32(0x7FFFFFFF), ~key)
    return pltpu.bitcast(bits, F32)


def _attn_kernel(qs_ref, sg_ref, q_ref, ki_ref, k_ref, vt_ref, o_ref,
                 s_scr, m_scr, l_scr, acc_scr, sp_scr, qmod_scr, kmax_scr, sb_scr):
    i = pl.program_id(1)
    n_tiles = (i * QB + QB + TK - 1) // TK
    nv = TK // SUBLANES
    shape3 = (nv, SUBLANES, QB)
    k_in_tile = (lax.broadcasted_iota(jnp.int32, shape3, 0) * SUBLANES
                 + lax.broadcasted_iota(jnp.int32, shape3, 1))
    q_pos = i * QB + lax.broadcasted_iota(jnp.int32, shape3, 2)
    ksel = float(INDEX_TOPK)

    def all_sublanes(a, op):
        for sh in (4, 2, 1):
            a = op(a, pltpu.roll(a, sh, 0))
        return a

    sg = sg_ref[0]

    def idx_body(j, carry):
        kt = ki_ref[0, pl.ds(pl.multiple_of(j * TK, TK), TK), :]
        acc = jnp.zeros((TK, QB), F32)
        for pr in range(IDX_HEADS // 2):
            x = _dot(kt, jnp.concatenate([qs_ref[0, 2 * pr], qs_ref[0, 2 * pr + 1]], axis=1))
            for e in range(2):
                hd = 2 * pr + e
                acc = acc + sg[hd:hd + 1, :] * jnp.maximum(x[:, e * QB:(e + 1) * QB], 0.0)
        vis = j * TK + k_in_tile <= q_pos
        sc = jnp.where(vis, acc.reshape(shape3), -jnp.inf).reshape(TK, QB)
        s_scr[j] = sc
        sb_scr[j] = sc.astype(BF16)
        return carry

    def for_each_tile(body):
        def pair(jj, carry):
            body(2 * jj, carry)
            return body(2 * jj + 1, carry)
        lax.fori_loop(0, n_tiles // 2, pair, 0)

        @pl.when(n_tiles % 2 == 1)
        def _():
            body(n_tiles - 1, 0)

    for_each_tile(idx_body)

    def count(pred):
        def tile(j, acc):
            hit = jnp.where(pred(s_scr[j].reshape(shape3), j), 1.0, 0.0)
            return acc + jnp.sum(hit.reshape(nv // 4, 4, SUBLANES, QB), axis=0)
        acc = lax.fori_loop(0, n_tiles // 2, lambda jj, a: tile(2 * jj + 1, tile(2 * jj, a)),
                            jnp.zeros((4, SUBLANES, QB), F32))
        acc = lax.cond(n_tiles % 2 == 1, lambda a: tile(n_tiles - 1, a), lambda a: a, acc)
        return all_sublanes(jnp.sum(acc, axis=0), jnp.add)

    every = float(2 * INDEX_TOPK * 1024)
    nv16 = TK // (2 * SUBLANES)
    one_b, zero_b = jnp.ones((), BF16), jnp.zeros((), BF16)

    def count_hi(gc):
        def tile(j, acc):
            hit = jnp.where(sb_scr[j].reshape(nv16, 2 * SUBLANES, QB) >= gc, one_b, zero_b)
            for k in range(nv16):
                acc[k % 4] = acc[k % 4] + hit[k]
            return acc
        acc = lax.fori_loop(0, n_tiles // 2, lambda jj, a: tuple(tile(2 * jj + 1, tile(2 * jj, list(a)))),
                            tuple(jnp.zeros((2 * SUBLANES, QB), BF16) for _ in range(4)))
        acc = lax.cond(n_tiles % 2 == 1, lambda a: tuple(tile(n_tiles - 1, list(a))), lambda a: a, acc)
        tot = (acc[0].astype(F32) + acc[1].astype(F32)) + (acc[2].astype(F32) + acc[3].astype(F32))
        return all_sublanes(tot[0:SUBLANES] + tot[SUBLANES:2 * SUBLANES], jnp.add)

    def grid_key(k16):
        return lax.shift_left(k16, 16) | jnp.where(k16 < 0x8000, 0xFFFF, 0)

    def hi_body(it, k16):
        cand = k16 | lax.shift_left(jnp.int32(1), 15 - it)
        g = _key_to_f32(grid_key(cand))
        n = count_hi(jnp.concatenate([g, g], axis=0).astype(BF16))
        cnt = jnp.where(lax.shift_right_logical(cand, 7) == 0, every, n)
        return jnp.where(cnt >= ksel, cand, k16)

    k16 = lax.fori_loop(0, 16, hi_body, jnp.zeros((SUBLANES, QB), jnp.int32))

    def lo_body(_, carry):
        lo, hi, cnt_lo = carry
        mid = lo + lax.shift_right_logical(hi - lo, 1)
        tc = _key_to_f32(mid)
        cnt = jnp.where(lax.shift_right_logical(mid, 23) == 0, every, count(lambda s, j: s >= tc))
        ok = cnt >= ksel
        return jnp.where(ok, mid, lo), jnp.where(ok, hi, mid), jnp.where(ok, cnt, cnt_lo)

    key, _, cnt_ge = lax.fori_loop(
        0, 17, lo_body,
        (grid_key(k16) - 0x8001, grid_key(jnp.minimum(k16 + 1, 0xFFFF)), jnp.full((SUBLANES, QB), every, F32)))
    thr = _key_to_f32(key)
    has_excess = jnp.max(cnt_ge) > ksel

    def write_bias(select):
        def body(j, carry):
            s = s_scr[j].reshape(shape3)
            vis = j * TK + k_in_tile <= q_pos
            bias = jnp.where(vis, jnp.where(select(s, j), 0.0, -jnp.inf), -jnp.inf)
            s_scr[j] = bias.reshape(TK, QB)
            return carry
        for_each_tile(body)

    @pl.when(jnp.logical_not(has_excess))
    def _():
        write_bias(lambda s, j: s >= thr)

    @pl.when(has_excess)
    def _():
        cnt_gt = count(lambda s, j: s > thr)
        need = ksel - cnt_gt

        def jbit_body(it, jkey):
            cand = jkey | lax.shift_left(jnp.int32(1), 10 - it)
            cnt = count(lambda s, j: (s == thr) & (j * TK + k_in_tile < cand))
            return jnp.where(cnt < need, cand, jkey)

        jkey = lax.fori_loop(0, 11, jbit_body, jnp.zeros((SUBLANES, QB), jnp.int32))
        write_bias(lambda s, j: (s > thr) | ((s == thr) & (j * TK + k_in_tile <= jkey)))

    lane_k = lax.broadcasted_iota(jnp.int32, (TK, K_EXT), 1)

    def attend(get_q, k_fill):
        l_scr[...] = jnp.zeros(l_scr.shape, F32)
        acc_scr[...] = jnp.zeros(acc_scr.shape, F32)

        def body(j, carry):
            kt = k_ref[0, pl.ds(pl.multiple_of(j * TK, TK), TK), :]
            kt = jnp.where(lane_k == Q_SHIFT_ROW, jnp.full((TK, K_EXT), k_fill, F32).astype(BF16), kt)
            vt = vt_ref[0, j]
            bias = s_scr[j].reshape(shape3)
            for pr in range(N_HEADS // 2):
                sp_scr[pr] = _dot(kt, jnp.concatenate([get_q(2 * pr), get_q(2 * pr + 1)], axis=1))
            for pr in range(N_HEADS // 2):
                for e in range(2):
                    hd = 2 * pr + e
                    p = jnp.exp2(sp_scr[pr, :, e * QB:(e + 1) * QB].reshape(shape3) + bias)
                    l_scr[hd] = l_scr[hd] + jnp.sum(p, axis=0)
                    acc_scr[hd] = acc_scr[hd] + _dot(vt, p.reshape(TK, QB).astype(BF16))
            return carry

        for_each_tile(body)

    @pl.when(i == 0)
    def _():
        kf = k_ref[0].astype(F32)
        kmax_scr[0] = jnp.max(jnp.sqrt(jnp.sum(kf * kf, axis=1, keepdims=True))) * NORM_UP

    attend(lambda hd: q_ref[0, hd], kmax_scr[0])

    l_min = all_sublanes(l_scr[0], jnp.add)
    for hd in range(1, N_HEADS):
        l_min = jnp.minimum(l_min, all_sublanes(l_scr[hd], jnp.add))
    underflow = jnp.logical_not(jnp.min(l_min) >= L_FLOOR)

    @pl.when(underflow)
    def _():
        row_q = lax.broadcasted_iota(jnp.int32, (K_EXT, QB), 0)
        m_scr[...] = jnp.full(m_scr.shape, NEG_BIG, F32)

        def max_body(j, carry):
            kt = k_ref[0, pl.ds(pl.multiple_of(j * TK, TK), TK), :]
            kt = jnp.where(lane_k == Q_SHIFT_ROW, jnp.zeros((TK, K_EXT), BF16), kt)
            bias = s_scr[j].reshape(shape3)
            for hd in range(N_HEADS):
                s = _dot(kt, q_ref[0, hd]).reshape(shape3) + bias
                m_scr[hd] = jnp.maximum(m_scr[hd], all_sublanes(jnp.max(s, axis=0), jnp.maximum))
            return carry

        lax.fori_loop(0, n_tiles, max_body, 0)
        for hd in range(N_HEADS):
            shift = jnp.broadcast_to(-m_scr[hd][0:1, :], (K_EXT, QB)).astype(BF16)
            qmod_scr[hd] = jnp.where(row_q == Q_SHIFT_ROW, shift, q_ref[0, hd])
        attend(lambda hd: qmod_scr[hd], 1.0)

    outs = []
    for hd in range(N_HEADS):
        l = all_sublanes(l_scr[hd], jnp.add)
        o = acc_scr[hd].reshape(HEAD_DIM // SUBLANES, SUBLANES, QB) / l
        outs.append(o.reshape(HEAD_DIM, QB))
    o_ref[0] = jnp.concatenate(outs, axis=0).T.astype(BF16)


def _ffn_kernel(x_ref, ps_ref, ya_ref, g1_ref, wgate_ref, wpa_ref, wo_ref, g2_ref, wfg_ref, wfu_ref, wfd_ref,
                o_ref):
    x = x_ref[0]
    h = _rms(x, g1_ref[...]).astype(BF16)
    gates = _dot(h, wgate_ref[...])
    pa = _dot(ya_ref[0], wpa_ref[...])
    merged = (jax.nn.sigmoid(gates[:, 0:D_MODEL]) * ps_ref[0].astype(F32)
              + jax.nn.sigmoid(gates[:, D_MODEL:2 * D_MODEL]) * pa)
    x1 = x + _dot(merged.astype(BF16), wo_ref[...])
    h2 = _rms(x1, g2_ref[...]).astype(BF16)
    hid = jax.nn.silu(_dot(h2, wfg_ref[...])) * _dot(h2, wfu_ref[...])
    o_ref[0] = x1 + _dot(hid.astype(BF16), wfd_ref[...])


def _gate_repack_kernel(a_ref, b_ref, c_ref, o_ref):
    off = GATE_COL0 % GATE_RB
    rows = jnp.concatenate([a_ref[off:], b_ref[...], c_ref[:off]], axis=0)
    o_ref[...] = rows.T.astype(BF16)


def _const_spec(shape):
    nd = len(shape)
    return pl.BlockSpec(shape, lambda *_: (0,) * nd, pipeline_mode=pl.Buffered(1))


def _block_diag(blocks):
    g, r, c = blocks.shape
    eye = jnp.eye(g, dtype=blocks.dtype)
    return jnp.einsum('grc,gh->grhc', blocks, eye).reshape(g * r, g * c)


def kernel(x, norm1_g, w_in, A_re, A_im, log_dt, B_re, B_im, C_re, C_im, D_skip, w_glu, b_glu, q_norm_g, k_norm_g,
           idx_k_norm_g, w_proj_ssm, w_proj_attn, w_out, norm2_g, w_ffn_gate, w_ffn_up, w_ffn_down):
    bsz, seq, _ = x.shape
    assert x.shape == (8, 2048, D_MODEL) and w_in.shape[0] == 1
    G, N, P = SSM_GROUPS, SSM_STATE, SSM_GROUP
    cparams = functools.partial(pltpu.CompilerParams, vmem_limit_bytes=VMEM_LIMIT)

    w_in_t = w_in[0].T
    gblk = GATE_COL0 // GATE_RB
    spec = lambda k: pl.BlockSpec((GATE_RB, D_MODEL), lambda j: (gblk + 2 * j + k, 0))
    w_gates = pl.pallas_call(
        _gate_repack_kernel,
        grid=(2 * D_MODEL // LANES,),
        in_specs=[spec(0), spec(1), spec(2)],
        out_specs=pl.BlockSpec((D_MODEL, LANES), lambda j: (0, j)),
        out_shape=jax.ShapeDtypeStruct((D_MODEL, 2 * D_MODEL), BF16),
        name="gate_weight_repack",
    )(w_in_t, w_in_t, w_in_t)
    g1 = norm1_g[0].reshape(1, D_MODEL)
    ones64 = jnp.ones((HEAD_DIM,), F32)

    rep = lambda a: jnp.repeat(a, P, axis=0)
    gpn = jax.ShapeDtypeStruct((G * P, N), F32)
    abar_r, abar_i, bp_r, bp_i = pl.pallas_call(
        _disc_kernel, out_shape=(gpn, gpn, gpn, gpn), name="s5_discretise",
    )(rep(A_re[0]), rep(A_im[0]), rep(jnp.broadcast_to(log_dt[0].reshape(G, 1), (G, N))),
      B_re[0].transpose(0, 2, 1).reshape(G * P, N), B_im[0].transpose(0, 2, 1).reshape(G * P, N))
    abar_r, abar_i = abar_r[::P], abar_i[::P]
    bp_r, bp_i = bp_r.reshape(G, P, N), bp_i.reshape(G, P, N)
    hg = G // 2
    bp = [jnp.concatenate([_block_diag(bp_r[sl]), _block_diag(bp_i[sl])], axis=1).astype(BF16)
          for sl in (slice(0, hg), slice(hg, G))]
    c_r = C_re[0].transpose(0, 2, 1)
    c_i = C_im[0].transpose(0, 2, 1)
    cp = [jnp.concatenate([_block_diag(c_r[sl]), -_block_diag(c_i[sl])], axis=0).astype(BF16)
          for sl in (slice(0, hg), slice(hg, G))]
    ar8 = jnp.broadcast_to(abar_r.reshape(1, G * N), (SUBLANES, G * N))
    ai8 = jnp.broadcast_to(abar_i.reshape(1, G * N), (SUBLANES, G * N))

    n_in = seq // TL_IN
    u_t, q_h, qs_h, sg_t, k_n, ki_n, v_t = pl.pallas_call(
        _inproj_kernel,
        grid=(bsz, n_in),
        in_specs=[
            pl.BlockSpec((1, TL_IN, D_MODEL), lambda b, i: (b, i, 0)),
            _const_spec((1, D_MODEL)),
            _const_spec((W1_COLS, D_MODEL)),
            _const_spec((1, ATTN_WIDTH)), _const_spec((1, LANES)), _const_spec((1, LANES)),
            _const_spec((2 * LANES, 2 * LANES)),
        ],
        out_specs=[
            pl.BlockSpec((1, TL_IN, SSM_WIDTH), lambda b, i: (b, i, 0)),
            pl.BlockSpec((1, N_HEADS, K_EXT, TL_IN), lambda b, i: (b, 0, 0, i)),
            pl.BlockSpec((1, IDX_HEADS, IDX_DIM, TL_IN), lambda b, i: (b, 0, 0, i)),
            pl.BlockSpec((1, SUBLANES, TL_IN), lambda b, i: (b, 0, i)),
            pl.BlockSpec((1, TL_IN, K_EXT), lambda b, i: (b, i, 0)),
            pl.BlockSpec((1, TL_IN, IDX_DIM), lambda b, i: (b, i, 0)),
            pl.BlockSpec((1, TL_IN // TK, HEAD_DIM, TK), lambda b, i: (b, i, 0, 0)),
        ],
        out_shape=(
            jax.ShapeDtypeStruct((bsz, seq, SSM_WIDTH), F32),
            jax.ShapeDtypeStruct((bsz, N_HEADS, K_EXT, seq), BF16),
            jax.ShapeDtypeStruct((bsz, IDX_HEADS, IDX_DIM, seq), BF16),
            jax.ShapeDtypeStruct((bsz, SUBLANES, seq), F32),
            jax.ShapeDtypeStruct((bsz, seq, K_EXT), BF16),
            jax.ShapeDtypeStruct((bsz, seq, IDX_DIM), BF16),
            jax.ShapeDtypeStruct((bsz, seq // TK, HEAD_DIM, TK), BF16),
        ),
        scratch_shapes=[pltpu.VMEM((D_MODEL, W1_COLS), BF16)],
        compiler_params=cparams(dimension_semantics=("arbitrary", "arbitrary")),
        name="in_projection",
    )(x, g1, w_in_t, jnp.tile(q_norm_g[0], N_HEADS).reshape(1, ATTN_WIDTH),
      jnp.concatenate([k_norm_g[0], ones64]).reshape(1, LANES),
      jnp.concatenate([idx_k_norm_g[0], ones64]).reshape(1, LANES),
      jnp.kron(jnp.eye(2 * LANES // HEAD_DIM, dtype=BF16), jnp.ones((HEAD_DIM, HEAD_DIM), BF16)))

    rows = SUBLANES * TC_SCAN
    ps_t = pl.pallas_call(
        _s5_kernel,
        grid=(seq // TC_SCAN,),
        in_specs=[
            pl.BlockSpec((bsz, TC_SCAN, SSM_WIDTH), lambda c: (0, c, 0)),
            _const_spec((256, 2048)), _const_spec((256, 2048)),
            _const_spec((2048, 256)), _const_spec((2048, 256)),
            _const_spec((SUBLANES, G * N)), _const_spec((SUBLANES, G * N)),
            _const_spec((1, SSM_WIDTH)), _const_spec((SSM_WIDTH, SSM_WIDTH)), _const_spec((1, SSM_WIDTH)),
            _const_spec((SSM_WIDTH, D_MODEL)),
        ],
        out_specs=pl.BlockSpec((bsz, TC_SCAN, D_MODEL), lambda c: (0, c, 0)),
        out_shape=jax.ShapeDtypeStruct((bsz, seq, D_MODEL), BF16),
        scratch_shapes=[pltpu.VMEM((rows, 2 * G * N), F32), pltpu.VMEM((SUBLANES, 2 * G * N), F32)],
        compiler_params=cparams(dimension_semantics=("arbitrary",)),
        name="s5_branch",
    )(u_t, bp[0], bp[1], cp[0], cp[1], ar8, ai8,
      D_skip[0].reshape(1, SSM_WIDTH), w_glu[0].astype(BF16), b_glu[0].reshape(1, SSM_WIDTH),
      w_proj_ssm[0].astype(BF16))

    y_att = pl.pallas_call(
        _attn_kernel,
        grid=(bsz, seq // QB),
        in_specs=[
            pl.BlockSpec((1, IDX_HEADS, IDX_DIM, QB), lambda b, i: (b, 0, 0, i)),
            pl.BlockSpec((1, SUBLANES, QB), lambda b, i: (b, 0, i)),
            pl.BlockSpec((1, N_HEADS, K_EXT, QB), lambda b, i: (b, 0, 0, i)),
            pl.BlockSpec((1, seq, IDX_DIM), lambda b, i: (b, 0, 0)),
            pl.BlockSpec((1, seq, K_EXT), lambda b, i: (b, 0, 0)),
            pl.BlockSpec((1, seq // TK, HEAD_DIM, TK), lambda b, i: (b, 0, 0, 0)),
        ],
        out_specs=pl.BlockSpec((1, QB, ATTN_WIDTH), lambda b, i: (b, i, 0)),
        out_shape=jax.ShapeDtypeStruct((bsz, seq, ATTN_WIDTH), BF16),
        scratch_shapes=[pltpu.VMEM((seq // TK, TK, QB), F32),
                        pltpu.VMEM((N_HEADS, SUBLANES, QB), F32),
                        pltpu.VMEM((N_HEADS, SUBLANES, QB), F32),
                        pltpu.VMEM((N_HEADS, HEAD_DIM, QB), F32),
                        pltpu.VMEM((N_HEADS // 2, TK, 2 * QB), F32),
                        pltpu.VMEM((N_HEADS, K_EXT, QB), BF16),
                        pltpu.SMEM((1,), F32),
                        pltpu.VMEM((seq // TK, TK, QB), BF16)],
        compiler_params=cparams(dimension_semantics=("parallel", "arbitrary")),
        name="sparse_attention",
    )(qs_h, sg_t, q_h, ki_n, k_n, v_t)

    out = pl.pallas_call(
        _ffn_kernel,
        grid=(bsz, seq // TL_FFN),
        in_specs=[
            pl.BlockSpec((1, TL_FFN, D_MODEL), lambda b, i: (b, i, 0)),
            pl.BlockSpec((1, TL_FFN, D_MODEL), lambda b, i: (b, i, 0)),
            pl.BlockSpec((1, TL_FFN, ATTN_WIDTH), lambda b, i: (b, i, 0)),
            _const_spec((1, D_MODEL)),
            _const_spec((D_MODEL, 2 * D_MODEL)),
            _const_spec((ATTN_WIDTH, D_MODEL)),
            _const_spec((D_MODEL, D_MODEL)),
            _const_spec((1, D_MODEL)),
            _const_spec((D_MODEL, D_FF)), _const_spec((D_MODEL, D_FF)), _const_spec((D_FF, D_MODEL)),
        ],
        out_specs=pl.BlockSpec((1, TL_FFN, D_MODEL), lambda b, i: (b, i, 0)),
        out_shape=jax.ShapeDtypeStruct((bsz, seq, D_MODEL), F32),
        compiler_params=cparams(dimension_semantics=("parallel", "parallel")),
        name="merge_ffn",
    )(x, ps_t, y_att, g1, w_gates, w_proj_attn[0].astype(BF16),
      w_out[0].astype(BF16), norm2_g[0].reshape(1, D_MODEL), w_ffn_gate[0].astype(BF16),
      w_ffn_up[0].astype(BF16), w_ffn_down[0].astype(BF16))
    return out
```

```python
import functools
import math

import jax
import jax.numpy as jnp
from jax import lax
from jax.experimental import pallas as pl
from jax.experimental.pallas import tpu as pltpu

F32 = jnp.float32
BF16 = jnp.bfloat16

D_MODEL = 1024
SSM_WIDTH = 512
SSM_GROUP = 16
SSM_GROUPS = 32
SSM_STATE = 64
N_HEADS = 8
HEAD_DIM = 64
ATTN_WIDTH = 512
IDX_HEADS = 4
IDX_DIM = 64
INDEX_TOPK = 256
D_FF = 2816
RMS_EPS = 1e-6

LANES = 128
SUBLANES = 8
VMEM_LIMIT = 56 * 1024 * 1024

TL_IN = 512
TC_SCAN = 128
QB = 256
TK = 256
TL_FFN = 256

W1_COLS = 1536
GATE_COL0 = 1476
GATE_RB = 64
NEG_BIG = -1e30
K_EXT = 80
Q_SHIFT_ROW = 64
NORM_UP = 1.0 + 2.0 ** -7
L_FLOOR = 2.0 ** -100
assert QB % LANES == 0


def _dot(a, b):
    return jnp.dot(a, b, preferred_element_type=F32)


def _rms(x, g):
    return x * lax.rsqrt(jnp.mean(x * x, axis=-1, keepdims=True) + RMS_EPS) * g


def _disc_kernel(are_ref, aim_ref, ldt_ref, bre_ref, bim_ref, abr_ref, abi_ref, bpr_ref, bpi_ref):
    ar = are_ref[...]
    ai = aim_ref[...]
    dt = jnp.exp(ldt_ref[...])
    mag = jnp.exp(ar * dt)
    abar_r = mag * jnp.cos(ai * dt)
    abar_i = mag * jnp.sin(ai * dt)
    den = ar * ar + ai * ai
    nr = abar_r - 1.0
    coef_r = (nr * ar + abar_i * ai) / den
    coef_i = (abar_i * ar - nr * ai) / den
    abr_ref[...] = abar_r
    abi_ref[...] = abar_i
    br = bre_ref[...]
    bi = bim_ref[...]
    bpr_ref[...] = coef_r * br - coef_i * bi
    bpi_ref[...] = coef_r * bi + coef_i * br


def _inproj_kernel(x_ref, g1_ref, w_ref, qg_ref, kg_ref, ikg_ref, seg_ref,
                   u_ref, q_ref, qs_ref, sg_ref, k_ref, ki_ref, v_ref, w_scr):
    @pl.when(jnp.logical_and(pl.program_id(0) == 0, pl.program_id(1) == 0))
    def _():
        w_scr[...] = w_ref[...].T.astype(BF16)

    x = x_ref[0]
    h = _rms(x, g1_ref[...]).astype(BF16)
    proj = _dot(h, w_scr[...])
    u_ref[0] = proj[:, 0:512]
    seg = seg_ref[...]

    def head_rms(v, g):
        sq = (v * v).astype(BF16)
        n = v.shape[1]
        w = min(n, seg.shape[0])
        ss = jnp.concatenate([_dot(sq[:, c:c + w], seg[0:w, 0:w]) for c in range(0, n, w)], axis=1)
        return v * lax.rsqrt(ss * (1.0 / HEAD_DIM) + RMS_EPS) * g

    def store_heads_transposed(ref, v, with_shift_rows=False):
        for g in range(v.shape[1] // LANES):
            tg = v[:, g * LANES:(g + 1) * LANES].T
            for e in range(2):
                th = tg[e * HEAD_DIM:(e + 1) * HEAD_DIM].astype(BF16)
                ref[0, 2 * g + e, 0:HEAD_DIM] = th
                if with_shift_rows:
                    tf = th.astype(F32)
                    nrm = jnp.sqrt(jnp.sum(tf * tf, axis=0, keepdims=True)) * NORM_UP
                    pad = (K_EXT - HEAD_DIM, th.shape[1])
                    first = lax.broadcasted_iota(jnp.int32, pad, 0) == 0
                    ref[0, 2 * g + e, HEAD_DIM:K_EXT] = jnp.where(first, -nrm, 0.0).astype(BF16)

    qscale = (HEAD_DIM ** -0.5) * math.log2(math.e)
    store_heads_transposed(q_ref, head_rms(proj[:, 512:1024], qg_ref[...]) * qscale, with_shift_rows=True)

    kv = proj[:, 1024:1152]
    kiw = proj[:, 1408:1536]
    w_scale = (IDX_HEADS ** -0.5) * (IDX_DIM ** -0.5)
    wabs = jnp.abs(kiw) * w_scale
    sshape = (LANES, IDX_HEADS * IDX_DIM)
    row = lax.broadcasted_iota(jnp.int32, sshape, 0)
    col = lax.broadcasted_iota(jnp.int32, sshape, 1)
    spread = jnp.where(row == IDX_DIM + col // IDX_DIM, 1.0, 0.0).astype(BF16)
    whi = wabs.astype(BF16)
    wlo = (wabs - whi.astype(F32)).astype(BF16)
    wrep = _dot(whi, spread) + _dot(wlo, spread)
    store_heads_transposed(qs_ref, proj[:, 1152:1408] * wrep)

    lane = lax.broadcasted_iota(jnp.int32, kv.shape, 1)
    k_ref[0] = jnp.where(lane < HEAD_DIM, head_rms(kv, kg_ref[...]), 0.0)[:, 0:K_EXT].astype(BF16)
    ki_ref[0] = head_rms(kiw, ikg_ref[...])[:, 0:IDX_DIM].astype(BF16)
    t_kv = kv.T
    t_kiw = kiw.T
    sg_ref[0] = jnp.where(t_kiw[IDX_DIM:IDX_DIM + SUBLANES] >= 0, 1.0, -1.0)
    for c in range(TL_IN // TK):
        v_ref[0, c] = t_kv[HEAD_DIM:2 * HEAD_DIM, c * TK:(c + 1) * TK].astype(BF16)


def _s5_kernel(u_ref, bplo_ref, bphi_ref, cplo_ref, cphi_ref, ar_ref, ai_ref, d_ref, wglu_ref, bglu_ref,
               wps_ref, o_ref, bu_scr, st_scr):
    @pl.when(pl.program_id(0) == 0)
    def _():
        st_scr[...] = jnp.zeros_like(st_scr)

    u = jnp.transpose(u_ref[...], (1, 0, 2)).reshape(SUBLANES * TC_SCAN, SSM_WIDTH)
    ub = u.astype(BF16)
    half_w = 2 * (SSM_GROUPS // 2) * SSM_STATE
    n_re = half_w // 2

    def scan_half(h):
        c0 = h * half_w
        a_r = ar_ref[:, h * n_re:(h + 1) * n_re]
        a_i = ai_ref[:, h * n_re:(h + 1) * n_re]
        sr = st_scr[:, c0:c0 + n_re]
        si = st_scr[:, c0 + n_re:c0 + half_w]
        for t in range(TC_SCAN):
            rows = slice(t * SUBLANES, (t + 1) * SUBLANES)
            br = bu_scr[rows, c0:c0 + n_re]
            bi = bu_scr[rows, c0 + n_re:c0 + half_w]
            sr, si = a_r * sr - a_i * si + br, a_r * si + a_i * sr + bi
            bu_scr[rows, c0:c0 + n_re] = sr
            bu_scr[rows, c0 + n_re:c0 + half_w] = si
        st_scr[:, c0:c0 + n_re] = sr
        st_scr[:, c0 + n_re:c0 + half_w] = si

    bu_scr[:, 0:half_w] = _dot(ub[:, 0:256], bplo_ref[...])
    bu_scr[:, half_w:2 * half_w] = _dot(ub[:, 256:512], bphi_ref[...])
    scan_half(0)
    y_lo = _dot(bu_scr[:, 0:half_w].astype(BF16), cplo_ref[...])
    scan_half(1)
    y_hi = _dot(bu_scr[:, half_w:2 * half_w].astype(BF16), cphi_ref[...])
    y = jnp.concatenate([y_lo, y_hi], axis=1)
    y = jax.nn.gelu(y + d_ref[...] * u)
    z = _dot(y.astype(BF16), wglu_ref[...]) + bglu_ref[...]
    y = y * jax.nn.sigmoid(z)
    ps = _dot(y.astype(BF16), wps_ref[...])
    o_ref[...] = jnp.transpose(ps.reshape(TC_SCAN, SUBLANES, D_MODEL), (1, 0, 2)).astype(BF16)


def _key_to_f32(key):
    bits = jnp.where(key < 0, key & jnp.int32(0x7FFFFFFF), ~key)
    return pltpu.bitcast(bits, F32)


def _attn_kernel(qs_ref, sg_ref, q_ref, ki_ref, k_ref, vt_ref, o_ref,
                 s_scr, m_scr, l_scr, acc_scr, sp_scr, qmod_scr, kmax_scr, sb_scr):
    i = pl.program_id(1)
    n_tiles = (i * QB + QB + TK - 1) // TK
    nv = TK // SUBLANES
    shape3 = (nv, SUBLANES, QB)
    k_in_tile = (lax.broadcasted_iota(jnp.int32, shape3, 0) * SUBLANES
                 + lax.broadcasted_iota(jnp.int32, shape3, 1))
    q_pos = i * QB + lax.broadcasted_iota(jnp.int32, shape3, 2)
    ksel = float(INDEX_TOPK)

    def all_sublanes(a, op):
        for sh in (4, 2, 1):
            a = op(a, pltpu.roll(a, sh, 0))
        return a

    sg = sg_ref[0]

    def idx_body(j, carry):
        kt = ki_ref[0, pl.ds(pl.multiple_of(j * TK, TK), TK), :]
        acc = jnp.zeros((TK, QB), F32)
        for pr in range(IDX_HEADS // 2):
            x = _dot(kt, jnp.concatenate([qs_ref[0, 2 * pr], qs_ref[0, 2 * pr + 1]], axis=1))
            for e in range(2):
                hd = 2 * pr + e
                acc = acc + sg[hd:hd + 1, :] * jnp.maximum(x[:, e * QB:(e + 1) * QB], 0.0)
        vis = j * TK + k_in_tile <= q_pos
        sc = jnp.where(vis, acc.reshape(shape3), -jnp.inf).reshape(TK, QB)
        s_scr[j] = sc
        sb_scr[j] = sc.astype(BF16)
        return carry

    def for_each_tile(body):
        def pair(jj, carry):
            body(2 * jj, carry)
            return body(2 * jj + 1, carry)
        lax.fori_loop(0, n_tiles // 2, pair, 0)

        @pl.when(n_tiles % 2 == 1)
        def _():
            body(n_tiles - 1, 0)

    for_each_tile(idx_body)

    def count(pred):
        def tile(j, acc):
            hit = jnp.where(pred(s_scr[j].reshape(shape3), j), 1.0, 0.0)
            return acc + jnp.sum(hit.reshape(nv // 4, 4, SUBLANES, QB), axis=0)
        acc = lax.fori_loop(0, n_tiles // 2, lambda jj, a: tile(2 * jj + 1, tile(2 * jj, a)),
                            jnp.zeros((4, SUBLANES, QB), F32))
        acc = lax.cond(n_tiles % 2 == 1, lambda a: tile(n_tiles - 1, a), lambda a: a, acc)
        return all_sublanes(jnp.sum(acc, axis=0), jnp.add)

    every = float(2 * INDEX_TOPK * 1024)
    nv16 = TK // (2 * SUBLANES)
    one_b, zero_b = jnp.ones((), BF16), jnp.zeros((), BF16)

    def count_hi(gc):
        def tile(j, acc):
            hit = jnp.where(sb_scr[j].reshape(nv16, 2 * SUBLANES, QB) >= gc, one_b, zero_b)
            for k in range(nv16):
                acc[k % 4] = acc[k % 4] + hit[k]
            return acc
        acc = lax.fori_loop(0, n_tiles // 2, lambda jj, a: tuple(tile(2 * jj + 1, tile(2 * jj, list(a)))),
                            tuple(jnp.zeros((2 * SUBLANES, QB), BF16) for _ in range(4)))
        acc = lax.cond(n_tiles % 2 == 1, lambda a: tuple(tile(n_tiles - 1, list(a))), lambda a: a, acc)
        tot = (acc[0].astype(F32) + acc[1].astype(F32)) + (acc[2].astype(F32) + acc[3].astype(F32))
        return all_sublanes(tot[0:SUBLANES] + tot[SUBLANES:2 * SUBLANES], jnp.add)

    def grid_key(k16):
        return lax.shift_left(k16, 16) | jnp.where(k16 < 0x8000, 0xFFFF, 0)

    def hi_body(it, k16):
        cand = k16 | lax.shift_left(jnp.int32(1), 15 - it)
        g = _key_to_f32(grid_key(cand))
        n = count_hi(jnp.concatenate([g, g], axis=0).astype(BF16))
        cnt = jnp.where(lax.shift_right_logical(cand, 7) == 0, every, n)
        return jnp.where(cnt >= ksel, cand, k16)

    def lo_body(_, carry):
        lo, hi, cnt_lo = carry
        mid = lo + lax.shift_right_logical(hi - lo, 1)
        tc = _key_to_f32(mid)
        cnt = jnp.where(lax.shift_right_logical(mid, 23) == 0, every, count(lambda s, j: s >= tc))
        ok = cnt >= ksel
        return jnp.where(ok, mid, lo), jnp.where(ok, hi, mid), jnp.where(ok, cnt, cnt_lo)

    def write_bias(select):
        def body(j, carry):
            s = s_scr[j].reshape(shape3)
            vis = j * TK + k_in_tile <= q_pos
            bias = jnp.where(vis, jnp.where(select(s, j), 0.0, -jnp.inf), -jnp.inf)
            s_scr[j] = bias.reshape(TK, QB)
            return carry
        for_each_tile(body)

    few_keys = (i + 1) * QB <= INDEX_TOPK

    @pl.when(few_keys)
    def _():
        write_bias(lambda s, j: jnp.ones(s.shape, jnp.bool_))

    @pl.when(jnp.logical_not(few_keys))
    def _():
        k16 = lax.fori_loop(0, 16, hi_body, jnp.zeros((SUBLANES, QB), jnp.int32))
        key, _, cnt_ge = lax.fori_loop(
            0, 17, lo_body,
            (grid_key(k16) - 0x8001, grid_key(jnp.minimum(k16 + 1, 0xFFFF)),
             jnp.full((SUBLANES, QB), every, F32)))
        thr = _key_to_f32(key)
        has_excess = jnp.max(cnt_ge) > ksel

        @pl.when(jnp.logical_not(has_excess))
        def _():
            write_bias(lambda s, j: s >= thr)

        @pl.when(has_excess)
        def _():
            cnt_gt = count(lambda s, j: s > thr)
            need = ksel - cnt_gt

            def jbit_body(it, jkey):
                cand = jkey | lax.shift_left(jnp.int32(1), 10 - it)
                cnt = count(lambda s, j: (s == thr) & (j * TK + k_in_tile < cand))
                return jnp.where(cnt < need, cand, jkey)

            jkey = lax.fori_loop(0, 11, jbit_body, jnp.zeros((SUBLANES, QB), jnp.int32))
            write_bias(lambda s, j: (s > thr) | ((s == thr) & (j * TK + k_in_tile <= jkey)))

    lane_k = lax.broadcasted_iota(jnp.int32, (TK, K_EXT), 1)

    def attend(get_q, k_fill):
        l_scr[...] = jnp.zeros(l_scr.shape, F32)
        acc_scr[...] = jnp.zeros(acc_scr.shape, F32)

        def body(j, carry):
            kt = k_ref[0, pl.ds(pl.multiple_of(j * TK, TK), TK), :]
            kt = jnp.where(lane_k == Q_SHIFT_ROW, jnp.full((TK, K_EXT), k_fill, F32).astype(BF16), kt)
            vt = vt_ref[0, j]
            bias = s_scr[j].reshape(shape3)
            for pr in range(N_HEADS // 2):
                sp_scr[pr] = _dot(kt, jnp.concatenate([get_q(2 * pr), get_q(2 * pr + 1)], axis=1))
            for pr in range(N_HEADS // 2):
                for e in range(2):
                    hd = 2 * pr + e
                    p = jnp.exp2(sp_scr[pr, :, e * QB:(e + 1) * QB].reshape(shape3) + bias)
                    l_scr[hd] = l_scr[hd] + jnp.sum(p, axis=0)
                    acc_scr[hd] = acc_scr[hd] + _dot(vt, p.reshape(TK, QB).astype(BF16))
            return carry

        for_each_tile(body)

    @pl.when(i == 0)
    def _():
        kf = k_ref[0].astype(F32)
        kmax_scr[0] = jnp.max(jnp.sqrt(jnp.sum(kf * kf, axis=1, keepdims=True))) * NORM_UP

    attend(lambda hd: q_ref[0, hd], kmax_scr[0])

    l_min = all_sublanes(l_scr[0], jnp.add)
    for hd in range(1, N_HEADS):
        l_min = jnp.minimum(l_min, all_sublanes(l_scr[hd], jnp.add))
    underflow = jnp.logical_not(jnp.min(l_min) >= L_FLOOR)

    @pl.when(underflow)
    def _():
        row_q = lax.broadcasted_iota(jnp.int32, (K_EXT, QB), 0)
        m_scr[...] = jnp.full(m_scr.shape, NEG_BIG, F32)

        def max_body(j, carry):
            kt = k_ref[0, pl.ds(pl.multiple_of(j * TK, TK), TK), :]
            kt = jnp.where(lane_k == Q_SHIFT_ROW, jnp.zeros((TK, K_EXT), BF16), kt)
            bias = s_scr[j].reshape(shape3)
            for hd in range(N_HEADS):
                s = _dot(kt, q_ref[0, hd]).reshape(shape3) + bias
                m_scr[hd] = jnp.maximum(m_scr[hd], all_sublanes(jnp.max(s, axis=0), jnp.maximum))
            return carry

        lax.fori_loop(0, n_tiles, max_body, 0)
        for hd in range(N_HEADS):
            shift = jnp.broadcast_to(-m_scr[hd][0:1, :], (K_EXT, QB)).astype(BF16)
            qmod_scr[hd] = jnp.where(row_q == Q_SHIFT_ROW, shift, q_ref[0, hd])
        attend(lambda hd: qmod_scr[hd], 1.0)

    outs = []
    for hd in range(N_HEADS):
        l = all_sublanes(l_scr[hd], jnp.add)
        o = acc_scr[hd].reshape(HEAD_DIM // SUBLANES, SUBLANES, QB) / l
        outs.append(o.reshape(HEAD_DIM, QB))
    o_ref[0] = jnp.concatenate(outs, axis=0).T.astype(BF16)


def _ffn_kernel(x_ref, ps_ref, ya_ref, g1_ref, wgate_ref, wpa_ref, wo_ref, g2_ref, wfg_ref, wfu_ref, wfd_ref,
                o_ref):
    x = x_ref[0]
    h = _rms(x, g1_ref[...]).astype(BF16)
    gates = _dot(h, wgate_ref[...])
    pa = _dot(ya_ref[0], wpa_ref[...])
    merged = (jax.nn.sigmoid(gates[:, 0:D_MODEL]) * ps_ref[0].astype(F32)
              + jax.nn.sigmoid(gates[:, D_MODEL:2 * D_MODEL]) * pa)
    x1 = x + _dot(merged.astype(BF16), wo_ref[...])
    h2 = _rms(x1, g2_ref[...]).astype(BF16)
    hid = jax.nn.silu(_dot(h2, wfg_ref[...])) * _dot(h2, wfu_ref[...])
    o_ref[0] = x1 + _dot(hid.astype(BF16), wfd_ref[...])


def _gate_repack_kernel(a_ref, b_ref, c_ref, o_ref):
    off = GATE_COL0 % GATE_RB
    rows = jnp.concatenate([a_ref[off:], b_ref[...], c_ref[:off]], axis=0)
    o_ref[...] = rows.T.astype(BF16)


def _const_spec(shape):
    nd = len(shape)
    return pl.BlockSpec(shape, lambda *_: (0,) * nd, pipeline_mode=pl.Buffered(1))


def _block_diag(blocks):
    g, r, c = blocks.shape
    eye = jnp.eye(g, dtype=blocks.dtype)
    return jnp.einsum('grc,gh->grhc', blocks, eye).reshape(g * r, g * c)


def kernel(x, norm1_g, w_in, A_re, A_im, log_dt, B_re, B_im, C_re, C_im, D_skip, w_glu, b_glu, q_norm_g, k_norm_g,
           idx_k_norm_g, w_proj_ssm, w_proj_attn, w_out, norm2_g, w_ffn_gate, w_ffn_up, w_ffn_down):
    bsz, seq, _ = x.shape
    assert x.shape == (8, 2048, D_MODEL) and w_in.shape[0] == 1
    G, N, P = SSM_GROUPS, SSM_STATE, SSM_GROUP
    cparams = functools.partial(pltpu.CompilerParams, vmem_limit_bytes=VMEM_LIMIT)

    w_in_t = w_in[0].T
    gblk = GATE_COL0 // GATE_RB
    spec = lambda k: pl.BlockSpec((GATE_RB, D_MODEL), lambda j: (gblk + 2 * j + k, 0))
    w_gates = pl.pallas_call(
        _gate_repack_kernel,
        grid=(2 * D_MODEL // LANES,),
        in_specs=[spec(0), spec(1), spec(2)],
        out_specs=pl.BlockSpec((D_MODEL, LANES), lambda j: (0, j)),
        out_shape=jax.ShapeDtypeStruct((D_MODEL, 2 * D_MODEL), BF16),
        name="gate_weight_repack",
    )(w_in_t, w_in_t, w_in_t)
    g1 = norm1_g[0].reshape(1, D_MODEL)
    ones64 = jnp.ones((HEAD_DIM,), F32)

    rep = lambda a: jnp.repeat(a, P, axis=0)
    gpn = jax.ShapeDtypeStruct((G * P, N), F32)
    abar_r, abar_i, bp_r, bp_i = pl.pallas_call(
        _disc_kernel, out_shape=(gpn, gpn, gpn, gpn), name="s5_discretise",
    )(rep(A_re[0]), rep(A_im[0]), rep(jnp.broadcast_to(log_dt[0].reshape(G, 1), (G, N))),
      B_re[0].transpose(0, 2, 1).reshape(G * P, N), B_im[0].transpose(0, 2, 1).reshape(G * P, N))
    abar_r, abar_i = abar_r[::P], abar_i[::P]
    bp_r, bp_i = bp_r.reshape(G, P, N), bp_i.reshape(G, P, N)
    hg = G // 2
    bp = [jnp.concatenate([_block_diag(bp_r[sl]), _block_diag(bp_i[sl])], axis=1).astype(BF16)
          for sl in (slice(0, hg), slice(hg, G))]
    c_r = C_re[0].transpose(0, 2, 1)
    c_i = C_im[0].transpose(0, 2, 1)
    cp = [jnp.concatenate([_block_diag(c_r[sl]), -_block_diag(c_i[sl])], axis=0).astype(BF16)
          for sl in (slice(0, hg), slice(hg, G))]
    ar8 = jnp.broadcast_to(abar_r.reshape(1, G * N), (SUBLANES, G * N))
    ai8 = jnp.broadcast_to(abar_i.reshape(1, G * N), (SUBLANES, G * N))

    n_in = seq // TL_IN
    u_t, q_h, qs_h, sg_t, k_n, ki_n, v_t = pl.pallas_call(
        _inproj_kernel,
        grid=(bsz, n_in),
        in_specs=[
            pl.BlockSpec((1, TL_IN, D_MODEL), lambda b, i: (b, i, 0)),
            _const_spec((1, D_MODEL)),
            _const_spec((W1_COLS, D_MODEL)),
            _const_spec((1, ATTN_WIDTH)), _const_spec((1, LANES)), _const_spec((1, LANES)),
            _const_spec((2 * LANES, 2 * LANES)),
        ],
        out_specs=[
            pl.BlockSpec((1, TL_IN, SSM_WIDTH), lambda b, i: (b, i, 0)),
            pl.BlockSpec((1, N_HEADS, K_EXT, TL_IN), lambda b, i: (b, 0, 0, i)),
            pl.BlockSpec((1, IDX_HEADS, IDX_DIM, TL_IN), lambda b, i: (b, 0, 0, i)),
            pl.BlockSpec((1, SUBLANES, TL_IN), lambda b, i: (b, 0, i)),
            pl.BlockSpec((1, TL_IN, K_EXT), lambda b, i: (b, i, 0)),
            pl.BlockSpec((1, TL_IN, IDX_DIM), lambda b, i: (b, i, 0)),
            pl.BlockSpec((1, TL_IN // TK, HEAD_DIM, TK), lambda b, i: (b, i, 0, 0)),
        ],
        out_shape=(
            jax.ShapeDtypeStruct((bsz, seq, SSM_WIDTH), F32),
            jax.ShapeDtypeStruct((bsz, N_HEADS, K_EXT, seq), BF16),
            jax.ShapeDtypeStruct((bsz, IDX_HEADS, IDX_DIM, seq), BF16),
            jax.ShapeDtypeStruct((bsz, SUBLANES, seq), F32),
            jax.ShapeDtypeStruct((bsz, seq, K_EXT), BF16),
            jax.ShapeDtypeStruct((bsz, seq, IDX_DIM), BF16),
            jax.ShapeDtypeStruct((bsz, seq // TK, HEAD_DIM, TK), BF16),
        ),
        scratch_shapes=[pltpu.VMEM((D_MODEL, W1_COLS), BF16)],
        compiler_params=cparams(dimension_semantics=("arbitrary", "arbitrary")),
        name="in_projection",
    )(x, g1, w_in_t, jnp.tile(q_norm_g[0], N_HEADS).reshape(1, ATTN_WIDTH),
      jnp.concatenate([k_norm_g[0], ones64]).reshape(1, LANES),
      jnp.concatenate([idx_k_norm_g[0], ones64]).reshape(1, LANES),
      jnp.kron(jnp.eye(2 * LANES // HEAD_DIM, dtype=BF16), jnp.ones((HEAD_DIM, HEAD_DIM), BF16)))

    rows = SUBLANES * TC_SCAN
    ps_t = pl.pallas_call(
        _s5_kernel,
        grid=(seq // TC_SCAN,),
        in_specs=[
            pl.BlockSpec((bsz, TC_SCAN, SSM_WIDTH), lambda c: (0, c, 0)),
            _const_spec((256, 2048)), _const_spec((256, 2048)),
            _const_spec((2048, 256)), _const_spec((2048, 256)),
            _const_spec((SUBLANES, G * N)), _const_spec((SUBLANES, G * N)),
            _const_spec((1, SSM_WIDTH)), _const_spec((SSM_WIDTH, SSM_WIDTH)), _const_spec((1, SSM_WIDTH)),
            _const_spec((SSM_WIDTH, D_MODEL)),
        ],
        out_specs=pl.BlockSpec((bsz, TC_SCAN, D_MODEL), lambda c: (0, c, 0)),
        out_shape=jax.ShapeDtypeStruct((bsz, seq, D_MODEL), BF16),
        scratch_shapes=[pltpu.VMEM((rows, 2 * G * N), F32), pltpu.VMEM((SUBLANES, 2 * G * N), F32)],
        compiler_params=cparams(dimension_semantics=("arbitrary",)),
        name="s5_branch",
    )(u_t, bp[0], bp[1], cp[0], cp[1], ar8, ai8,
      D_skip[0].reshape(1, SSM_WIDTH), w_glu[0].astype(BF16), b_glu[0].reshape(1, SSM_WIDTH),
      w_proj_ssm[0].astype(BF16))

    y_att = pl.pallas_call(
        _attn_kernel,
        grid=(bsz, seq // QB),
        in_specs=[
            pl.BlockSpec((1, IDX_HEADS, IDX_DIM, QB), lambda b, i: (b, 0, 0, i)),
            pl.BlockSpec((1, SUBLANES, QB), lambda b, i: (b, 0, i)),
            pl.BlockSpec((1, N_HEADS, K_EXT, QB), lambda b, i: (b, 0, 0, i)),
            pl.BlockSpec((1, seq, IDX_DIM), lambda b, i: (b, 0, 0)),
            pl.BlockSpec((1, seq, K_EXT), lambda b, i: (b, 0, 0)),
            pl.BlockSpec((1, seq // TK, HEAD_DIM, TK), lambda b, i: (b, 0, 0, 0)),
        ],
        out_specs=pl.BlockSpec((1, QB, ATTN_WIDTH), lambda b, i: (b, i, 0)),
        out_shape=jax.ShapeDtypeStruct((bsz, seq, ATTN_WIDTH), BF16),
        scratch_shapes=[pltpu.VMEM((seq // TK, TK, QB), F32),
                        pltpu.VMEM((N_HEADS, SUBLANES, QB), F32),
                        pltpu.VMEM((N_HEADS, SUBLANES, QB), F32),
                        pltpu.VMEM((N_HEADS, HEAD_DIM, QB), F32),
                        pltpu.VMEM((N_HEADS // 2, TK, 2 * QB), F32),
                        pltpu.VMEM((N_HEADS, K_EXT, QB), BF16),
                        pltpu.SMEM((1,), F32),
                        pltpu.VMEM((seq // TK, TK, QB), BF16)],
        compiler_params=cparams(dimension_semantics=("parallel", "arbitrary")),
        name="sparse_attention",
    )(qs_h, sg_t, q_h, ki_n, k_n, v_t)

    out = pl.pallas_call(
        _ffn_kernel,
        grid=(bsz, seq // TL_FFN),
        in_specs=[
            pl.BlockSpec((1, TL_FFN, D_MODEL), lambda b, i: (b, i, 0)),
            pl.BlockSpec((1, TL_FFN, D_MODEL), lambda b, i: (b, i, 0)),
            pl.BlockSpec((1, TL_FFN, ATTN_WIDTH), lambda b, i: (b, i, 0)),
            _const_spec((1, D_MODEL)),
            _const_spec((D_MODEL, 2 * D_MODEL)),
            _const_spec((ATTN_WIDTH, D_MODEL)),
            _const_spec((D_MODEL, D_MODEL)),
            _const_spec((1, D_MODEL)),
            _const_spec((D_MODEL, D_FF)), _const_spec((D_MODEL, D_FF)), _const_spec((D_FF, D_MODEL)),
        ],
        out_specs=pl.BlockSpec((1, TL_FFN, D_MODEL), lambda b, i: (b, i, 0)),
        out_shape=jax.ShapeDtypeStruct((bsz, seq, D_MODEL), F32),
        compiler_params=cparams(dimension_semantics=("parallel", "parallel")),
        name="merge_ffn",
    )(x, ps_t, y_att, g1, w_gates, w_proj_attn[0].astype(BF16),
      w_out[0].astype(BF16), norm2_g[0].reshape(1, D_MODEL), w_ffn_gate[0].astype(BF16),
      w_ffn_up[0].astype(BF16), w_ffn_down[0].astype(BF16))
    return out
```

```python
import functools
import math

import jax
import jax.numpy as jnp
from jax import lax
from jax.experimental import pallas as pl
from jax.experimental.pallas import tpu as pltpu

F32 = jnp.float32
BF16 = jnp.bfloat16

D_MODEL = 1024
SSM_WIDTH = 512
SSM_GROUP = 16
SSM_GROUPS = 32
SSM_STATE = 64
N_HEADS = 8
HEAD_DIM = 64
ATTN_WIDTH = 512
IDX_HEADS = 4
IDX_DIM = 64
INDEX_TOPK = 256
D_FF = 2816
RMS_EPS = 1e-6

LANES = 128
SUBLANES = 8
VMEM_LIMIT = 56 * 1024 * 1024

TL_IN = 512
TC_SCAN = 128
QB = 256
TK = 256
TL_FFN = 256

W1_COLS = 1536
GATE_COL0 = 1476
GATE_RB = 64
NEG_BIG = -1e30
K_EXT = 80
Q_SHIFT_ROW = 64
NORM_UP = 1.0 + 2.0 ** -7
L_FLOOR = 2.0 ** -100
assert QB % LANES == 0


def _dot(a, b):
    return jnp.dot(a, b, preferred_element_type=F32)


def _rms(x, g):
    return x * lax.rsqrt(jnp.mean(x * x, axis=-1, keepdims=True) + RMS_EPS) * g


def _disc_kernel(are_ref, aim_ref, ldt_ref, bre_ref, bim_ref, cre_ref, cim_ref,
                 ar8_ref, ai8_ref, bplo_ref, bphi_ref, cplo_ref, cphi_ref):
    G, N, P = SSM_GROUPS, SSM_STATE, SSM_GROUP
    ar = are_ref[...]
    ai = aim_ref[...]
    dt = jnp.exp(ldt_ref[...])
    mag = jnp.exp(ar * dt)
    abar_r = mag * jnp.cos(ai * dt)
    abar_i = mag * jnp.sin(ai * dt)
    den = ar * ar + ai * ai
    nr = abar_r - 1.0
    coef_r = (nr * ar + abar_i * ai) / den
    coef_i = (abar_i * ar - nr * ai) / den
    for g in range(G):
        ar8_ref[:, g * N:(g + 1) * N] = abar_r[g * P:g * P + SUBLANES]
        ai8_ref[:, g * N:(g + 1) * N] = abar_i[g * P:g * P + SUBLANES]
    br = bre_ref[...]
    bi = bim_ref[...]
    bpr = (coef_r * br - coef_i * bi).astype(BF16)
    bpi = (coef_r * bi + coef_i * br).astype(BF16)
    cre = cre_ref[...].astype(BF16)
    cim = (-cim_ref[...]).astype(BF16)

    def spread(x, reps):
        w = x.shape[1]
        r = lax.broadcasted_iota(jnp.int32, (w, w * reps), 0)
        c = lax.broadcasted_iota(jnp.int32, (w, w * reps), 1)
        return _dot(x, jnp.where(c % w == r, 1.0, 0.0).astype(BF16))

    hg = G // 2
    rb = lax.broadcasted_iota(jnp.int32, (hg * P, hg * N), 0) // P
    cb = lax.broadcasted_iota(jnp.int32, (hg * P, hg * N), 1) // N
    rc = lax.broadcasted_iota(jnp.int32, (hg * N, hg * P), 0) // N
    cc = lax.broadcasted_iota(jnp.int32, (hg * N, hg * P), 1) // P
    for h, (bp_ref, cp_ref) in enumerate(((bplo_ref, cplo_ref), (bphi_ref, cphi_ref))):
        rows_b = slice(h * hg * P, (h + 1) * hg * P)
        rows_c = slice(h * hg * N, (h + 1) * hg * N)
        bp_ref[:, 0:hg * N] = jnp.where(rb == cb, spread(bpr[rows_b], hg), 0.0).astype(BF16)
        bp_ref[:, hg * N:2 * hg * N] = jnp.where(rb == cb, spread(bpi[rows_b], hg), 0.0).astype(BF16)
        cp_ref[0:hg * N, :] = jnp.where(rc == cc, spread(cre[rows_c], hg), 0.0).astype(BF16)
        cp_ref[hg * N:2 * hg * N, :] = jnp.where(rc == cc, spread(cim[rows_c], hg), 0.0).astype(BF16)


def _inproj_kernel(x_ref, g1_ref, w_ref, qg_ref, kg_ref, ikg_ref, seg_ref,
                   u_ref, q_ref, qs_ref, sg_ref, k_ref, ki_ref, v_ref, w_scr):
    @pl.when(jnp.logical_and(pl.program_id(0) == 0, pl.program_id(1) == 0))
    def _():
        w_scr[...] = w_ref[...].T.astype(BF16)

    x = x_ref[0]
    h = _rms(x, g1_ref[...]).astype(BF16)
    proj = _dot(h, w_scr[...])
    u_ref[0] = proj[:, 0:512]
    seg = seg_ref[...]

    def head_rms(v, g):
        sq = (v * v).astype(BF16)
        n = v.shape[1]
        w = min(n, seg.shape[0])
        ss = jnp.concatenate([_dot(sq[:, c:c + w], seg[0:w, 0:w]) for c in range(0, n, w)], axis=1)
        return v * lax.rsqrt(ss * (1.0 / HEAD_DIM) + RMS_EPS) * g

    def store_heads_transposed(ref, v, with_shift_rows=False):
        for g in range(v.shape[1] // LANES):
            tg = v[:, g * LANES:(g + 1) * LANES].T
            for e in range(2):
                th = tg[e * HEAD_DIM:(e + 1) * HEAD_DIM].astype(BF16)
                ref[0, 2 * g + e, 0:HEAD_DIM] = th
                if with_shift_rows:
                    tf = th.astype(F32)
                    nrm = jnp.sqrt(jnp.sum(tf * tf, axis=0, keepdims=True)) * NORM_UP
                    pad = (K_EXT - HEAD_DIM, th.shape[1])
                    first = lax.broadcasted_iota(jnp.int32, pad, 0) == 0
                    ref[0, 2 * g + e, HEAD_DIM:K_EXT] = jnp.where(first, -nrm, 0.0).astype(BF16)

    qscale = (HEAD_DIM ** -0.5) * math.log2(math.e)
    store_heads_transposed(q_ref, head_rms(proj[:, 512:1024], qg_ref[...]) * qscale, with_shift_rows=True)

    kv = proj[:, 1024:1152]
    kiw = proj[:, 1408:1536]
    w_scale = (IDX_HEADS ** -0.5) * (IDX_DIM ** -0.5)
    wabs = jnp.abs(kiw) * w_scale
    sshape = (LANES, IDX_HEADS * IDX_DIM)
    row = lax.broadcasted_iota(jnp.int32, sshape, 0)
    col = lax.broadcasted_iota(jnp.int32, sshape, 1)
    spread = jnp.where(row == IDX_DIM + col // IDX_DIM, 1.0, 0.0).astype(BF16)
    whi = wabs.astype(BF16)
    wlo = (wabs - whi.astype(F32)).astype(BF16)
    wrep = _dot(whi, spread) + _dot(wlo, spread)
    store_heads_transposed(qs_ref, proj[:, 1152:1408] * wrep)

    lane = lax.broadcasted_iota(jnp.int32, kv.shape, 1)
    k_ref[0] = jnp.where(lane < HEAD_DIM, head_rms(kv, kg_ref[...]), 0.0)[:, 0:K_EXT].astype(BF16)
    ki_ref[0] = head_rms(kiw, ikg_ref[...])[:, 0:IDX_DIM].astype(BF16)
    t_kv = kv.T
    t_kiw = kiw.T
    sg_ref[0] = jnp.where(t_kiw[IDX_DIM:IDX_DIM + SUBLANES] >= 0, 1.0, -1.0)
    for c in range(TL_IN // TK):
        v_ref[0, c] = t_kv[HEAD_DIM:2 * HEAD_DIM, c * TK:(c + 1) * TK].astype(BF16)


def _s5_kernel(u_ref, bplo_ref, bphi_ref, cplo_ref, cphi_ref, ar_ref, ai_ref, d_ref, wglu_ref, bglu_ref,
               wps_ref, o_ref, bu_scr, st_scr):
    @pl.when(pl.program_id(0) == 0)
    def _():
        st_scr[...] = jnp.zeros_like(st_scr)

    u = jnp.transpose(u_ref[...], (1, 0, 2)).reshape(SUBLANES * TC_SCAN, SSM_WIDTH)
    ub = u.astype(BF16)
    half_w = 2 * (SSM_GROUPS // 2) * SSM_STATE
    n_re = half_w // 2

    def scan_half(h):
        c0 = h * half_w
        a_r = ar_ref[:, h * n_re:(h + 1) * n_re]
        a_i = ai_ref[:, h * n_re:(h + 1) * n_re]
        sr = st_scr[:, c0:c0 + n_re]
        si = st_scr[:, c0 + n_re:c0 + half_w]
        for t in range(TC_SCAN):
            rows = slice(t * SUBLANES, (t + 1) * SUBLANES)
            br = bu_scr[rows, c0:c0 + n_re]
            bi = bu_scr[rows, c0 + n_re:c0 + half_w]
            sr, si = a_r * sr - a_i * si + br, a_r * si + a_i * sr + bi
            bu_scr[rows, c0:c0 + n_re] = sr
            bu_scr[rows, c0 + n_re:c0 + half_w] = si
        st_scr[:, c0:c0 + n_re] = sr
        st_scr[:, c0 + n_re:c0 + half_w] = si

    bu_scr[:, 0:half_w] = _dot(ub[:, 0:256], bplo_ref[...])
    bu_scr[:, half_w:2 * half_w] = _dot(ub[:, 256:512], bphi_ref[...])
    scan_half(0)
    y_lo = _dot(bu_scr[:, 0:half_w].astype(BF16), cplo_ref[...])
    scan_half(1)
    y_hi = _dot(bu_scr[:, half_w:2 * half_w].astype(BF16), cphi_ref[...])
    y = jnp.concatenate([y_lo, y_hi], axis=1)
    y = jax.nn.gelu(y + d_ref[...] * u)
    z = _dot(y.astype(BF16), wglu_ref[...]) + bglu_ref[...]
    y = y * jax.nn.sigmoid(z)
    ps = _dot(y.astype(BF16), wps_ref[...])
    o_ref[...] = jnp.transpose(ps.reshape(TC_SCAN, SUBLANES, D_MODEL), (1, 0, 2)).astype(BF16)


def _key_to_f32(key):
    bits = jnp.where(key < 0, key & jnp.int32(0x7FFFFFFF), ~key)
    return pltpu.bitcast(bits, F32)


def _attn_kernel(qs_ref, sg_ref, q_ref, ki_ref, k_ref, vt_ref, o_ref,
                 s_scr, m_scr, l_scr, acc_scr, sp_scr, qmod_scr, kmax_scr, sb_scr):
    i = pl.program_id(1)
    n_tiles = (i * QB + QB + TK - 1) // TK
    nv = TK // SUBLANES
    shape3 = (nv, SUBLANES, QB)
    k_in_tile = (lax.broadcasted_iota(jnp.int32, shape3, 0) * SUBLANES
                 + lax.broadcasted_iota(jnp.int32, shape3, 1))
    q_pos = i * QB + lax.broadcasted_iota(jnp.int32, shape3, 2)
    ksel = float(INDEX_TOPK)

    def all_sublanes(a, op):
        for sh in (4, 2, 1):
            a = op(a, pltpu.roll(a, sh, 0))
        return a

    sg = sg_ref[0]

    def idx_body(j, carry):
        kt = ki_ref[0, pl.ds(pl.multiple_of(j * TK, TK), TK), :]
        acc = jnp.zeros((TK, QB), F32)
        for pr in range(IDX_HEADS // 2):
            x = _dot(kt, jnp.concatenate([qs_ref[0, 2 * pr], qs_ref[0, 2 * pr + 1]], axis=1))
            for e in range(2):
                hd = 2 * pr + e
                acc = acc + sg[hd:hd + 1, :] * jnp.maximum(x[:, e * QB:(e + 1) * QB], 0.0)
        vis = j * TK + k_in_tile <= q_pos
        sc = jnp.where(vis, acc.reshape(shape3), -jnp.inf).reshape(TK, QB)
        s_scr[j] = sc
        sb_scr[j] = sc.astype(BF16)
        return carry

    def for_each_tile(body):
        def pair(jj, carry):
            body(2 * jj, carry)
            return body(2 * jj + 1, carry)
        lax.fori_loop(0, n_tiles // 2, pair, 0)

        @pl.when(n_tiles % 2 == 1)
        def _():
            body(n_tiles - 1, 0)

    for_each_tile(idx_body)

    def count(pred):
        def tile(j, acc):
            hit = jnp.where(pred(s_scr[j].reshape(shape3), j), 1.0, 0.0)
            return acc + jnp.sum(hit.reshape(nv // 4, 4, SUBLANES, QB), axis=0)
        acc = lax.fori_loop(0, n_tiles // 2, lambda jj, a: tile(2 * jj + 1, tile(2 * jj, a)),
                            jnp.zeros((4, SUBLANES, QB), F32))
        acc = lax.cond(n_tiles % 2 == 1, lambda a: tile(n_tiles - 1, a), lambda a: a, acc)
        return all_sublanes(jnp.sum(acc, axis=0), jnp.add)

    every = float(2 * INDEX_TOPK * 1024)
    nv16 = TK // (2 * SUBLANES)
    one_b, zero_b = jnp.ones((), BF16), jnp.zeros((), BF16)

    def count_hi(gc):
        def tile(j, acc):
            hit = jnp.where(sb_scr[j].reshape(nv16, 2 * SUBLANES, QB) >= gc, one_b, zero_b)
            for k in range(nv16):
                acc[k % 4] = acc[k % 4] + hit[k]
            return acc
        acc = lax.fori_loop(0, n_tiles // 2, lambda jj, a: tuple(tile(2 * jj + 1, tile(2 * jj, list(a)))),
                            tuple(jnp.zeros((2 * SUBLANES, QB), BF16) for _ in range(4)))
        acc = lax.cond(n_tiles % 2 == 1, lambda a: tuple(tile(n_tiles - 1, list(a))), lambda a: a, acc)
        tot = (acc[0].astype(F32) + acc[1].astype(F32)) + (acc[2].astype(F32) + acc[3].astype(F32))
        return all_sublanes(tot[0:SUBLANES] + tot[SUBLANES:2 * SUBLANES], jnp.add)

    def grid_key(k16):
        return lax.shift_left(k16, 16) | jnp.where(k16 < 0x8000, 0xFFFF, 0)

    def hi_body(it, k16):
        cand = k16 | lax.shift_left(jnp.int32(1), 15 - it)
        g = _key_to_f32(grid_key(cand))
        n = count_hi(jnp.concatenate([g, g], axis=0).astype(BF16))
        cnt = jnp.where(lax.shift_right_logical(cand, 7) == 0, every, n)
        return jnp.where(cnt >= ksel, cand, k16)

    def lo_body(_, carry):
        lo, hi, cnt_lo = carry
        mid = lo + lax.shift_right_logical(hi - lo, 1)
        tc = _key_to_f32(mid)
        cnt = jnp.where(lax.shift_right_logical(mid, 23) == 0, every, count(lambda s, j: s >= tc))
        ok = cnt >= ksel
        return jnp.where(ok, mid, lo), jnp.where(ok, hi, mid), jnp.where(ok, cnt, cnt_lo)

    def write_bias(select):
        def body(j, carry):
            s = s_scr[j].reshape(shape3)
            vis = j * TK + k_in_tile <= q_pos
            bias = jnp.where(vis, jnp.where(select(s, j), 0.0, -jnp.inf), -jnp.inf)
            s_scr[j] = bias.reshape(TK, QB)
            return carry
        for_each_tile(body)

    few_keys = (i + 1) * QB <= INDEX_TOPK

    @pl.when(few_keys)
    def _():
        write_bias(lambda s, j: jnp.ones(s.shape, jnp.bool_))

    @pl.when(jnp.logical_not(few_keys))
    def _():
        k16 = lax.fori_loop(0, 16, hi_body, jnp.zeros((SUBLANES, QB), jnp.int32))
        key, _, cnt_ge = lax.fori_loop(
            0, 17, lo_body,
            (grid_key(k16) - 0x8001, grid_key(jnp.minimum(k16 + 1, 0xFFFF)),
             jnp.full((SUBLANES, QB), every, F32)))
        thr = _key_to_f32(key)
        has_excess = jnp.max(cnt_ge) > ksel

        @pl.when(jnp.logical_not(has_excess))
        def _():
            write_bias(lambda s, j: s >= thr)

        @pl.when(has_excess)
        def _():
            cnt_gt = count(lambda s, j: s > thr)
            need = ksel - cnt_gt

            def jbit_body(it, jkey):
                cand = jkey | lax.shift_left(jnp.int32(1), 10 - it)
                cnt = count(lambda s, j: (s == thr) & (j * TK + k_in_tile < cand))
                return jnp.where(cnt < need, cand, jkey)

            jkey = lax.fori_loop(0, 11, jbit_body, jnp.zeros((SUBLANES, QB), jnp.int32))
            write_bias(lambda s, j: (s > thr) | ((s == thr) & (j * TK + k_in_tile <= jkey)))

    lane_k = lax.broadcasted_iota(jnp.int32, (TK, K_EXT), 1)

    def attend(get_q, k_fill):
        l_scr[...] = jnp.zeros(l_scr.shape, F32)
        acc_scr[...] = jnp.zeros(acc_scr.shape, F32)

        def body(j, carry):
            kt = k_ref[0, pl.ds(pl.multiple_of(j * TK, TK), TK), :]
            kt = jnp.where(lane_k == Q_SHIFT_ROW, jnp.full((TK, K_EXT), k_fill, F32).astype(BF16), kt)
            vt = vt_ref[0, j]
            bias = s_scr[j].reshape(shape3)
            for pr in range(N_HEADS // 2):
                sp_scr[pr] = _dot(kt, jnp.concatenate([get_q(2 * pr), get_q(2 * pr + 1)], axis=1))
            for pr in range(N_HEADS // 2):
                for e in range(2):
                    hd = 2 * pr + e
                    p = jnp.exp2(sp_scr[pr, :, e * QB:(e + 1) * QB].reshape(shape3) + bias)
                    l_scr[hd] = l_scr[hd] + jnp.sum(p, axis=0)
                    acc_scr[hd] = acc_scr[hd] + _dot(vt, p.reshape(TK, QB).astype(BF16))
            return carry

        for_each_tile(body)

    @pl.when(i == 0)
    def _():
        kf = k_ref[0].astype(F32)
        kmax_scr[0] = jnp.max(jnp.sqrt(jnp.sum(kf * kf, axis=1, keepdims=True))) * NORM_UP

    attend(lambda hd: q_ref[0, hd], kmax_scr[0])

    l_min = all_sublanes(l_scr[0], jnp.add)
    for hd in range(1, N_HEADS):
        l_min = jnp.minimum(l_min, all_sublanes(l_scr[hd], jnp.add))
    underflow = jnp.logical_not(jnp.min(l_min) >= L_FLOOR)

    @pl.when(underflow)
    def _():
        row_q = lax.broadcasted_iota(jnp.int32, (K_EXT, QB), 0)
        m_scr[...] = jnp.full(m_scr.shape, NEG_BIG, F32)

        def max_body(j, carry):
            kt = k_ref[0, pl.ds(pl.multiple_of(j * TK, TK), TK), :]
            kt = jnp.where(lane_k == Q_SHIFT_ROW, jnp.zeros((TK, K_EXT), BF16), kt)
            bias = s_scr[j].reshape(shape3)
            for hd in range(N_HEADS):
                s = _dot(kt, q_ref[0, hd]).reshape(shape3) + bias
                m_scr[hd] = jnp.maximum(m_scr[hd], all_sublanes(jnp.max(s, axis=0), jnp.maximum))
            return carry

        lax.fori_loop(0, n_tiles, max_body, 0)
        for hd in range(N_HEADS):
            shift = jnp.broadcast_to(-m_scr[hd][0:1, :], (K_EXT, QB)).astype(BF16)
            qmod_scr[hd] = jnp.where(row_q == Q_SHIFT_ROW, shift, q_ref[0, hd])
        attend(lambda hd: qmod_scr[hd], 1.0)

    outs = []
    for hd in range(N_HEADS):
        l = all_sublanes(l_scr[hd], jnp.add)
        o = acc_scr[hd].reshape(HEAD_DIM // SUBLANES, SUBLANES, QB) / l
        outs.append(o.reshape(HEAD_DIM, QB))
    o_ref[0] = jnp.concatenate(outs, axis=0).T.astype(BF16)


def _ffn_kernel(x_ref, ps_ref, ya_ref, g1_ref, wgate_ref, wpa_ref, wo_ref, g2_ref, wfg_ref, wfu_ref, wfd_ref,
                o_ref):
    x = x_ref[0]
    h = _rms(x, g1_ref[...]).astype(BF16)
    gates = _dot(h, wgate_ref[...])
    pa = _dot(ya_ref[0], wpa_ref[...])
    merged = (jax.nn.sigmoid(gates[:, 0:D_MODEL]) * ps_ref[0].astype(F32)
              + jax.nn.sigmoid(gates[:, D_MODEL:2 * D_MODEL]) * pa)
    x1 = x + _dot(merged.astype(BF16), wo_ref[...])
    h2 = _rms(x1, g2_ref[...]).astype(BF16)
    hid = jax.nn.silu(_dot(h2, wfg_ref[...])) * _dot(h2, wfu_ref[...])
    o_ref[0] = x1 + _dot(hid.astype(BF16), wfd_ref[...])


def _gate_repack_kernel(a_ref, b_ref, c_ref, o_ref):
    off = GATE_COL0 % GATE_RB
    rows = jnp.concatenate([a_ref[off:], b_ref[...], c_ref[:off]], axis=0)
    o_ref[...] = rows.T.astype(BF16)


def _const_spec(shape):
    nd = len(shape)
    return pl.BlockSpec(shape, lambda *_: (0,) * nd, pipeline_mode=pl.Buffered(1))


def kernel(x, norm1_g, w_in, A_re, A_im, log_dt, B_re, B_im, C_re, C_im, D_skip, w_glu, b_glu, q_norm_g, k_norm_g,
           idx_k_norm_g, w_proj_ssm, w_proj_attn, w_out, norm2_g, w_ffn_gate, w_ffn_up, w_ffn_down):
    bsz, seq, _ = x.shape
    assert x.shape == (8, 2048, D_MODEL) and w_in.shape[0] == 1
    G, N, P = SSM_GROUPS, SSM_STATE, SSM_GROUP
    cparams = functools.partial(pltpu.CompilerParams, vmem_limit_bytes=VMEM_LIMIT)

    w_in_t = w_in[0].T
    gblk = GATE_COL0 // GATE_RB
    spec = lambda k: pl.BlockSpec((GATE_RB, D_MODEL), lambda j: (gblk + 2 * j + k, 0))
    w_gates = pl.pallas_call(
        _gate_repack_kernel,
        grid=(2 * D_MODEL // LANES,),
        in_specs=[spec(0), spec(1), spec(2)],
        out_specs=pl.BlockSpec((D_MODEL, LANES), lambda j: (0, j)),
        out_shape=jax.ShapeDtypeStruct((D_MODEL, 2 * D_MODEL), BF16),
        name="gate_weight_repack",
    )(w_in_t, w_in_t, w_in_t)
    g1 = norm1_g[0].reshape(1, D_MODEL)
    ones64 = jnp.ones((HEAD_DIM,), F32)

    rep = lambda a: jnp.repeat(a, P, axis=0)
    hgn, hgp = G // 2 * N, G // 2 * P
    ar8, ai8, bp_lo, bp_hi, cp_lo, cp_hi = pl.pallas_call(
        _disc_kernel,
        out_shape=(jax.ShapeDtypeStruct((SUBLANES, G * N), F32), jax.ShapeDtypeStruct((SUBLANES, G * N), F32),
                   jax.ShapeDtypeStruct((hgp, 2 * hgn), BF16), jax.ShapeDtypeStruct((hgp, 2 * hgn), BF16),
                   jax.ShapeDtypeStruct((2 * hgn, hgp), BF16), jax.ShapeDtypeStruct((2 * hgn, hgp), BF16)),
        name="s5_discretise",
    )(rep(A_re[0]), rep(A_im[0]), rep(jnp.broadcast_to(log_dt[0].reshape(G, 1), (G, N))),
      B_re[0].transpose(0, 2, 1).reshape(G * P, N), B_im[0].transpose(0, 2, 1).reshape(G * P, N),
      C_re[0].transpose(0, 2, 1).reshape(G * N, P), C_im[0].transpose(0, 2, 1).reshape(G * N, P))
    bp, cp = (bp_lo, bp_hi), (cp_lo, cp_hi)

    n_in = seq // TL_IN
    u_t, q_h, qs_h, sg_t, k_n, ki_n, v_t = pl.pallas_call(
        _inproj_kernel,
        grid=(bsz, n_in),
        in_specs=[
            pl.BlockSpec((1, TL_IN, D_MODEL), lambda b, i: (b, i, 0)),
            _const_spec((1, D_MODEL)),
            _const_spec((W1_COLS, D_MODEL)),
            _const_spec((1, ATTN_WIDTH)), _const_spec((1, LANES)), _const_spec((1, LANES)),
            _const_spec((2 * LANES, 2 * LANES)),
        ],
        out_specs=[
            pl.BlockSpec((1, TL_IN, SSM_WIDTH), lambda b, i: (b, i, 0)),
            pl.BlockSpec((1, N_HEADS, K_EXT, TL_IN), lambda b, i: (b, 0, 0, i)),
            pl.BlockSpec((1, IDX_HEADS, IDX_DIM, TL_IN), lambda b, i: (b, 0, 0, i)),
            pl.BlockSpec((1, SUBLANES, TL_IN), lambda b, i: (b, 0, i)),
            pl.BlockSpec((1, TL_IN, K_EXT), lambda b, i: (b, i, 0)),
            pl.BlockSpec((1, TL_IN, IDX_DIM), lambda b, i: (b, i, 0)),
            pl.BlockSpec((1, TL_IN // TK, HEAD_DIM, TK), lambda b, i: (b, i, 0, 0)),
        ],
        out_shape=(
            jax.ShapeDtypeStruct((bsz, seq, SSM_WIDTH), F32),
            jax.ShapeDtypeStruct((bsz, N_HEADS, K_EXT, seq), BF16),
            jax.ShapeDtypeStruct((bsz, IDX_HEADS, IDX_DIM, seq), BF16),
            jax.ShapeDtypeStruct((bsz, SUBLANES, seq), F32),
            jax.ShapeDtypeStruct((bsz, seq, K_EXT), BF16),
            jax.ShapeDtypeStruct((bsz, seq, IDX_DIM), BF16),
            jax.ShapeDtypeStruct((bsz, seq // TK, HEAD_DIM, TK), BF16),
        ),
        scratch_shapes=[pltpu.VMEM((D_MODEL, W1_COLS), BF16)],
        compiler_params=cparams(dimension_semantics=("arbitrary", "arbitrary")),
        name="in_projection",
    )(x, g1, w_in_t, jnp.tile(q_norm_g[0], N_HEADS).reshape(1, ATTN_WIDTH),
      jnp.concatenate([k_norm_g[0], ones64]).reshape(1, LANES),
      jnp.concatenate([idx_k_norm_g[0], ones64]).reshape(1, LANES),
      jnp.kron(jnp.eye(2 * LANES // HEAD_DIM, dtype=BF16), jnp.ones((HEAD_DIM, HEAD_DIM), BF16)))

    rows = SUBLANES * TC_SCAN
    ps_t = pl.pallas_call(
        _s5_kernel,
        grid=(seq // TC_SCAN,),
        in_specs=[
            pl.BlockSpec((bsz, TC_SCAN, SSM_WIDTH), lambda c: (0, c, 0)),
            _const_spec((256, 2048)), _const_spec((256, 2048)),
            _const_spec((2048, 256)), _const_spec((2048, 256)),
            _const_spec((SUBLANES, G * N)), _const_spec((SUBLANES, G * N)),
            _const_spec((1, SSM_WIDTH)), _const_spec((SSM_WIDTH, SSM_WIDTH)), _const_spec((1, SSM_WIDTH)),
            _const_spec((SSM_WIDTH, D_MODEL)),
        ],
        out_specs=pl.BlockSpec((bsz, TC_SCAN, D_MODEL), lambda c: (0, c, 0)),
        out_shape=jax.ShapeDtypeStruct((bsz, seq, D_MODEL), BF16),
        scratch_shapes=[pltpu.VMEM((rows, 2 * G * N), F32), pltpu.VMEM((SUBLANES, 2 * G * N), F32)],
        compiler_params=cparams(dimension_semantics=("arbitrary",)),
        name="s5_branch",
    )(u_t, bp[0], bp[1], cp[0], cp[1], ar8, ai8,
      D_skip[0].reshape(1, SSM_WIDTH), w_glu[0].astype(BF16), b_glu[0].reshape(1, SSM_WIDTH),
      w_proj_ssm[0].astype(BF16))

    y_att = pl.pallas_call(
        _attn_kernel,
        grid=(bsz, seq // QB),
        in_specs=[
            pl.BlockSpec((1, IDX_HEADS, IDX_DIM, QB), lambda b, i: (b, 0, 0, i)),
            pl.BlockSpec((1, SUBLANES, QB), lambda b, i: (b, 0, i)),
            pl.BlockSpec((1, N_HEADS, K_EXT, QB), lambda b, i: (b, 0, 0, i)),
            pl.BlockSpec((1, seq, IDX_DIM), lambda b, i: (b, 0, 0)),
            pl.BlockSpec((1, seq, K_EXT), lambda b, i: (b, 0, 0)),
            pl.BlockSpec((1, seq // TK, HEAD_DIM, TK), lambda b, i: (b, 0, 0, 0)),
        ],
        out_specs=pl.BlockSpec((1, QB, ATTN_WIDTH), lambda b, i: (b, i, 0)),
        out_shape=jax.ShapeDtypeStruct((bsz, seq, ATTN_WIDTH), BF16),
        scratch_shapes=[pltpu.VMEM((seq // TK, TK, QB), F32),
                        pltpu.VMEM((N_HEADS, SUBLANES, QB), F32),
                        pltpu.VMEM((N_HEADS, SUBLANES, QB), F32),
                        pltpu.VMEM((N_HEADS, HEAD_DIM, QB), F32),
                        pltpu.VMEM((N_HEADS // 2, TK, 2 * QB), F32),
                        pltpu.VMEM((N_HEADS, K_EXT, QB), BF16),
                        pltpu.SMEM((1,), F32),
                        pltpu.VMEM((seq // TK, TK, QB), BF16)],
        compiler_params=cparams(dimension_semantics=("parallel", "arbitrary")),
        name="sparse_attention",
    )(qs_h, sg_t, q_h, ki_n, k_n, v_t)

    out = pl.pallas_call(
        _ffn_kernel,
        grid=(bsz, seq // TL_FFN),
        in_specs=[
            pl.BlockSpec((1, TL_FFN, D_MODEL), lambda b, i: (b, i, 0)),
            pl.BlockSpec((1, TL_FFN, D_MODEL), lambda b, i: (b, i, 0)),
            pl.BlockSpec((1, TL_FFN, ATTN_WIDTH), lambda b, i: (b, i, 0)),
            _const_spec((1, D_MODEL)),
            _const_spec((D_MODEL, 2 * D_MODEL)),
            _const_spec((ATTN_WIDTH, D_MODEL)),
            _const_spec((D_MODEL, D_MODEL)),
            _const_spec((1, D_MODEL)),
            _const_spec((D_MODEL, D_FF)), _const_spec((D_MODEL, D_FF)), _const_spec((D_FF, D_MODEL)),
        ],
        out_specs=pl.BlockSpec((1, TL_FFN, D_MODEL), lambda b, i: (b, i, 0)),
        out_shape=jax.ShapeDtypeStruct((bsz, seq, D_MODEL), F32),
        compiler_params=cparams(dimension_semantics=("parallel", "parallel")),
        name="merge_ffn",
    )(x, ps_t, y_att, g1, w_gates, w_proj_attn[0].astype(BF16),
      w_out[0].astype(BF16), norm2_g[0].reshape(1, D_MODEL), w_ffn_gate[0].astype(BF16),
      w_ffn_up[0].astype(BF16), w_ffn_down[0].astype(BF16))
    return out
```

```python
import functools
import math

import jax
import jax.numpy as jnp
from jax import lax
from jax.experimental import pallas as pl
from jax.experimental.pallas import tpu as pltpu

F32 = jnp.float32
BF16 = jnp.bfloat16

D_MODEL = 1024
SSM_WIDTH = 512
SSM_GROUP = 16
SSM_GROUPS = 32
SSM_STATE = 64
N_HEADS = 8
HEAD_DIM = 64
ATTN_WIDTH = 512
IDX_HEADS = 4
IDX_DIM = 64
INDEX_TOPK = 256
D_FF = 2816
RMS_EPS = 1e-6

LANES = 128
SUBLANES = 8
VMEM_LIMIT = 56 * 1024 * 1024

TL_IN = 512
TC_SCAN = 128
QB = 256
TK = 256
TL_FFN = 256

W1_COLS = 1536
GATE_COL0 = 1476
GATE_RB = 64
REPACK_COLS = 512
NEG_BIG = -1e30
K_EXT = 80
Q_SHIFT_ROW = 64
NORM_UP = 1.0 + 2.0 ** -7
L_FLOOR = 2.0 ** -100
assert QB % LANES == 0


def _dot(a, b):
    return jnp.dot(a, b, preferred_element_type=F32)


def _rms(x, g):
    return x * lax.rsqrt(jnp.mean(x * x, axis=-1, keepdims=True) + RMS_EPS) * g


def _disc_kernel(are_ref, aim_ref, ldt_ref, bre_ref, bim_ref, cre_ref, cim_ref,
                 ar8_ref, ai8_ref, bplo_ref, bphi_ref, cplo_ref, cphi_ref):
    G, N, P = SSM_GROUPS, SSM_STATE, SSM_GROUP
    ar = are_ref[...]
    ai = aim_ref[...]
    dt = jnp.exp(ldt_ref[...])
    mag = jnp.exp(ar * dt)
    abar_r = mag * jnp.cos(ai * dt)
    abar_i = mag * jnp.sin(ai * dt)
    den = ar * ar + ai * ai
    nr = abar_r - 1.0
    coef_r = (nr * ar + abar_i * ai) / den
    coef_i = (abar_i * ar - nr * ai) / den
    for g in range(G):
        ar8_ref[:, g * N:(g + 1) * N] = abar_r[g * P:g * P + SUBLANES]
        ai8_ref[:, g * N:(g + 1) * N] = abar_i[g * P:g * P + SUBLANES]
    br = bre_ref[...]
    bi = bim_ref[...]
    bpr = (coef_r * br - coef_i * bi).astype(BF16)
    bpi = (coef_r * bi + coef_i * br).astype(BF16)
    cre = cre_ref[...].astype(BF16)
    cim = (-cim_ref[...]).astype(BF16)

    def spread(x, reps):
        w = x.shape[1]
        r = lax.broadcasted_iota(jnp.int32, (w, w * reps), 0)
        c = lax.broadcasted_iota(jnp.int32, (w, w * reps), 1)
        return _dot(x, jnp.where(c % w == r, 1.0, 0.0).astype(BF16))

    hg = G // 2
    rb = lax.broadcasted_iota(jnp.int32, (hg * P, hg * N), 0) // P
    cb = lax.broadcasted_iota(jnp.int32, (hg * P, hg * N), 1) // N
    rc = lax.broadcasted_iota(jnp.int32, (hg * N, hg * P), 0) // N
    cc = lax.broadcasted_iota(jnp.int32, (hg * N, hg * P), 1) // P
    for h, (bp_ref, cp_ref) in enumerate(((bplo_ref, cplo_ref), (bphi_ref, cphi_ref))):
        rows_b = slice(h * hg * P, (h + 1) * hg * P)
        rows_c = slice(h * hg * N, (h + 1) * hg * N)
        bp_ref[:, 0:hg * N] = jnp.where(rb == cb, spread(bpr[rows_b], hg), 0.0).astype(BF16)
        bp_ref[:, hg * N:2 * hg * N] = jnp.where(rb == cb, spread(bpi[rows_b], hg), 0.0).astype(BF16)
        cp_ref[0:hg * N, :] = jnp.where(rc == cc, spread(cre[rows_c], hg), 0.0).astype(BF16)
        cp_ref[hg * N:2 * hg * N, :] = jnp.where(rc == cc, spread(cim[rows_c], hg), 0.0).astype(BF16)


def _inproj_kernel(x_ref, g1_ref, w_ref, qg_ref, kg_ref, ikg_ref, seg_ref,
                   u_ref, q_ref, qs_ref, sg_ref, k_ref, ki_ref, v_ref, w_scr):
    @pl.when(jnp.logical_and(pl.program_id(0) == 0, pl.program_id(1) == 0))
    def _():
        w_scr[...] = w_ref[...].T.astype(BF16)

    x = x_ref[0]
    h = _rms(x, g1_ref[...]).astype(BF16)
    proj = _dot(h, w_scr[...])
    u_ref[0] = proj[:, 0:512]
    seg = seg_ref[...]

    def head_rms(v, g):
        sq = (v * v).astype(BF16)
        n = v.shape[1]
        w = min(n, seg.shape[0])
        ss = jnp.concatenate([_dot(sq[:, c:c + w], seg[0:w, 0:w]) for c in range(0, n, w)], axis=1)
        return v * lax.rsqrt(ss * (1.0 / HEAD_DIM) + RMS_EPS) * g

    def store_heads_transposed(ref, v, with_shift_rows=False):
        for g in range(v.shape[1] // LANES):
            tg = v[:, g * LANES:(g + 1) * LANES].T
            for e in range(2):
                th = tg[e * HEAD_DIM:(e + 1) * HEAD_DIM].astype(BF16)
                ref[0, 2 * g + e, 0:HEAD_DIM] = th
                if with_shift_rows:
                    tf = th.astype(F32)
                    nrm = jnp.sqrt(jnp.sum(tf * tf, axis=0, keepdims=True)) * NORM_UP
                    pad = (K_EXT - HEAD_DIM, th.shape[1])
                    first = lax.broadcasted_iota(jnp.int32, pad, 0) == 0
                    ref[0, 2 * g + e, HEAD_DIM:K_EXT] = jnp.where(first, -nrm, 0.0).astype(BF16)

    qscale = (HEAD_DIM ** -0.5) * math.log2(math.e)
    store_heads_transposed(q_ref, head_rms(proj[:, 512:1024], qg_ref[...]) * qscale, with_shift_rows=True)

    kv = proj[:, 1024:1152]
    kiw = proj[:, 1408:1536]
    w_scale = (IDX_HEADS ** -0.5) * (IDX_DIM ** -0.5)
    wabs = jnp.abs(kiw) * w_scale
    sshape = (LANES, IDX_HEADS * IDX_DIM)
    row = lax.broadcasted_iota(jnp.int32, sshape, 0)
    col = lax.broadcasted_iota(jnp.int32, sshape, 1)
    spread = jnp.where(row == IDX_DIM + col // IDX_DIM, 1.0, 0.0).astype(BF16)
    whi = wabs.astype(BF16)
    wlo = (wabs - whi.astype(F32)).astype(BF16)
    wrep = _dot(whi, spread) + _dot(wlo, spread)
    store_heads_transposed(qs_ref, proj[:, 1152:1408] * wrep)

    lane = lax.broadcasted_iota(jnp.int32, kv.shape, 1)
    k_ref[0] = jnp.where(lane < HEAD_DIM, head_rms(kv, kg_ref[...]), 0.0)[:, 0:K_EXT].astype(BF16)
    ki_ref[0] = head_rms(kiw, ikg_ref[...])[:, 0:IDX_DIM].astype(BF16)
    t_kv = kv.T
    t_kiw = kiw.T
    sg_ref[0] = jnp.where(t_kiw[IDX_DIM:IDX_DIM + SUBLANES] >= 0, 1.0, -1.0)
    for c in range(TL_IN // TK):
        v_ref[0, c] = t_kv[HEAD_DIM:2 * HEAD_DIM, c * TK:(c + 1) * TK].astype(BF16)


def _s5_kernel(u_ref, bplo_ref, bphi_ref, cplo_ref, cphi_ref, ar_ref, ai_ref, d_ref, wglu_ref, bglu_ref,
               wps_ref, o_ref, bu_scr, st_scr):
    @pl.when(pl.program_id(0) == 0)
    def _():
        st_scr[...] = jnp.zeros_like(st_scr)

    u = jnp.transpose(u_ref[...], (1, 0, 2)).reshape(SUBLANES * TC_SCAN, SSM_WIDTH)
    ub = u.astype(BF16)
    half_w = 2 * (SSM_GROUPS // 2) * SSM_STATE
    n_re = half_w // 2

    def scan_half(h):
        c0 = h * half_w
        a_r = ar_ref[:, h * n_re:(h + 1) * n_re]
        a_i = ai_ref[:, h * n_re:(h + 1) * n_re]
        sr = st_scr[:, c0:c0 + n_re]
        si = st_scr[:, c0 + n_re:c0 + half_w]
        for t in range(TC_SCAN):
            rows = slice(t * SUBLANES, (t + 1) * SUBLANES)
            br = bu_scr[rows, c0:c0 + n_re]
            bi = bu_scr[rows, c0 + n_re:c0 + half_w]
            sr, si = a_r * sr - a_i * si + br, a_r * si + a_i * sr + bi
            bu_scr[rows, c0:c0 + n_re] = sr
            bu_scr[rows, c0 + n_re:c0 + half_w] = si
        st_scr[:, c0:c0 + n_re] = sr
        st_scr[:, c0 + n_re:c0 + half_w] = si

    bu_scr[:, 0:half_w] = _dot(ub[:, 0:256], bplo_ref[...])
    bu_scr[:, half_w:2 * half_w] = _dot(ub[:, 256:512], bphi_ref[...])
    scan_half(0)
    y_lo = _dot(bu_scr[:, 0:half_w].astype(BF16), cplo_ref[...])
    scan_half(1)
    y_hi = _dot(bu_scr[:, half_w:2 * half_w].astype(BF16), cphi_ref[...])
    y = jnp.concatenate([y_lo, y_hi], axis=1)
    y = jax.nn.gelu(y + d_ref[...] * u)
    z = _dot(y.astype(BF16), wglu_ref[...]) + bglu_ref[...]
    y = y * jax.nn.sigmoid(z)
    ps = _dot(y.astype(BF16), wps_ref[...])
    o_ref[...] = jnp.transpose(ps.reshape(TC_SCAN, SUBLANES, D_MODEL), (1, 0, 2)).astype(BF16)


def _key_to_f32(key):
    bits = jnp.where(key < 0, key & jnp.int32(0x7FFFFFFF), ~key)
    return pltpu.bitcast(bits, F32)


def _attn_kernel(qs_ref, sg_ref, q_ref, ki_ref, k_ref, vt_ref, o_ref,
                 s_scr, m_scr, l_scr, acc_scr, sp_scr, qmod_scr, kmax_scr, sb_scr):
    i = pl.program_id(1)
    n_tiles = (i * QB + QB + TK - 1) // TK
    nv = TK // SUBLANES
    shape3 = (nv, SUBLANES, QB)
    k_in_tile = (lax.broadcasted_iota(jnp.int32, shape3, 0) * SUBLANES
                 + lax.broadcasted_iota(jnp.int32, shape3, 1))
    q_pos = i * QB + lax.broadcasted_iota(jnp.int32, shape3, 2)
    ksel = float(INDEX_TOPK)

    def all_sublanes(a, op):
        for sh in (4, 2, 1):
            a = op(a, pltpu.roll(a, sh, 0))
        return a

    sg = sg_ref[0]

    def idx_body(j, carry):
        kt = ki_ref[0, pl.ds(pl.multiple_of(j * TK, TK), TK), :]
        acc = jnp.zeros((TK, QB), F32)
        for pr in range(IDX_HEADS // 2):
            x = _dot(kt, jnp.concatenate([qs_ref[0, 2 * pr], qs_ref[0, 2 * pr + 1]], axis=1))
            for e in range(2):
                hd = 2 * pr + e
                acc = acc + sg[hd:hd + 1, :] * jnp.maximum(x[:, e * QB:(e + 1) * QB], 0.0)
        vis = j * TK + k_in_tile <= q_pos
        sc = jnp.where(vis, acc.reshape(shape3), -jnp.inf).reshape(TK, QB)
        s_scr[j] = sc
        sb_scr[j] = sc.astype(BF16)
        return carry

    def for_each_tile(body):
        def pair(jj, carry):
            body(2 * jj, carry)
            return body(2 * jj + 1, carry)
        lax.fori_loop(0, n_tiles // 2, pair, 0)

        @pl.when(n_tiles % 2 == 1)
        def _():
            body(n_tiles - 1, 0)

    for_each_tile(idx_body)

    def count(pred):
        def tile(j, acc):
            hit = jnp.where(pred(s_scr[j].reshape(shape3), j), 1.0, 0.0)
            return acc + jnp.sum(hit.reshape(nv // 4, 4, SUBLANES, QB), axis=0)
        acc = lax.fori_loop(0, n_tiles // 2, lambda jj, a: tile(2 * jj + 1, tile(2 * jj, a)),
                            jnp.zeros((4, SUBLANES, QB), F32))
        acc = lax.cond(n_tiles % 2 == 1, lambda a: tile(n_tiles - 1, a), lambda a: a, acc)
        return all_sublanes(jnp.sum(acc, axis=0), jnp.add)

    every = float(2 * INDEX_TOPK * 1024)
    nv16 = TK // (2 * SUBLANES)
    one_b, zero_b = jnp.ones((), BF16), jnp.zeros((), BF16)

    def count_hi(gc):
        def tile(j, acc):
            hit = jnp.where(sb_scr[j].reshape(nv16, 2 * SUBLANES, QB) >= gc, one_b, zero_b)
            for k in range(nv16):
                acc[k % 4] = acc[k % 4] + hit[k]
            return acc
        acc = lax.fori_loop(0, n_tiles // 2, lambda jj, a: tuple(tile(2 * jj + 1, tile(2 * jj, list(a)))),
                            tuple(jnp.zeros((2 * SUBLANES, QB), BF16) for _ in range(4)))
        acc = lax.cond(n_tiles % 2 == 1, lambda a: tuple(tile(n_tiles - 1, list(a))), lambda a: a, acc)
        tot = (acc[0].astype(F32) + acc[1].astype(F32)) + (acc[2].astype(F32) + acc[3].astype(F32))
        return all_sublanes(tot[0:SUBLANES] + tot[SUBLANES:2 * SUBLANES], jnp.add)

    def grid_key(k16):
        return lax.shift_left(k16, 16) | jnp.where(k16 < 0x8000, 0xFFFF, 0)

    def hi_body(it, k16):
        cand = k16 | lax.shift_left(jnp.int32(1), 15 - it)
        g = _key_to_f32(grid_key(cand))
        n = count_hi(jnp.concatenate([g, g], axis=0).astype(BF16))
        cnt = jnp.where(lax.shift_right_logical(cand, 7) == 0, every, n)
        return jnp.where(cnt >= ksel, cand, k16)

    def lo_body(_, carry):
        lo, hi, cnt_lo = carry
        mid = lo + lax.shift_right_logical(hi - lo, 1)
        tc = _key_to_f32(mid)
        cnt = jnp.where(lax.shift_right_logical(mid, 23) == 0, every, count(lambda s, j: s >= tc))
        ok = cnt >= ksel
        return jnp.where(ok, mid, lo), jnp.where(ok, hi, mid), jnp.where(ok, cnt, cnt_lo)

    def write_bias(select):
        def body(j, carry):
            s = s_scr[j].reshape(shape3)
            vis = j * TK + k_in_tile <= q_pos
            bias = jnp.where(vis, jnp.where(select(s, j), 0.0, -jnp.inf), -jnp.inf)
            s_scr[j] = bias.reshape(TK, QB)
            return carry
        for_each_tile(body)

    few_keys = (i + 1) * QB <= INDEX_TOPK

    @pl.when(few_keys)
    def _():
        write_bias(lambda s, j: jnp.ones(s.shape, jnp.bool_))

    @pl.when(jnp.logical_not(few_keys))
    def _():
        k16 = lax.fori_loop(0, 16, hi_body, jnp.zeros((SUBLANES, QB), jnp.int32))
        key, _, cnt_ge = lax.fori_loop(
            0, 17, lo_body,
            (grid_key(k16) - 0x8001, grid_key(jnp.minimum(k16 + 1, 0xFFFF)),
             jnp.full((SUBLANES, QB), every, F32)))
        thr = _key_to_f32(key)
        has_excess = jnp.max(cnt_ge) > ksel

        @pl.when(jnp.logical_not(has_excess))
        def _():
            write_bias(lambda s, j: s >= thr)

        @pl.when(has_excess)
        def _():
            cnt_gt = count(lambda s, j: s > thr)
            need = ksel - cnt_gt

            def jbit_body(it, jkey):
                cand = jkey | lax.shift_left(jnp.int32(1), 10 - it)
                cnt = count(lambda s, j: (s == thr) & (j * TK + k_in_tile < cand))
                return jnp.where(cnt < need, cand, jkey)

            jkey = lax.fori_loop(0, 11, jbit_body, jnp.zeros((SUBLANES, QB), jnp.int32))
            write_bias(lambda s, j: (s > thr) | ((s == thr) & (j * TK + k_in_tile <= jkey)))

    lane_k = lax.broadcasted_iota(jnp.int32, (TK, K_EXT), 1)

    def attend(get_q, k_fill):
        l_scr[...] = jnp.zeros(l_scr.shape, F32)
        acc_scr[...] = jnp.zeros(acc_scr.shape, F32)

        def body(j, carry):
            kt = k_ref[0, pl.ds(pl.multiple_of(j * TK, TK), TK), :]
            kt = jnp.where(lane_k == Q_SHIFT_ROW, jnp.full((TK, K_EXT), k_fill, F32).astype(BF16), kt)
            vt = vt_ref[0, j]
            bias = s_scr[j].reshape(shape3)
            for pr in range(N_HEADS // 2):
                sp_scr[pr] = _dot(kt, jnp.concatenate([get_q(2 * pr), get_q(2 * pr + 1)], axis=1))
            for pr in range(N_HEADS // 2):
                for e in range(2):
                    hd = 2 * pr + e
                    p = jnp.exp2(sp_scr[pr, :, e * QB:(e + 1) * QB].reshape(shape3) + bias)
                    l_scr[hd] = l_scr[hd] + jnp.sum(p, axis=0)
                    acc_scr[hd] = acc_scr[hd] + _dot(vt, p.reshape(TK, QB).astype(BF16))
            return carry

        for_each_tile(body)

    @pl.when(i == 0)
    def _():
        kf = k_ref[0].astype(F32)
        kmax_scr[0] = jnp.max(jnp.sqrt(jnp.sum(kf * kf, axis=1, keepdims=True))) * NORM_UP

    attend(lambda hd: q_ref[0, hd], kmax_scr[0])

    l_min = all_sublanes(l_scr[0], jnp.add)
    for hd in range(1, N_HEADS):
        l_min = jnp.minimum(l_min, all_sublanes(l_scr[hd], jnp.add))
    underflow = jnp.logical_not(jnp.min(l_min) >= L_FLOOR)

    @pl.when(underflow)
    def _():
        row_q = lax.broadcasted_iota(jnp.int32, (K_EXT, QB), 0)
        m_scr[...] = jnp.full(m_scr.shape, NEG_BIG, F32)

        def max_body(j, carry):
            kt = k_ref[0, pl.ds(pl.multiple_of(j * TK, TK), TK), :]
            kt = jnp.where(lane_k == Q_SHIFT_ROW, jnp.zeros((TK, K_EXT), BF16), kt)
            bias = s_scr[j].reshape(shape3)
            for hd in range(N_HEADS):
                s = _dot(kt, q_ref[0, hd]).reshape(shape3) + bias
                m_scr[hd] = jnp.maximum(m_scr[hd], all_sublanes(jnp.max(s, axis=0), jnp.maximum))
            return carry

        lax.fori_loop(0, n_tiles, max_body, 0)
        for hd in range(N_HEADS):
            shift = jnp.broadcast_to(-m_scr[hd][0:1, :], (K_EXT, QB)).astype(BF16)
            qmod_scr[hd] = jnp.where(row_q == Q_SHIFT_ROW, shift, q_ref[0, hd])
        attend(lambda hd: qmod_scr[hd], 1.0)

    outs = []
    for hd in range(N_HEADS):
        l = all_sublanes(l_scr[hd], jnp.add)
        o = acc_scr[hd].reshape(HEAD_DIM // SUBLANES, SUBLANES, QB) / l
        outs.append(o.reshape(HEAD_DIM, QB))
    o_ref[0] = jnp.concatenate(outs, axis=0).T.astype(BF16)


def _ffn_kernel(x_ref, ps_ref, ya_ref, g1_ref, wgate_ref, wpa_ref, wo_ref, g2_ref, wfg_ref, wfu_ref, wfd_ref,
                o_ref):
    x = x_ref[0]
    h = _rms(x, g1_ref[...]).astype(BF16)
    gates = _dot(h, wgate_ref[...])
    pa = _dot(ya_ref[0], wpa_ref[...])
    merged = (jax.nn.sigmoid(gates[:, 0:D_MODEL]) * ps_ref[0].astype(F32)
              + jax.nn.sigmoid(gates[:, D_MODEL:2 * D_MODEL]) * pa)
    x1 = x + _dot(merged.astype(BF16), wo_ref[...])
    h2 = _rms(x1, g2_ref[...]).astype(BF16)
    hid = jax.nn.silu(_dot(h2, wfg_ref[...])) * _dot(h2, wfu_ref[...])
    o_ref[0] = x1 + _dot(hid.astype(BF16), wfd_ref[...])


def _gate_repack_kernel(*refs):
    *in_refs, o_ref = refs
    off = GATE_COL0 % GATE_RB
    parts = [in_refs[0][off:]] + [r[...] for r in in_refs[1:-1]] + [in_refs[-1][:off]]
    o_ref[...] = jnp.concatenate(parts, axis=0).T.astype(BF16)


def _const_spec(shape):
    nd = len(shape)
    return pl.BlockSpec(shape, lambda *_: (0,) * nd, pipeline_mode=pl.Buffered(1))


def kernel(x, norm1_g, w_in, A_re, A_im, log_dt, B_re, B_im, C_re, C_im, D_skip, w_glu, b_glu, q_norm_g, k_norm_g,
           idx_k_norm_g, w_proj_ssm, w_proj_attn, w_out, norm2_g, w_ffn_gate, w_ffn_up, w_ffn_down):
    bsz, seq, _ = x.shape
    assert x.shape == (8, 2048, D_MODEL) and w_in.shape[0] == 1
    G, N, P = SSM_GROUPS, SSM_STATE, SSM_GROUP
    cparams = functools.partial(pltpu.CompilerParams, vmem_limit_bytes=VMEM_LIMIT)

    w_in_t = w_in[0].T
    gblk = GATE_COL0 // GATE_RB
    nrb = REPACK_COLS // GATE_RB
    spec = lambda k: pl.BlockSpec((GATE_RB, D_MODEL), lambda j: (gblk + nrb * j + k, 0))
    w_gates = pl.pallas_call(
        _gate_repack_kernel,
        grid=(2 * D_MODEL // REPACK_COLS,),
        in_specs=[spec(k) for k in range(nrb + 1)],
        out_specs=pl.BlockSpec((D_MODEL, REPACK_COLS), lambda j: (0, j)),
        out_shape=jax.ShapeDtypeStruct((D_MODEL, 2 * D_MODEL), BF16),
        name="gate_weight_repack",
    )(*([w_in_t] * (nrb + 1)))
    g1 = norm1_g[0].reshape(1, D_MODEL)
    ones64 = jnp.ones((HEAD_DIM,), F32)

    rep = lambda a: jnp.repeat(a, P, axis=0)
    hgn, hgp = G // 2 * N, G // 2 * P
    ar8, ai8, bp_lo, bp_hi, cp_lo, cp_hi = pl.pallas_call(
        _disc_kernel,
        out_shape=(jax.ShapeDtypeStruct((SUBLANES, G * N), F32), jax.ShapeDtypeStruct((SUBLANES, G * N), F32),
                   jax.ShapeDtypeStruct((hgp, 2 * hgn), BF16), jax.ShapeDtypeStruct((hgp, 2 * hgn), BF16),
                   jax.ShapeDtypeStruct((2 * hgn, hgp), BF16), jax.ShapeDtypeStruct((2 * hgn, hgp), BF16)),
        name="s5_discretise",
    )(rep(A_re[0]), rep(A_im[0]), rep(jnp.broadcast_to(log_dt[0].reshape(G, 1), (G, N))),
      B_re[0].transpose(0, 2, 1).reshape(G * P, N), B_im[0].transpose(0, 2, 1).reshape(G * P, N),
      C_re[0].transpose(0, 2, 1).reshape(G * N, P), C_im[0].transpose(0, 2, 1).reshape(G * N, P))
    bp, cp = (bp_lo, bp_hi), (cp_lo, cp_hi)

    n_in = seq // TL_IN
    u_t, q_h, qs_h, sg_t, k_n, ki_n, v_t = pl.pallas_call(
        _inproj_kernel,
        grid=(bsz, n_in),
        in_specs=[
            pl.BlockSpec((1, TL_IN, D_MODEL), lambda b, i: (b, i, 0)),
            _const_spec((1, D_MODEL)),
            _const_spec((W1_COLS, D_MODEL)),
            _const_spec((1, ATTN_WIDTH)), _const_spec((1, LANES)), _const_spec((1, LANES)),
            _const_spec((2 * LANES, 2 * LANES)),
        ],
        out_specs=[
            pl.BlockSpec((1, TL_IN, SSM_WIDTH), lambda b, i: (b, i, 0)),
            pl.BlockSpec((1, N_HEADS, K_EXT, TL_IN), lambda b, i: (b, 0, 0, i)),
            pl.BlockSpec((1, IDX_HEADS, IDX_DIM, TL_IN), lambda b, i: (b, 0, 0, i)),
            pl.BlockSpec((1, SUBLANES, TL_IN), lambda b, i: (b, 0, i)),
            pl.BlockSpec((1, TL_IN, K_EXT), lambda b, i: (b, i, 0)),
            pl.BlockSpec((1, TL_IN, IDX_DIM), lambda b, i: (b, i, 0)),
            pl.BlockSpec((1, TL_IN // TK, HEAD_DIM, TK), lambda b, i: (b, i, 0, 0)),
        ],
        out_shape=(
            jax.ShapeDtypeStruct((bsz, seq, SSM_WIDTH), F32),
            jax.ShapeDtypeStruct((bsz, N_HEADS, K_EXT, seq), BF16),
            jax.ShapeDtypeStruct((bsz, IDX_HEADS, IDX_DIM, seq), BF16),
            jax.ShapeDtypeStruct((bsz, SUBLANES, seq), F32),
            jax.ShapeDtypeStruct((bsz, seq, K_EXT), BF16),
            jax.ShapeDtypeStruct((bsz, seq, IDX_DIM), BF16),
            jax.ShapeDtypeStruct((bsz, seq // TK, HEAD_DIM, TK), BF16),
        ),
        scratch_shapes=[pltpu.VMEM((D_MODEL, W1_COLS), BF16)],
        compiler_params=cparams(dimension_semantics=("arbitrary", "arbitrary")),
        name="in_projection",
    )(x, g1, w_in_t, jnp.tile(q_norm_g[0], N_HEADS).reshape(1, ATTN_WIDTH),
      jnp.concatenate([k_norm_g[0], ones64]).reshape(1, LANES),
      jnp.concatenate([idx_k_norm_g[0], ones64]).reshape(1, LANES),
      jnp.kron(jnp.eye(2 * LANES // HEAD_DIM, dtype=BF16), jnp.ones((HEAD_DIM, HEAD_DIM), BF16)))

    rows = SUBLANES * TC_SCAN
    ps_t = pl.pallas_call(
        _s5_kernel,
        grid=(seq // TC_SCAN,),
        in_specs=[
            pl.BlockSpec((bsz, TC_SCAN, SSM_WIDTH), lambda c: (0, c, 0)),
            _const_spec((256, 2048)), _const_spec((256, 2048)),
            _const_spec((2048, 256)), _const_spec((2048, 256)),
            _const_spec((SUBLANES, G * N)), _const_spec((SUBLANES, G * N)),
            _const_spec((1, SSM_WIDTH)), _const_spec((SSM_WIDTH, SSM_WIDTH)), _const_spec((1, SSM_WIDTH)),
            _const_spec((SSM_WIDTH, D_MODEL)),
        ],
        out_specs=pl.BlockSpec((bsz, TC_SCAN, D_MODEL), lambda c: (0, c, 0)),
        out_shape=jax.ShapeDtypeStruct((bsz, seq, D_MODEL), BF16),
        scratch_shapes=[pltpu.VMEM((rows, 2 * G * N), F32), pltpu.VMEM((SUBLANES, 2 * G * N), F32)],
        compiler_params=cparams(dimension_semantics=("arbitrary",)),
        name="s5_branch",
    )(u_t, bp[0], bp[1], cp[0], cp[1], ar8, ai8,
      D_skip[0].reshape(1, SSM_WIDTH), w_glu[0].astype(BF16), b_glu[0].reshape(1, SSM_WIDTH),
      w_proj_ssm[0].astype(BF16))

    y_att = pl.pallas_call(
        _attn_kernel,
        grid=(bsz, seq // QB),
        in_specs=[
            pl.BlockSpec((1, IDX_HEADS, IDX_DIM, QB), lambda b, i: (b, 0, 0, i)),
            pl.BlockSpec((1, SUBLANES, QB), lambda b, i: (b, 0, i)),
            pl.BlockSpec((1, N_HEADS, K_EXT, QB), lambda b, i: (b, 0, 0, i)),
            pl.BlockSpec((1, seq, IDX_DIM), lambda b, i: (b, 0, 0)),
            pl.BlockSpec((1, seq, K_EXT), lambda b, i: (b, 0, 0)),
            pl.BlockSpec((1, seq // TK, HEAD_DIM, TK), lambda b, i: (b, 0, 0, 0)),
        ],
        out_specs=pl.BlockSpec((1, QB, ATTN_WIDTH), lambda b, i: (b, i, 0)),
        out_shape=jax.ShapeDtypeStruct((bsz, seq, ATTN_WIDTH), BF16),
        scratch_shapes=[pltpu.VMEM((seq // TK, TK, QB), F32),
                        pltpu.VMEM((N_HEADS, SUBLANES, QB), F32),
                        pltpu.VMEM((N_HEADS, SUBLANES, QB), F32),
                        pltpu.VMEM((N_HEADS, HEAD_DIM, QB), F32),
                        pltpu.VMEM((N_HEADS // 2, TK, 2 * QB), F32),
                        pltpu.VMEM((N_HEADS, K_EXT, QB), BF16),
                        pltpu.SMEM((1,), F32),
                        pltpu.VMEM((seq // TK, TK, QB), BF16)],
        compiler_params=cparams(dimension_semantics=("parallel", "arbitrary")),
        name="sparse_attention",
    )(qs_h, sg_t, q_h, ki_n, k_n, v_t)

    out = pl.pallas_call(
        _ffn_kernel,
        grid=(bsz, seq // TL_FFN),
        in_specs=[
            pl.BlockSpec((1, TL_FFN, D_MODEL), lambda b, i: (b, i, 0)),
            pl.BlockSpec((1, TL_FFN, D_MODEL), lambda b, i: (b, i, 0)),
            pl.BlockSpec((1, TL_FFN, ATTN_WIDTH), lambda b, i: (b, i, 0)),
            _const_spec((1, D_MODEL)),
            _const_spec((D_MODEL, 2 * D_MODEL)),
            _const_spec((ATTN_WIDTH, D_MODEL)),
            _const_spec((D_MODEL, D_MODEL)),
            _const_spec((1, D_MODEL)),
            _const_spec((D_MODEL, D_FF)), _const_spec((D_MODEL, D_FF)), _const_spec((D_FF, D_MODEL)),
        ],
        out_specs=pl.BlockSpec((1, TL_FFN, D_MODEL), lambda b, i: (b, i, 0)),
        out_shape=jax.ShapeDtypeStruct((bsz, seq, D_MODEL), F32),
        compiler_params=cparams(dimension_semantics=("parallel", "parallel")),
        name="merge_ffn",
    )(x, ps_t, y_att, g1, w_gates, w_proj_attn[0].astype(BF16),
      w_out[0].astype(BF16), norm2_g[0].reshape(1, D_MODEL), w_ffn_gate[0].astype(BF16),
      w_ffn_up[0].astype(BF16), w_ffn_down[0].astype(BF16))
    return out
```
